```python
import jax, jax.numpy as jnp
from jax import lax
import numpy as np

D_MODEL = 1024
BATCH = 16
SEQ = 4096
DEPTH = 1

HEAD_DIM = 64
N_HEADS_NSA = 8
N_KV_NSA = 2
GROUP = N_HEADS_NSA // N_KV_NSA
N_HEADS_FOX = 8
D_NSA = N_HEADS_NSA * HEAD_DIM
D_KV_NSA = N_KV_NSA * HEAD_DIM
D_FOX = N_HEADS_FOX * HEAD_DIM
CMP_BLOCK = 32
CMP_STRIDE = 16
CMP_RATIO = CMP_BLOCK // CMP_STRIDE
CMP_HIDDEN = 128
SLC_BLOCK = 64
TOP_N = 16
WINDOW = 512
NSA_Q_BLOCK = 64
Q_BLOCK = 128
ROPE_THETA = 500000.0
ROPE_DIM = HEAD_DIM // 4
D_FF = 2816
N_MOD = 9
RMS_EPS = 1e-6
NEG_INF = -1e30
FORCE_SCORE = 1e4
IN_SIZES = (D_NSA, D_KV_NSA, D_KV_NSA, D_KV_NSA, D_KV_NSA, D_KV_NSA, D_KV_NSA,
            3 * N_HEADS_NSA, D_FOX, D_FOX, D_FOX, N_HEADS_FOX, D_MODEL, D_MODEL)
D_IN = sum(IN_SIZES)

kernel_name = "macaron_nsa_fox_hybrid_adaln"


def rms_norm(x, g):
    xf = x.astype(jnp.float32)
    y = xf * lax.rsqrt(jnp.mean(xf * xf, axis=-1, keepdims=True) + RMS_EPS)
    return (y * g.astype(jnp.float32)).astype(x.dtype)


def modulate(h, shift, scale):
    return h * (1 + scale[:, None, :]) + shift[:, None, :]


def swiglu(h, w_gate, w_up, w_down):
    return (jax.nn.silu(h @ w_gate) * (h @ w_up)) @ w_down


def partial_rope(x, positions):
    half = ROPE_DIM // 2
    inv_freq = ROPE_THETA ** (-jnp.arange(half, dtype=jnp.float32) / half)
    ang = positions.astype(jnp.float32)[..., None] * inv_freq
    cos = jnp.cos(ang)[:, :, None, :]
    sin = jnp.sin(ang)[:, :, None, :]
    xr = x[..., :ROPE_DIM].astype(jnp.float32)
    x1, x2 = xr[..., :half], xr[..., half:]
    rot = jnp.concatenate([x1 * cos - x2 * sin, x2 * cos + x1 * sin], axis=-1).astype(x.dtype)
    return jnp.concatenate([rot, x[..., ROPE_DIM:]], axis=-1)


def masked_softmax(s, mask):
    p = jax.nn.softmax(jnp.where(mask, s, NEG_INF), axis=-1)
    return jnp.where(mask, p, 0.0)


def compress_blocks(kv, pe, w1, w2):
    B, T, G, dh = kv.shape
    n_sub = T // CMP_STRIDE
    n_cmp = n_sub - CMP_RATIO + 1
    sub = kv.reshape(B, n_sub, CMP_STRIDE, G, dh)
    blocks = jnp.concatenate([sub[:, i:i + n_cmp] for i in range(CMP_RATIO)], axis=2)
    blocks = blocks + pe[None, None, :, None, :]
    flat = blocks.transpose(0, 1, 3, 2, 4).reshape(B, n_cmp, G, CMP_BLOCK * dh)
    return jax.nn.gelu(flat @ w1) @ w2


def nsa_attention(q, k_cmp, v_cmp, k_slc, v_slc, k_win, v_win, gates,
                  pe_ck, w1_ck, w2_ck, pe_cv, w1_cv, w2_cv):
    B, T, H, dh = q.shape
    G = k_cmp.shape[2]
    scale = dh ** -0.5
    kc = compress_blocks(k_cmp, pe_ck, w1_ck, w2_ck)
    vc = compress_blocks(v_cmp, pe_cv, w1_cv, w2_cv)
    n_cmp = kc.shape[1]
    n_slc = T // SLC_BLOCK
    n_sel = min(TOP_N, n_slc)
    cmp_start = jnp.arange(n_cmp) * CMP_STRIDE
    cmp_end = cmp_start + CMP_BLOCK - 1
    slc_start = jnp.arange(n_slc) * SLC_BLOCK
    overlap = ((cmp_start[:, None] < slc_start[None, :] + SLC_BLOCK)
               & (slc_start[None, :] < cmp_start[:, None] + CMP_BLOCK)).astype(jnp.float32)
    ks_blocks = k_slc.reshape(B, n_slc, SLC_BLOCK, G, dh).transpose(0, 3, 1, 2, 4)
    vs_blocks = v_slc.reshape(B, n_slc, SLC_BLOCK, G, dh).transpose(0, 3, 1, 2, 4)
    kw_pad = jnp.pad(k_win, ((0, 0), (WINDOW, 0), (0, 0), (0, 0)))
    vw_pad = jnp.pad(v_win, ((0, 0), (WINDOW, 0), (0, 0), (0, 0)))
    qg = q.reshape(B, T, G, GROUP, dh)
    b_idx = jnp.arange(B)[:, None, None, None]
    g_idx = jnp.arange(G)[None, :, None, None]
    blk = jnp.arange(n_slc)
    span = NSA_Q_BLOCK + WINDOW
    Q = NSA_Q_BLOCK

    def block(i):
        q0 = i * Q
        qb = lax.dynamic_slice_in_dim(qg, q0, Q, axis=1)
        gb = lax.dynamic_slice_in_dim(gates, q0, Q, axis=1).reshape(B, Q, G, GROUP, 3)
        t = q0 + jnp.arange(Q)
        s = jnp.einsum('bqgrd,bcgd->bgrqc', qb, kc, preferred_element_type=jnp.float32) * scale
        p_cmp = masked_softmax(s, cmp_end[None, :] <= t[:, None])
        o_cmp = jnp.einsum('bgrqc,bcgd->bqgrd', p_cmp, vc)
        imp = jnp.einsum('bgrqc,cn->bgqn', p_cmp, overlap)
        cur = t // SLC_BLOCK
        visible = blk[None, :] <= cur[:, None]
        forced = (blk[None, :] == 0) | (blk[None, :] == cur[:, None]) | (blk[None, :] == cur[:, None] - 1)
        imp = jnp.where(forced, FORCE_SCORE, jnp.where(visible, imp, -1.0))
        _, idx = lax.top_k(imp, n_sel)
        ks = ks_blocks[b_idx, g_idx, idx]
        vs = vs_blocks[b_idx, g_idx, idx].reshape(B, G, Q, n_sel * SLC_BLOCK, dh)
        s = jnp.einsum('bqgrd,bgqnld->bgrqnl', qb, ks, preferred_element_type=jnp.float32) * scale
        s = s.reshape(B, G, GROUP, Q, n_sel * SLC_BLOCK)
        key_pos = idx[..., None] * SLC_BLOCK + jnp.arange(SLC_BLOCK)
        mask_s = (key_pos <= t[None, None, :, None, None]).reshape(B, G, 1, Q, n_sel * SLC_BLOCK)
        p = masked_softmax(s, mask_s)
        o_slc = jnp.einsum('bgrqm,bgqmd->bqgrd', p, vs)
        kw = lax.dynamic_slice_in_dim(kw_pad, q0, span, axis=1)
        vw = lax.dynamic_slice_in_dim(vw_pad, q0, span, axis=1)
        s = jnp.einsum('bqgrd,bkgd->bgrqk', qb, kw, preferred_element_type=jnp.float32) * scale
        kpos = q0 - WINDOW + jnp.arange(span)
        mask_w = (kpos[None, :] >= 0) & (kpos[None, :] <= t[:, None]) & (t[:, None] - kpos[None, :] < WINDOW)
        p = masked_softmax(s, mask_w)
        o_win = jnp.einsum('bgrqk,bkgd->bqgrd', p, vw)
        o = gb[..., 0:1] * o_cmp + gb[..., 1:2] * o_slc + gb[..., 2:3] * o_win
        return o.reshape(B, Q, H * dh).astype(q.dtype)

    out = lax.map(block, jnp.arange(T // Q))
    return out.transpose(1, 0, 2, 3).reshape(B, T, H * dh)


def forgetting_attention(q, k, v, log_f):
    B, T, H, dh = q.shape
    scale = dh ** -0.5
    cf = lax.cumsum(log_f, axis=1).transpose(0, 2, 1)
    kpos = jnp.arange(T)

    def block(i):
        q0 = i * Q_BLOCK
        qb = lax.dynamic_slice_in_dim(q, q0, Q_BLOCK, axis=1)
        cq = lax.dynamic_slice_in_dim(cf, q0, Q_BLOCK, axis=2)
        t = q0 + jnp.arange(Q_BLOCK)
        s = jnp.einsum('bqhd,bkhd->bhqk', qb, k, preferred_element_type=jnp.float32) * scale
        s = s + (cq[..., :, None] - cf[..., None, :])
        p = masked_softmax(s, kpos[None, :] <= t[:, None])
        o = jnp.einsum('bhqk,bkhd->bqhd', p, v)
        return o.reshape(B, Q_BLOCK, H * dh).astype(q.dtype)

    out = lax.map(block, jnp.arange(T // Q_BLOCK))
    return out.transpose(1, 0, 2, 3).reshape(B, T, H * dh)


def hybrid_mixer(h, positions, w_in, b_forget, pe_ck, w1_ck, w2_ck, pe_cv, w1_cv, w2_cv,
                 w_up_nsa, w_up_fox, w_o):
    B, T, _ = h.shape
    split_points = [int(s) for s in np.cumsum(IN_SIZES)[:-1]]
    (q_n, k_c, v_c, k_s, v_s, k_w, v_w, gate_n, q_f, k_f, v_f, f_logit,
     gm_nsa, gm_fox) = jnp.split(h @ w_in, split_points, axis=-1)

    def heads(a, n):
        return a.reshape(B, T, n, HEAD_DIM)

    o_nsa = nsa_attention(
        partial_rope(heads(q_n, N_HEADS_NSA), positions),
        partial_rope(heads(k_c, N_KV_NSA), positions), heads(v_c, N_KV_NSA),
        partial_rope(heads(k_s, N_KV_NSA), positions), heads(v_s, N_KV_NSA),
        partial_rope(heads(k_w, N_KV_NSA), positions), heads(v_w, N_KV_NSA),
        jax.nn.sigmoid(gate_n.reshape(B, T, N_HEADS_NSA, 3)),
        pe_ck, w1_ck, w2_ck, pe_cv, w1_cv, w2_cv)
    log_f = jax.nn.log_sigmoid((f_logit + b_forget).astype(jnp.float32))
    o_fox = forgetting_attention(heads(q_f, N_HEADS_FOX), heads(k_f, N_HEADS_FOX),
                                 heads(v_f, N_HEADS_FOX), log_f)
    merged = jax.nn.sigmoid(gm_nsa) * (o_nsa @ w_up_nsa) + jax.nn.sigmoid(gm_fox) * (o_fox @ w_up_fox)
    return merged @ w_o


def setup_inputs(seed: int = 0) -> dict:
    key = jax.random.key(seed)
    ks = jax.random.split(key, 32)
    L, D = DEPTH, D_MODEL

    def w(k, shape, fan_in, mult=1.0):
        return jax.random.normal(k, shape, jnp.float32) * (mult * fan_in ** -0.5)

    def gain(k, shape):
        return 1.0 + 0.05 * jax.random.normal(k, shape, jnp.float32)

    return {
        'x': jax.random.normal(ks[0], (BATCH, SEQ, D), jnp.float32),
        'c': jax.random.normal(ks[1], (BATCH, D), jnp.float32),
        'positions': (jnp.arange(SEQ, dtype=jnp.int32)[None, :]
                      + jax.random.randint(ks[2], (BATCH, 1), 0, 1024, dtype=jnp.int32)),
        'w_ada': w(ks[3], (L, D, N_MOD * D), D, 0.5),
        'b_ada': 0.02 * jax.random.normal(ks[4], (L, N_MOD * D), jnp.float32),
        'g_ffn1': gain(ks[5], (L, D)),
        'w_gate1': w(ks[6], (L, D, D_FF), D),
        'w_up1': w(ks[7], (L, D, D_FF), D),
        'w_down1': w(ks[8], (L, D_FF, D), D_FF),
        'g_mix': gain(ks[9], (L, D)),
        'w_in': w(ks[10], (L, D, D_IN), D),
        'b_forget': 3.0 + 0.5 * jax.random.normal(ks[11], (L, N_HEADS_FOX), jnp.float32),
        'pe_ck': 0.5 * jax.random.normal(ks[12], (L, CMP_BLOCK, HEAD_DIM), jnp.float32),
        'w1_ck': w(ks[13], (L, CMP_BLOCK * HEAD_DIM, CMP_HIDDEN), CMP_BLOCK * HEAD_DIM),
        'w2_ck': w(ks[14], (L, CMP_HIDDEN, HEAD_DIM), CMP_HIDDEN),
        'pe_cv': 0.5 * jax.random.normal(ks[15], (L, CMP_BLOCK, HEAD_DIM), jnp.float32),
        'w1_cv': w(ks[16], (L, CMP_BLOCK * HEAD_DIM, CMP_HIDDEN), CMP_BLOCK * HEAD_DIM),
        'w2_cv': w(ks[17], (L, CMP_HIDDEN, HEAD_DIM), CMP_HIDDEN),
        'w_up_nsa': w(ks[18], (L, D_NSA, D), D_NSA),
        'w_up_fox': w(ks[19], (L, D_FOX, D), D_FOX),
        'w_o': w(ks[20], (L, D, D), D),
        'g_ffn2': gain(ks[21], (L, D)),
        'w_gate2': w(ks[22], (L, D, D_FF), D),
        'w_up2': w(ks[23], (L, D, D_FF), D),
        'w_down2': w(ks[24], (L, D_FF, D), D_FF),
        'g_final': gain(ks[25], (D,)),
    }


def reference(x, c, positions, w_ada, b_ada, g_ffn1, w_gate1, w_up1, w_down1, g_mix, w_in,
              b_forget, pe_ck, w1_ck, w2_ck, pe_cv, w1_cv, w2_cv, w_up_nsa, w_up_fox, w_o,
              g_ffn2, w_gate2, w_up2, w_down2, g_final):
    c_act = jax.nn.silu(c)
    for l in range(DEPTH):
        mod = c_act @ w_ada[l] + b_ada[l]
        sh1, sc1, ga1, sh2, sc2, ga2, sh3, sc3, ga3 = jnp.split(mod, N_MOD, axis=-1)
        h = modulate(rms_norm(x, g_ffn1[l]), sh1, sc1)
        x = x + 0.5 * ga1[:, None, :] * swiglu(h, w_gate1[l], w_up1[l], w_down1[l])
        h = modulate(rms_norm(x, g_mix[l]), sh2, sc2)
        y = hybrid_mixer(h, positions, w_in[l], b_forget[l], pe_ck[l], w1_ck[l], w2_ck[l],
                         pe_cv[l], w1_cv[l], w2_cv[l], w_up_nsa[l], w_up_fox[l], w_o[l])
        x = x + ga2[:, None, :] * y
        h = modulate(rms_norm(x, g_ffn2[l]), sh3, sc3)
        x = x + 0.5 * ga3[:, None, :] * swiglu(h, w_gate2[l], w_up2[l], w_down2[l])
    return rms_norm(x, g_final)
```

```python
import functools

import numpy as np
import jax
import jax.numpy as jnp
from jax import lax
from jax.experimental import pallas as pl
from jax.experimental.pallas import tpu as pltpu

HEAD_DIM = 64
N_HEADS_NSA = 8
N_KV_NSA = 2
GROUP = N_HEADS_NSA // N_KV_NSA
N_HEADS_FOX = 8
CMP_BLOCK = 32
CMP_STRIDE = 16
CMP_HIDDEN = 128
SLC_BLOCK = 64
TOP_N = 16
WINDOW = 512
ROPE_THETA = 500000.0
ROPE_DIM = HEAD_DIM // 4
N_MOD = 9
RMS_EPS = 1e-6
NEG_INF = -1e30
FORCE_SCORE = 1e4

LANES = 128
F32 = jnp.float32
BF16 = jnp.bfloat16
VMEM_LIMIT = 56 * 1024 * 1024

_OFF_QN = 0
_OFF_KV = 512
_OFF_QF = 1280
_OFF_KF = 1792
_OFF_VF = 2304
_OFF_SMALL = 2816
_W_IN_COLS = 2944
_GATE_COLS = 3 * N_HEADS_NSA
_TAIL = HEAD_DIM


def _params(sem):
    return pltpu.CompilerParams(dimension_semantics=sem, vmem_limit_bytes=VMEM_LIMIT)


def _nt(a, b):
    return lax.dot_general(a, b, (((1,), (1,)), ((), ())), preferred_element_type=F32)


def _mm(a, b):
    return jnp.dot(a, b, preferred_element_type=F32)


def _split3(x):
    hi = x.astype(BF16)
    r = x - hi.astype(F32)
    mid = r.astype(BF16)
    lo = (r - mid.astype(F32)).astype(BF16)
    return hi, mid, lo


def _rms_mod(x, g, shift, scale):
    ms = jnp.mean(x * x, axis=-1, keepdims=True)
    y = x * lax.rsqrt(ms + RMS_EPS) * g
    return y * (1.0 + scale) + shift


def _ada_kernel(c_ref, w_ref, b_ref, o_ref):
    c = c_ref[...]
    ca = c * jax.nn.sigmoid(c)
    h0, h1, h2 = _split3(ca)
    w0, w1, w2 = _split3(w_ref[...])
    acc = _mm(h0, w0) + _mm(h0, w1) + _mm(h1, w0)
    acc = acc + (_mm(h1, w1) + _mm(h0, w2) + _mm(h2, w0))
    o_ref[...] = acc + b_ref[...]


def _ada(c, w_ada, b_ada):
    B, D = c.shape
    n = w_ada.shape[1]
    tn = 1024
    return pl.pallas_call(
        _ada_kernel,
        grid=(n // tn,),
        in_specs=[
            pl.BlockSpec((B, D), lambda j: (0, 0)),
            pl.BlockSpec((D, tn), lambda j: (0, j)),
            pl.BlockSpec((1, tn), lambda j: (0, j)),
        ],
        out_specs=pl.BlockSpec((B, tn), lambda j: (0, j)),
        out_shape=jax.ShapeDtypeStruct((B, n), F32),
        compiler_params=_params(("arbitrary",)),
        name="ada",
    )(c, w_ada, b_ada.reshape(1, n))


def _ffn_kernel(x_ref, mod_ref, g_ref, wg_ref, wu_ref, wd_ref, gf_ref, o_ref, h_ref, acc_ref, *, k_mod, final):
    x = x_ref[0]
    sh = mod_ref[0, k_mod:k_mod + 1, :]
    sc = mod_ref[0, k_mod + 1:k_mod + 2, :]
    ga = mod_ref[0, k_mod + 2:k_mod + 3, :]
    h_ref[...] = _rms_mod(x, g_ref[...], sh, sc).astype(BF16)
    acc_ref[...] = jnp.zeros_like(acc_ref)

    def body(j, carry):
        h = h_ref[...]
        g = _mm(h, wg_ref[j])
        u = _mm(h, wu_ref[j])
        a = (g * jax.nn.sigmoid(g) * u).astype(BF16)
        acc_ref[...] += _mm(a, wd_ref[j])
        return carry

    lax.fori_loop(0, wg_ref.shape[0], body, 0)
    out = x + 0.5 * ga * acc_ref[...]
    if final:
        ms = jnp.mean(out * out, axis=-1, keepdims=True)
        out = out * lax.rsqrt(ms + RMS_EPS) * gf_ref[...]
    o_ref[0] = out


def _ffn(x, mod3, g, wg, wu, wd, g_final, *, k_mod, final, tm):
    B, T, D = x.shape
    nch, _, ck = wg.shape
    const3 = lambda b, t: (0, 0, 0)
    return pl.pallas_call(
        functools.partial(_ffn_kernel, k_mod=k_mod, final=final),
        grid=(B, T // tm),
        in_specs=[
            pl.BlockSpec((1, tm, D), lambda b, t: (b, t, 0)),
            pl.BlockSpec((1, N_MOD, D), lambda b, t: (b, 0, 0)),
            pl.BlockSpec((1, D), lambda b, t: (0, 0)),
            pl.BlockSpec((nch, D, ck), const3, pipeline_mode=pl.Buffered(1)),
            pl.BlockSpec((nch, D, ck), const3, pipeline_mode=pl.Buffered(1)),
            pl.BlockSpec((nch, ck, D), const3, pipeline_mode=pl.Buffered(1)),
            pl.BlockSpec((1, D), lambda b, t: (0, 0)),
        ],
        out_specs=pl.BlockSpec((1, tm, D), lambda b, t: (b, t, 0)),
        out_shape=jax.ShapeDtypeStruct((B, T, D), F32),
        scratch_shapes=[pltpu.VMEM((tm, D), BF16), pltpu.VMEM((tm, D), F32)],
        compiler_params=_params(("arbitrary", "arbitrary")),
        name="ffn_final" if final else "ffn",
    )(x, mod3, g.reshape(1, D), wg, wu, wd, g_final.reshape(1, D))


def _inproj_kernel(x_ref, mod_ref, g_ref, pos_ref, invf_ref, sgn_ref, bf_ref, w_ref,
                   qn_ref, kcr_ref, vcr_ref, ks_ref, vsT_ref, kw_ref, vwT_ref, qf_ref, kf_ref, vfT_ref, gT_ref,
                   h_ref, carry_ref, *, tm):
    t_idx = pl.program_id(1)
    sh = mod_ref[0, 3:4, :]
    sc = mod_ref[0, 4:5, :]
    h_ref[...] = _rms_mod(x_ref[0], g_ref[...], sh, sc).astype(BF16)
    h = h_ref[...]

    lane = lax.broadcasted_iota(jnp.int32, (tm, LANES), 1)
    low = lane < HEAD_DIM
    first8 = (lane & (HEAD_DIM - 1)) < (ROPE_DIM // 2)
    ang = pos_ref[0] * invf_ref[...]
    cos_t = jnp.cos(ang)
    sin_t = jnp.sin(ang) * sgn_ref[...]

    def rope(xs):
        partner = jnp.where(first8, pltpu.roll(xs, LANES - ROPE_DIM // 2, 1), pltpu.roll(xs, ROPE_DIM // 2, 1))
        return xs * cos_t + partner * sin_t

    def split_heads(xs):
        return jnp.where(low, xs, 0.0), jnp.where(low, pltpu.roll(xs, HEAD_DIM, 1), 0.0)

    sm = _mm(h, w_ref[:, _OFF_SMALL:_OFF_SMALL + LANES])
    gT_ref[0] = jax.nn.sigmoid(sm).T[:_GATE_COLS, :]
    xl = sm + bf_ref[...]
    logf = jnp.minimum(xl, 0.0) - jnp.log1p(jnp.exp(-jnp.abs(xl)))
    tri = (lax.broadcasted_iota(jnp.int32, (tm, tm), 0) >= lax.broadcasted_iota(jnp.int32, (tm, tm), 1)).astype(BF16)
    l0, l1, l2 = _split3(logf)

    @pl.when(t_idx == 0)
    def _():
        carry_ref[...] = jnp.zeros_like(carry_ref)

    cf = _mm(tri, l0) + _mm(tri, l1) + _mm(tri, l2) + carry_ref[...]
    carry_ref[...] = cf[tm - 1:tm, :]
    c_hi = cf.astype(BF16).astype(F32)
    c_r = cf - c_hi
    c_mid = c_r.astype(BF16).astype(F32)
    c_lo = c_r - c_mid

    def tails(hd):
        col = _GATE_COLS + hd
        hi = jnp.broadcast_to(c_hi[:, col:col + 1], (tm, LANES))
        mid = jnp.broadcast_to(c_mid[:, col:col + 1], (tm, LANES))
        lo = jnp.broadcast_to(c_lo[:, col:col + 1], (tm, LANES))
        ones3 = (lane >= _TAIL) & (lane < _TAIL + 3)
        ones3b = (lane >= _TAIL + 3) & (lane < _TAIL + 6)
        qt = jnp.where(ones3, 1.0, jnp.where(lane == _TAIL + 3, hi, jnp.where(lane == _TAIL + 4, mid, jnp.where(lane == _TAIL + 5, lo, 0.0))))
        kt = jnp.where(ones3b, 1.0, jnp.where(lane == _TAIL, -hi, jnp.where(lane == _TAIL + 1, -mid, jnp.where(lane == _TAIL + 2, -lo, 0.0))))
        return qt, kt

    qn = _mm(h, w_ref[:, _OFF_QN:_OFF_QN + 512])
    scale = HEAD_DIM ** -0.5
    for j in range(4):
        a, b = split_heads(rope(qn[:, j * LANES:(j + 1) * LANES]) * scale)
        qn_ref[0, 2 * j] = a.astype(BF16)
        qn_ref[0, 2 * j + 1] = b.astype(BF16)

    kv = _mm(h, w_ref[:, _OFF_KV:_OFF_KV + 768])
    kcr_ref[0] = rope(kv[:, 0:LANES])
    vcr_ref[0] = kv[:, LANES:2 * LANES]
    for k_out, v_out, off in ((ks_ref, vsT_ref, 2 * LANES), (kw_ref, vwT_ref, 4 * LANES)):
        a, b = split_heads(rope(kv[:, off:off + LANES]))
        k_out[0, 0] = a.astype(BF16)
        k_out[0, 1] = b.astype(BF16)
        vT = kv[:, off + LANES:off + 2 * LANES].T.astype(BF16)
        for gi in range(N_KV_NSA):
            for c in range(tm // LANES):
                v_out[0, gi, c] = vT[gi * HEAD_DIM:(gi + 1) * HEAD_DIM, c * LANES:(c + 1) * LANES]

    qf = _mm(h, w_ref[:, _OFF_QF:_OFF_QF + 512])
    kf = _mm(h, w_ref[:, _OFF_KF:_OFF_KF + 512])
    for j in range(4):
        qa, qb = split_heads(qf[:, j * LANES:(j + 1) * LANES] * scale)
        ka, kb = split_heads(kf[:, j * LANES:(j + 1) * LANES])
        qta, kta = tails(2 * j)
        qtb, ktb = tails(2 * j + 1)
        qf_ref[0, 2 * j] = (qa + qta).astype(BF16)
        qf_ref[0, 2 * j + 1] = (qb + qtb).astype(BF16)
        kf_ref[0, 2 * j] = (ka + kta).astype(BF16)
        kf_ref[0, 2 * j + 1] = (kb + ktb).astype(BF16)
    vf = _mm(h, w_ref[:, _OFF_VF:_OFF_VF + 512])
    fox_tk = vfT_ref.shape[-1]
    for j in range(4):
        vT = vf[:, j * LANES:(j + 1) * LANES].T.astype(BF16)
        for hh in range(2):
            for c in range(tm // fox_tk):
                vfT_ref[0, 2 * j + hh, c] = vT[hh * HEAD_DIM:(hh + 1) * HEAD_DIM, c * fox_tk:(c + 1) * fox_tk]


def _inproj(x, mod3, g, pos_f, invf, sgn, bf_row, w, *, tm, nsa_tk, fox_tk):
    B, T, D = x.shape
    H, G = N_HEADS_NSA, N_KV_NSA
    bt = lambda b, t: (b, t, 0)
    hb = lambda b, t: (b, 0, t, 0)
    vb = lambda b, t: (b, 0, t, 0, 0)
    out_shape = [
        jax.ShapeDtypeStruct((B, H, T, LANES), BF16),
        jax.ShapeDtypeStruct((B, T, LANES), F32),
        jax.ShapeDtypeStruct((B, T, LANES), F32),
        jax.ShapeDtypeStruct((B, G, T, LANES), BF16),
        jax.ShapeDtypeStruct((B, G, T // nsa_tk, HEAD_DIM, nsa_tk), BF16),
        jax.ShapeDtypeStruct((B, G, T, LANES), BF16),
        jax.ShapeDtypeStruct((B, G, T // nsa_tk, HEAD_DIM, nsa_tk), BF16),
        jax.ShapeDtypeStruct((B, N_HEADS_FOX, T, LANES), BF16),
        jax.ShapeDtypeStruct((B, N_HEADS_FOX, T, LANES), BF16),
        jax.ShapeDtypeStruct((B, N_HEADS_FOX, T // fox_tk, HEAD_DIM, fox_tk), BF16),
        jax.ShapeDtypeStruct((B, _GATE_COLS, T), F32),
    ]
    out_specs = [
        pl.BlockSpec((1, H, tm, LANES), hb),
        pl.BlockSpec((1, tm, LANES), bt),
        pl.BlockSpec((1, tm, LANES), bt),
        pl.BlockSpec((1, G, tm, LANES), hb),
        pl.BlockSpec((1, G, tm // nsa_tk, HEAD_DIM, nsa_tk), vb),
        pl.BlockSpec((1, G, tm, LANES), hb),
        pl.BlockSpec((1, G, tm // nsa_tk, HEAD_DIM, nsa_tk), vb),
        pl.BlockSpec((1, N_HEADS_FOX, tm, LANES), hb),
        pl.BlockSpec((1, N_HEADS_FOX, tm, LANES), hb),
        pl.BlockSpec((1, N_HEADS_FOX, tm // fox_tk, HEAD_DIM, fox_tk), vb),
        pl.BlockSpec((1, _GATE_COLS, tm), lambda b, t: (b, 0, t)),
    ]
    row = lambda b, t: (0, 0)
    return pl.pallas_call(
        functools.partial(_inproj_kernel, tm=tm),
        grid=(B, T // tm),
        in_specs=[
            pl.BlockSpec((1, tm, D), bt),
            pl.BlockSpec((1, N_MOD, D), lambda b, t: (b, 0, 0)),
            pl.BlockSpec((1, D), row),
            pl.BlockSpec((1, tm, 1), bt),
            pl.BlockSpec((1, LANES), row),
            pl.BlockSpec((1, LANES), row),
            pl.BlockSpec((1, LANES), row),
            pl.BlockSpec((D, _W_IN_COLS), row, pipeline_mode=pl.Buffered(1)),
        ],
        out_specs=out_specs,
        out_shape=out_shape,
        scratch_shapes=[pltpu.VMEM((tm, D), BF16), pltpu.VMEM((1, LANES), F32)],
        compiler_params=_params(("arbitrary", "arbitrary")),
        name="inproj",
    )(x, mod3, g.reshape(1, D), pos_f, invf, sgn, bf_row, w)


def _gelu_tanh(x):
    c = np.float32(np.sqrt(2.0 / np.pi))
    return x * (0.5 * (1.0 + jnp.tanh(c * (x + 0.044715 * (x * x * x)))))


def _compress_kernel(zk_ref, zv_ref, pe_ref, wkt_ref, wkb_ref, wvt_ref, wvb_ref, w2k_ref, w2v_ref, kc_ref, vcT_ref):
    nsub = zk_ref.shape[1]

    def mlp(z, pe_top, pe_bot, wt_ref, wb_ref):
        a = _mm((z + pe_top).astype(BF16), wt_ref[...])
        b = _mm((z + pe_bot).astype(BF16), wb_ref[...])
        return _gelu_tanh(a + pltpu.roll(b, nsub - 1, 0))

    hk = mlp(zk_ref[0], pe_ref[0:1, :], pe_ref[1:2, :], wkt_ref, wkb_ref)
    hv = mlp(zv_ref[0], pe_ref[2:3, :], pe_ref[3:4, :], wvt_ref, wvb_ref)
    for gi in range(N_KV_NSA):
        kc_ref[0, gi] = _mm(hk[:, gi * CMP_HIDDEN:(gi + 1) * CMP_HIDDEN].astype(BF16), w2k_ref[...]).astype(BF16)
        vc = _mm(hv[:, gi * CMP_HIDDEN:(gi + 1) * CMP_HIDDEN].astype(BF16), w2v_ref[...])
        vcT_ref[0, gi] = vc.T[:HEAD_DIM, :].astype(BF16)


def _compress(zk, zv, pe4, wkt, wkb, wvt, wvb, w2k, w2v):
    B, nsub, zc = zk.shape
    G = N_KV_NSA
    c2 = lambda b: (0, 0)
    zspec = pl.BlockSpec((1, nsub, zc), lambda b: (b, 0, 0))
    wspec = pl.BlockSpec((zc, G * CMP_HIDDEN), c2)
    w2spec = pl.BlockSpec((CMP_HIDDEN, LANES), c2)
    return pl.pallas_call(
        _compress_kernel,
        grid=(B,),
        in_specs=[zspec, zspec, pl.BlockSpec((4, zc), c2), wspec, wspec, wspec, wspec, w2spec, w2spec],
        out_specs=[
            pl.BlockSpec((1, G, nsub, LANES), lambda b: (b, 0, 0, 0)),
            pl.BlockSpec((1, G, HEAD_DIM, nsub), lambda b: (b, 0, 0, 0)),
        ],
        out_shape=[
            jax.ShapeDtypeStruct((B, G, nsub, LANES), BF16),
            jax.ShapeDtypeStruct((B, G, HEAD_DIM, nsub), BF16),
        ],
        compiler_params=_params(("arbitrary",)),
        name="compress",
    )(zk, zv, pe4, wkt, wkb, wvt, wvb, w2k, w2v)


def _flash_init(m_ref, l_ref, acc_ref):
    m_ref[...] = jnp.full_like(m_ref, NEG_INF)
    l_ref[...] = jnp.zeros_like(l_ref)
    acc_ref[...] = jnp.zeros_like(acc_ref)


def _flash_update(sT, vT, m_ref, l_ref, acc_ref):
    m_old = m_ref[...]
    m_new = jnp.maximum(m_old, jnp.max(sT, axis=0, keepdims=True))
    alpha = jnp.exp(m_old - m_new)
    p = jnp.exp(sT - m_new)
    l_ref[...] = alpha * l_ref[...] + jnp.sum(p, axis=0, keepdims=True)
    acc_ref[...] = alpha * acc_ref[...] + _mm(vT, p.astype(BF16))
    m_ref[...] = m_new


def _fox_kernel(q_ref, k_ref, vT_ref, o_ref, m_ref, l_ref, acc_ref, oT_ref, *, tq, tk):
    qi = pl.program_id(2)
    nkc = tq // tk
    key_l = lax.broadcasted_iota(jnp.int32, (tk, tq), 0)
    q_l = lax.broadcasted_iota(jnp.int32, (tk, tq), 1)
    for hh in range(2):
        q = q_ref[0, hh]
        _flash_init(m_ref, l_ref, acc_ref)

        def body(kc, carry):
            k0 = pl.multiple_of(kc * tk, tk)
            sT = _nt(k_ref[0, hh, pl.ds(k0, tk), :], q)
            _flash_update(sT, vT_ref[0, hh, kc], m_ref, l_ref, acc_ref)
            return carry

        lax.fori_loop(0, qi * nkc, body, 0)
        for c in range(nkc):
            kc = qi * nkc + c
            k0 = pl.multiple_of(kc * tk, tk)
            sT = _nt(k_ref[0, hh, pl.ds(k0, tk), :], q)
            sT = jnp.where(key_l + c * tk <= q_l, sT, NEG_INF)
            _flash_update(sT, vT_ref[0, hh, kc], m_ref, l_ref, acc_ref)
        oT_ref[hh * HEAD_DIM:(hh + 1) * HEAD_DIM, :] = acc_ref[...] / l_ref[...]
    o_ref[0] = oT_ref[...].T.astype(BF16)


def _fox(qf, kf, vfT, *, tq):
    B, H, T, _ = qf.shape
    tk = vfT.shape[-1]
    return pl.pallas_call(
        functools.partial(_fox_kernel, tq=tq, tk=tk),
        grid=(B, H // 2, T // tq),
        in_specs=[
            pl.BlockSpec((1, 2, tq, LANES), lambda b, p, i: (b, p, i, 0)),
            pl.BlockSpec((1, 2, T, LANES), lambda b, p, i: (b, p, 0, 0)),
            pl.BlockSpec((1, 2, T // tk, HEAD_DIM, tk), lambda b, p, i: (b, p, 0, 0, 0)),
        ],
        out_specs=pl.BlockSpec((1, tq, LANES), lambda b, p, i: (b, i, p)),
        out_shape=jax.ShapeDtypeStruct((B, T, H * HEAD_DIM), BF16),
        scratch_shapes=[
            pltpu.VMEM((1, tq), F32),
            pltpu.VMEM((1, tq), F32),
            pltpu.VMEM((HEAD_DIM, tq), F32),
            pltpu.VMEM((2 * HEAD_DIM, tq), F32),
        ],
        compiler_params=_params(("arbitrary", "arbitrary", "arbitrary")),
        name="fox",
    )(qf, kf, vfT)


def _nsa_kernel(q_ref, kc_ref, vcT_ref, ovT_ref, ks_ref, vsT_ref, kw_ref, vwT_ref, g_ref, o_ref,
                m_ref, l_ref, acc_ref, sel_ref, imp_ref, *, tq, tk):
    qt = pl.program_id(2)
    nq = GROUP * tq
    ncp = kc_ref.shape[2]
    ns = ovT_ref.shape[0]
    q = q_ref[0].reshape(nq, LANES)
    t_row = qt * tq + (lax.broadcasted_iota(jnp.int32, (1, nq), 1) & (tq - 1))

    sT = _nt(kc_ref[0, 0], q)
    c_io = lax.broadcasted_iota(jnp.int32, (ncp, nq), 0)
    valid = (c_io * CMP_STRIDE + (CMP_BLOCK - 1)) <= t_row
    smk = jnp.where(valid, sT, NEG_INF)
    mc = jnp.max(smk, axis=0, keepdims=True)
    p = jnp.where(valid, jnp.exp(smk - mc), 0.0)
    lc = jnp.sum(p, axis=0, keepdims=True)
    inv = jnp.where(lc > 0.0, 1.0 / lc, 0.0)
    o_cmp = _mm(vcT_ref[0, 0], p.astype(BF16)) * inv
    pn = p * inv
    psum = pn[:, 0:tq]
    for r in range(1, GROUP):
        psum = psum + pn[:, r * tq:(r + 1) * tq]
    p0, p1, p2 = _split3(psum)
    ovT = ovT_ref[...]
    impT = _mm(ovT, p0) + _mm(ovT, p1) + _mm(ovT, p2)

    n_io = lax.broadcasted_iota(jnp.int32, (ns, tq), 0)
    cur = lax.shift_right_logical(qt * tq + lax.broadcasted_iota(jnp.int32, (ns, tq), 1), SLC_BLOCK.bit_length() - 1)
    forced = (n_io == 0) | (n_io == cur) | (n_io == cur - 1)
    imp2 = jnp.where(forced, FORCE_SCORE, jnp.where(n_io <= cur, impT, -1.0))
    imp_ref[...] = imp2

    def rank_body(mm, cnt):
        row = imp_ref[pl.ds(mm, 1), :]
        beats = (row > imp2) | ((row >= imp2) & (n_io > mm))
        return cnt + beats.astype(F32)

    cnt = lax.fori_loop(0, ns, rank_body, jnp.zeros((ns, tq), F32))
    sel_ref[...] = (cnt < float(TOP_N)).astype(F32)

    key_l = lax.broadcasted_iota(jnp.int32, (tk, nq), 0)
    upper = lax.broadcasted_iota(jnp.int32, (tk, tq), 0) < SLC_BLOCK

    _flash_init(m_ref, l_ref, acc_ref)

    def slc_body(kj, carry):
        k0 = pl.multiple_of(kj * tk, tk)
        sT = _nt(ks_ref[0, 0, pl.ds(k0, tk), :], q)
        nb = tk // SLC_BLOCK
        s0 = sel_ref[pl.ds(kj * nb, 1), :]
        s1 = sel_ref[pl.ds(kj * nb + 1, 1), :]
        selt = jnp.where(upper, s0, s1)
        sel4 = jnp.concatenate([selt] * GROUP, axis=1)
        mask = (sel4 > 0.5) & ((key_l + k0) <= t_row)
        _flash_update(jnp.where(mask, sT, NEG_INF), vsT_ref[0, 0, kj], m_ref, l_ref, acc_ref)
        return carry

    lax.fori_loop(0, qt + 1, slc_body, 0)
    o_slc = acc_ref[...] / l_ref[...]

    _flash_init(m_ref, l_ref, acc_ref)

    def win_body(kj, carry):
        k0 = pl.multiple_of(kj * tk, tk)
        sT = _nt(kw_ref[0, 0, pl.ds(k0, tk), :], q)
        kpos = key_l + k0
        mask = (kpos <= t_row) & ((t_row - kpos) < WINDOW)
        _flash_update(jnp.where(mask, sT, NEG_INF), vwT_ref[0, 0, kj], m_ref, l_ref, acc_ref)
        return carry

    lax.fori_loop(jnp.maximum(qt - WINDOW // tk, 0), qt + 1, win_body, 0)
    o_win = acc_ref[...] / l_ref[...]

    def gate_row(j):
        return jnp.concatenate([g_ref[0, 0, 3 * r + j:3 * r + j + 1, :] for r in range(GROUP)], axis=1)

    oT = gate_row(0) * o_cmp + gate_row(1) * o_slc + gate_row(2) * o_win
    for pr in range(GROUP // 2):
        st = jnp.concatenate([oT[:, (2 * pr) * tq:(2 * pr + 1) * tq], oT[:, (2 * pr + 1) * tq:(2 * pr + 2) * tq]], axis=0)
        o_ref[0, :, pr * LANES:(pr + 1) * LANES] = st.T.astype(BF16)


def _nsa(qn, kc, vcT, ovT, ks, vsT, kw, vwT, gT4, *, tq):
    B, H, T, _ = qn.shape
    G = N_KV_NSA
    tk = vsT.shape[-1]
    ncp = kc.shape[2]
    ns = ovT.shape[0]
    kvspec = pl.BlockSpec((1, 1, T, LANES), lambda b, g, i: (b, g, 0, 0))
    vtspec = pl.BlockSpec((1, 1, T // tk, HEAD_DIM, tk), lambda b, g, i: (b, g, 0, 0, 0))
    return pl.pallas_call(
        functools.partial(_nsa_kernel, tq=tq, tk=tk),
        grid=(B, G, T // tq),
        in_specs=[
            pl.BlockSpec((1, GROUP, tq, LANES), lambda b, g, i: (b, g, i, 0)),
            pl.BlockSpec((1, 1, ncp, LANES), lambda b, g, i: (b, g, 0, 0)),
            pl.BlockSpec((1, 1, HEAD_DIM, ncp), lambda b, g, i: (b, g, 0, 0)),
            pl.BlockSpec((ns, ncp), lambda b, g, i: (0, 0)),
            kvspec, vtspec, kvspec, vtspec,
            pl.BlockSpec((1, 1, 3 * GROUP, tq), lambda b, g, i: (b, g, 0, i)),
        ],
        out_specs=pl.BlockSpec((1, tq, GROUP * HEAD_DIM), lambda b, g, i: (b, i, g)),
        out_shape=jax.ShapeDtypeStruct((B, T, H * HEAD_DIM), BF16),
        scratch_shapes=[
            pltpu.VMEM((1, GROUP * tq), F32),
            pltpu.VMEM((1, GROUP * tq), F32),
            pltpu.VMEM((HEAD_DIM, GROUP * tq), F32),
            pltpu.VMEM((ns, tq), F32),
            pltpu.VMEM((ns, tq), F32),
        ],
        compiler_params=_params(("arbitrary", "arbitrary", "arbitrary")),
        name="nsa",
    )(qn, kc, vcT, ovT, ks, vsT, kw, vwT, gT4)


def _mixout_kernel(x_ref, mod_ref, g_ref, on_ref, of_ref, wgm_ref, wun_ref, wuf_ref, wo_ref, o_ref):
    x = x_ref[0]
    D = x.shape[-1]
    sh = mod_ref[0, 3:4, :]
    sc = mod_ref[0, 4:5, :]
    ga = mod_ref[0, 5:6, :]
    h = _rms_mod(x, g_ref[...], sh, sc).astype(BF16)
    gm = _mm(h, wgm_ref[...])
    un = _mm(on_ref[0], wun_ref[...])
    uf = _mm(of_ref[0], wuf_ref[...])
    merged = jax.nn.sigmoid(gm[:, :D]) * un + jax.nn.sigmoid(gm[:, D:]) * uf
    y = _mm(merged.astype(BF16), wo_ref[...])
    o_ref[0] = x + ga * y


def _mixout(x, mod3, g, o_nsa, o_fox, wgm, wun, wuf, wo, *, tm):
    B, T, D = x.shape
    bt = lambda b, t: (b, t, 0)
    c2 = lambda b, t: (0, 0)
    dn = o_nsa.shape[-1]
    return pl.pallas_call(
        _mixout_kernel,
        grid=(B, T // tm),
        in_specs=[
            pl.BlockSpec((1, tm, D), bt),
            pl.BlockSpec((1, N_MOD, D), lambda b, t: (b, 0, 0)),
            pl.BlockSpec((1, D), c2),
            pl.BlockSpec((1, tm, dn), bt),
            pl.BlockSpec((1, tm, dn), bt),
            pl.BlockSpec((D, 2 * D), c2, pipeline_mode=pl.Buffered(1)),
            pl.BlockSpec((dn, D), c2, pipeline_mode=pl.Buffered(1)),
            pl.BlockSpec((dn, D), c2, pipeline_mode=pl.Buffered(1)),
            pl.BlockSpec((D, D), c2, pipeline_mode=pl.Buffered(1)),
        ],
        out_specs=pl.BlockSpec((1, tm, D), bt),
        out_shape=jax.ShapeDtypeStruct((B, T, D), F32),
        compiler_params=_params(("arbitrary", "arbitrary")),
        name="mixout",
    )(x, mod3, g.reshape(1, D), o_nsa, o_fox, wgm, wun, wuf, wo)


def _chunk_ffn_weights(w_gate, w_up, w_down, ck):
    D, F = w_gate.shape
    n = F // ck
    wg = w_gate.astype(BF16).reshape(D, n, ck).transpose(1, 0, 2)
    wu = w_up.astype(BF16).reshape(D, n, ck).transpose(1, 0, 2)
    wd = w_down.astype(BF16).reshape(n, ck, D)
    return wg, wu, wd


def _compress_weights(pe, w1, w2):
    half = CMP_BLOCK // 2
    eye = jnp.eye(N_KV_NSA, dtype=F32)

    def expand(w_half):
        w3 = w_half.reshape(half, HEAD_DIM, CMP_HIDDEN)
        return jnp.einsum('jdn,gh->jgdhn', w3, eye).reshape(half * N_KV_NSA * HEAD_DIM, N_KV_NSA * CMP_HIDDEN).astype(BF16)

    def pe_row(pe_half):
        return jnp.broadcast_to(pe_half[:, None, :], (half, N_KV_NSA, HEAD_DIM)).reshape(1, -1)

    w2p = jnp.pad(w2, ((0, 0), (0, LANES - HEAD_DIM))).astype(BF16)
    return (pe_row(pe[:half]), pe_row(pe[half:]), expand(w1[:half * HEAD_DIM]), expand(w1[half * HEAD_DIM:]), w2p)


def kernel(x, c, positions, w_ada, b_ada, g_ffn1, w_gate1, w_up1, w_down1, g_mix, w_in, b_forget, pe_ck, w1_ck, w2_ck, pe_cv, w1_cv, w2_cv, w_up_nsa, w_up_fox, w_o, g_ffn2, w_gate2, w_up2, w_down2, g_final):
    B, T, D = x.shape
    depth = w_ada.shape[0]
    tm = 512
    ffn_ck = 256
    nsa_tq, nsa_tk = 128, 128
    fox_tq, fox_tk = 512, 256
    n_slc = T // SLC_BLOCK
    n_sub = T // CMP_STRIDE

    half = ROPE_DIM // 2
    inv_freq = ROPE_THETA ** (-jnp.arange(half, dtype=F32) / half)
    d_in_head = jnp.arange(LANES) % HEAD_DIM
    invf = jnp.where(d_in_head < ROPE_DIM, inv_freq[d_in_head % half], 0.0).astype(F32).reshape(1, LANES)
    sgn = jnp.where(d_in_head < half, -1.0, 1.0).astype(F32).reshape(1, LANES)
    cmp_start = np.arange(n_sub) * CMP_STRIDE
    slc_start = np.arange(n_slc) * SLC_BLOCK
    ov = ((cmp_start[:, None] < slc_start[None, :] + SLC_BLOCK) & (slc_start[None, :] < cmp_start[:, None] + CMP_BLOCK))
    ov[n_sub - CMP_BLOCK // CMP_STRIDE + 1:, :] = False
    ovT = jnp.asarray(ov.T, dtype=BF16)

    pos_f = positions.astype(F32)[..., None]
    c_in = c
    for l in range(depth):
        mod3 = _ada(c_in, w_ada[l], b_ada[l]).reshape(B, N_MOD, D)
        wg1, wu1, wd1 = _chunk_ffn_weights(w_gate1[l], w_up1[l], w_down1[l], ffn_ck)
        wg2, wu2, wd2 = _chunk_ffn_weights(w_gate2[l], w_up2[l], w_down2[l], ffn_ck)
        wl = w_in[l]
        small = jnp.concatenate([wl[:, 1280:1304], wl[:, 2840:2848], jnp.zeros((D, LANES - 32), F32)], axis=1)
        w_proj = jnp.concatenate([wl[:, :1280], wl[:, 1304:2840], small], axis=1).astype(BF16)
        w_gm = wl[:, 2848:].astype(BF16)
        bf_row = jnp.zeros((1, LANES), F32).at[0, _GATE_COLS:_GATE_COLS + N_HEADS_FOX].set(b_forget[l])

        x = _ffn(x, mod3, g_ffn1[l], wg1, wu1, wd1, g_final, k_mod=0, final=False, tm=tm)

        (qn, kcr, vcr, ks, vsT, kw, vwT, qf, kf, vfT, gT) = _inproj(
            x, mod3, g_mix[l], pos_f, invf, sgn, bf_row, w_proj, tm=tm, nsa_tk=nsa_tk, fox_tk=fox_tk)

        pk_t, pk_b, wk_t, wk_b, w2k = _compress_weights(pe_ck[l], w1_ck[l], w2_ck[l])
        pv_t, pv_b, wv_t, wv_b, w2v = _compress_weights(pe_cv[l], w1_cv[l], w2_cv[l])
        pe4 = jnp.concatenate([pk_t, pk_b, pv_t, pv_b], axis=0)
        zc = CMP_STRIDE * LANES
        kc, vcT = _compress(kcr.reshape(B, n_sub, zc), vcr.reshape(B, n_sub, zc), pe4, wk_t, wk_b, wv_t, wv_b, w2k, w2v)

        o_fox = _fox(qf, kf, vfT, tq=fox_tq)
        gT4 = gT.reshape(B, N_KV_NSA, 3 * GROUP, T)
        o_nsa = _nsa(qn, kc, vcT, ovT, ks, vsT, kw, vwT, gT4, tq=nsa_tq)

        x = _mixout(x, mod3, g_mix[l], o_nsa, o_fox, w_gm, w_up_nsa[l].astype(BF16), w_up_fox[l].astype(BF16),
                    w_o[l].astype(BF16), tm=tm)
        last = l == depth - 1
        x = _ffn(x, mod3, g_ffn2[l], wg2, wu2, wd2, g_final, k_mod=6, final=last, tm=tm)
    return x
```

```python
import functools

import numpy as np
import jax
import jax.numpy as jnp
from jax import lax
from jax.experimental import pallas as pl
from jax.experimental.pallas import tpu as pltpu

HEAD_DIM = 64
N_HEADS_NSA = 8
N_KV_NSA = 2
GROUP = N_HEADS_NSA // N_KV_NSA
N_HEADS_FOX = 8
CMP_BLOCK = 32
CMP_STRIDE = 16
CMP_HIDDEN = 128
SLC_BLOCK = 64
TOP_N = 16
WINDOW = 512
ROPE_THETA = 500000.0
ROPE_DIM = HEAD_DIM // 4
N_MOD = 9
RMS_EPS = 1e-6
NEG_INF = -1e30
FORCE_SCORE = 1e4

LANES = 128
F32 = jnp.float32
BF16 = jnp.bfloat16
VMEM_LIMIT = 56 * 1024 * 1024

_OFF_QN = 0
_OFF_KV = 512
_OFF_QF = 1280
_OFF_KF = 1792
_OFF_VF = 2304
_OFF_SMALL = 2816
_W_IN_COLS = 2944
_GATE_COLS = 3 * N_HEADS_NSA
_TAIL = HEAD_DIM


def _params(sem):
    return pltpu.CompilerParams(dimension_semantics=sem, vmem_limit_bytes=VMEM_LIMIT)


def _nt(a, b):
    return lax.dot_general(a, b, (((1,), (1,)), ((), ())), preferred_element_type=F32)


def _mm(a, b):
    return jnp.dot(a, b, preferred_element_type=F32)


def _split3(x):
    hi = x.astype(BF16)
    r = x - hi.astype(F32)
    mid = r.astype(BF16)
    lo = (r - mid.astype(F32)).astype(BF16)
    return hi, mid, lo


def _rms_mod(x, g, shift, scale):
    ms = jnp.mean(x * x, axis=-1, keepdims=True)
    y = x * lax.rsqrt(ms + RMS_EPS) * g
    return y * (1.0 + scale) + shift


def _ada_kernel(c_ref, w_ref, b_ref, o_ref):
    c = c_ref[...]
    ca = c * jax.nn.sigmoid(c)
    h0, h1, h2 = _split3(ca)
    w0, w1, w2 = _split3(w_ref[...])
    acc = _mm(h0, w0) + _mm(h0, w1) + _mm(h1, w0)
    acc = acc + (_mm(h1, w1) + _mm(h0, w2) + _mm(h2, w0))
    o_ref[...] = acc + b_ref[...]


def _ada(c, w_ada, b_ada):
    B, D = c.shape
    n = w_ada.shape[1]
    tn = 1024
    return pl.pallas_call(
        _ada_kernel,
        grid=(n // tn,),
        in_specs=[
            pl.BlockSpec((B, D), lambda j: (0, 0)),
            pl.BlockSpec((D, tn), lambda j: (0, j)),
            pl.BlockSpec((1, tn), lambda j: (0, j)),
        ],
        out_specs=pl.BlockSpec((B, tn), lambda j: (0, j)),
        out_shape=jax.ShapeDtypeStruct((B, n), F32),
        compiler_params=_params(("arbitrary",)),
        name="ada",
    )(c, w_ada, b_ada.reshape(1, n))


def _ffn_kernel(x_ref, mod_ref, g_ref, wg_ref, wu_ref, wd_ref, gf_ref, o_ref, h_ref, acc_ref, *, k_mod, final):
    x = x_ref[0]
    sh = mod_ref[0, k_mod:k_mod + 1, :]
    sc = mod_ref[0, k_mod + 1:k_mod + 2, :]
    ga = mod_ref[0, k_mod + 2:k_mod + 3, :]
    h_ref[...] = _rms_mod(x, g_ref[...], sh, sc).astype(BF16)
    acc_ref[...] = jnp.zeros_like(acc_ref)

    def body(j, carry):
        h = h_ref[...]
        g = _mm(h, wg_ref[j])
        u = _mm(h, wu_ref[j])
        a = (g * jax.nn.sigmoid(g) * u).astype(BF16)
        acc_ref[...] += _mm(a, wd_ref[j])
        return carry

    lax.fori_loop(0, wg_ref.shape[0], body, 0)
    out = x + 0.5 * ga * acc_ref[...]
    if final:
        ms = jnp.mean(out * out, axis=-1, keepdims=True)
        out = out * lax.rsqrt(ms + RMS_EPS) * gf_ref[...]
    o_ref[0] = out


def _ffn(x, mod3, g, wg, wu, wd, g_final, *, k_mod, final, tm):
    B, T, D = x.shape
    nch, _, ck = wg.shape
    const3 = lambda b, t: (0, 0, 0)
    return pl.pallas_call(
        functools.partial(_ffn_kernel, k_mod=k_mod, final=final),
        grid=(B, T // tm),
        in_specs=[
            pl.BlockSpec((1, tm, D), lambda b, t: (b, t, 0)),
            pl.BlockSpec((1, N_MOD, D), lambda b, t: (b, 0, 0)),
            pl.BlockSpec((1, D), lambda b, t: (0, 0)),
            pl.BlockSpec((nch, D, ck), const3, pipeline_mode=pl.Buffered(1)),
            pl.BlockSpec((nch, D, ck), const3, pipeline_mode=pl.Buffered(1)),
            pl.BlockSpec((nch, ck, D), const3, pipeline_mode=pl.Buffered(1)),
            pl.BlockSpec((1, D), lambda b, t: (0, 0)),
        ],
        out_specs=pl.BlockSpec((1, tm, D), lambda b, t: (b, t, 0)),
        out_shape=jax.ShapeDtypeStruct((B, T, D), F32),
        scratch_shapes=[pltpu.VMEM((tm, D), BF16), pltpu.VMEM((tm, D), F32)],
        compiler_params=_params(("arbitrary", "arbitrary")),
        name="ffn_final" if final else "ffn",
    )(x, mod3, g.reshape(1, D), wg, wu, wd, g_final.reshape(1, D))


def _inproj_kernel(x_ref, mod_ref, g_ref, pos_ref, invf_ref, sgn_ref, bf_ref, w_ref,
                   qn_ref, kcr_ref, vcr_ref, ks_ref, vsT_ref, kw_ref, vwT_ref, qf_ref, kf_ref, vfT_ref, gT_ref,
                   h_ref, carry_ref, *, tm):
    t_idx = pl.program_id(1)
    sh = mod_ref[0, 3:4, :]
    sc = mod_ref[0, 4:5, :]
    h_ref[...] = _rms_mod(x_ref[0], g_ref[...], sh, sc).astype(BF16)
    h = h_ref[...]

    lane = lax.broadcasted_iota(jnp.int32, (tm, LANES), 1)
    low = lane < HEAD_DIM
    first8 = (lane & (HEAD_DIM - 1)) < (ROPE_DIM // 2)
    ang = pos_ref[0] * invf_ref[...]
    cos_t = jnp.cos(ang)
    sin_t = jnp.sin(ang) * sgn_ref[...]

    def rope(xs):
        partner = jnp.where(first8, pltpu.roll(xs, LANES - ROPE_DIM // 2, 1), pltpu.roll(xs, ROPE_DIM // 2, 1))
        return xs * cos_t + partner * sin_t

    def split_heads(xs):
        return jnp.where(low, xs, 0.0), jnp.where(low, pltpu.roll(xs, HEAD_DIM, 1), 0.0)

    sm = _mm(h, w_ref[:, _OFF_SMALL:_OFF_SMALL + LANES])
    gT_ref[0] = jax.nn.sigmoid(sm).T[:_GATE_COLS, :]
    xl = sm + bf_ref[...]
    logf = jnp.minimum(xl, 0.0) - jnp.log1p(jnp.exp(-jnp.abs(xl)))
    tri = (lax.broadcasted_iota(jnp.int32, (tm, tm), 0) >= lax.broadcasted_iota(jnp.int32, (tm, tm), 1)).astype(BF16)
    l0, l1, l2 = _split3(logf)

    @pl.when(t_idx == 0)
    def _():
        carry_ref[...] = jnp.zeros_like(carry_ref)

    cf = _mm(tri, l0) + _mm(tri, l1) + _mm(tri, l2) + carry_ref[...]
    carry_ref[...] = cf[tm - 1:tm, :]
    c_hi = cf.astype(BF16).astype(F32)
    c_r = cf - c_hi
    c_mid = c_r.astype(BF16).astype(F32)
    c_lo = c_r - c_mid

    def tails(hd):
        col = _GATE_COLS + hd
        hi = jnp.broadcast_to(c_hi[:, col:col + 1], (tm, LANES))
        mid = jnp.broadcast_to(c_mid[:, col:col + 1], (tm, LANES))
        lo = jnp.broadcast_to(c_lo[:, col:col + 1], (tm, LANES))
        ones3 = (lane >= _TAIL) & (lane < _TAIL + 3)
        ones3b = (lane >= _TAIL + 3) & (lane < _TAIL + 6)
        qt = jnp.where(ones3, 1.0, jnp.where(lane == _TAIL + 3, hi, jnp.where(lane == _TAIL + 4, mid, jnp.where(lane == _TAIL + 5, lo, 0.0))))
        kt = jnp.where(ones3b, 1.0, jnp.where(lane == _TAIL, -hi, jnp.where(lane == _TAIL + 1, -mid, jnp.where(lane == _TAIL + 2, -lo, 0.0))))
        return qt, kt

    qn = _mm(h, w_ref[:, _OFF_QN:_OFF_QN + 512])
    scale = HEAD_DIM ** -0.5
    for j in range(4):
        a, b = split_heads(rope(qn[:, j * LANES:(j + 1) * LANES]) * scale)
        qn_ref[0, 2 * j] = a.astype(BF16)
        qn_ref[0, 2 * j + 1] = b.astype(BF16)

    nsa_tk = vsT_ref.shape[-1]
    kv = _mm(h, w_ref[:, _OFF_KV:_OFF_KV + 768])
    kcr_ref[0] = rope(kv[:, 0:LANES])
    vcr_ref[0] = kv[:, LANES:2 * LANES]
    blk = lax.shift_right_logical(t_idx * tm + lax.broadcasted_iota(jnp.int32, (tm, LANES), 0), SLC_BLOCK.bit_length() - 1)
    onehot = (lane == blk + HEAD_DIM).astype(F32)
    for k_out, v_out, off, tail in ((ks_ref, vsT_ref, 2 * LANES, onehot), (kw_ref, vwT_ref, 4 * LANES, None)):
        a, b = split_heads(rope(kv[:, off:off + LANES]))
        if tail is not None:
            a, b = a + tail, b + tail
        k_out[0, 0] = a.astype(BF16)
        k_out[0, 1] = b.astype(BF16)
        vT = kv[:, off + LANES:off + 2 * LANES].T.astype(BF16)
        for gi in range(N_KV_NSA):
            for c in range(tm // nsa_tk):
                v_out[0, gi, c] = vT[gi * HEAD_DIM:(gi + 1) * HEAD_DIM, c * nsa_tk:(c + 1) * nsa_tk]

    qf = _mm(h, w_ref[:, _OFF_QF:_OFF_QF + 512])
    kf = _mm(h, w_ref[:, _OFF_KF:_OFF_KF + 512])
    for j in range(4):
        qa, qb = split_heads(qf[:, j * LANES:(j + 1) * LANES] * scale)
        ka, kb = split_heads(kf[:, j * LANES:(j + 1) * LANES])
        qta, kta = tails(2 * j)
        qtb, ktb = tails(2 * j + 1)
        qf_ref[0, 2 * j] = (qa + qta).astype(BF16)
        qf_ref[0, 2 * j + 1] = (qb + qtb).astype(BF16)
        kf_ref[0, 2 * j] = (ka + kta).astype(BF16)
        kf_ref[0, 2 * j + 1] = (kb + ktb).astype(BF16)
    vf = _mm(h, w_ref[:, _OFF_VF:_OFF_VF + 512])
    fox_tk = vfT_ref.shape[-1]
    for j in range(4):
        vT = vf[:, j * LANES:(j + 1) * LANES].T.astype(BF16)
        for hh in range(2):
            for c in range(tm // fox_tk):
                vfT_ref[0, 2 * j + hh, c] = vT[hh * HEAD_DIM:(hh + 1) * HEAD_DIM, c * fox_tk:(c + 1) * fox_tk]


def _inproj(x, mod3, g, pos_f, invf, sgn, bf_row, w, *, tm, nsa_tk, fox_tk):
    B, T, D = x.shape
    H, G = N_HEADS_NSA, N_KV_NSA
    bt = lambda b, t: (b, t, 0)
    hb = lambda b, t: (b, 0, t, 0)
    vb = lambda b, t: (b, 0, t, 0, 0)
    out_shape = [
        jax.ShapeDtypeStruct((B, H, T, LANES), BF16),
        jax.ShapeDtypeStruct((B, T, LANES), F32),
        jax.ShapeDtypeStruct((B, T, LANES), F32),
        jax.ShapeDtypeStruct((B, G, T, LANES), BF16),
        jax.ShapeDtypeStruct((B, G, T // nsa_tk, HEAD_DIM, nsa_tk), BF16),
        jax.ShapeDtypeStruct((B, G, T, LANES), BF16),
        jax.ShapeDtypeStruct((B, G, T // nsa_tk, HEAD_DIM, nsa_tk), BF16),
        jax.ShapeDtypeStruct((B, N_HEADS_FOX, T, LANES), BF16),
        jax.ShapeDtypeStruct((B, N_HEADS_FOX, T, LANES), BF16),
        jax.ShapeDtypeStruct((B, N_HEADS_FOX, T // fox_tk, HEAD_DIM, fox_tk), BF16),
        jax.ShapeDtypeStruct((B, _GATE_COLS, T), F32),
    ]
    out_specs = [
        pl.BlockSpec((1, H, tm, LANES), hb),
        pl.BlockSpec((1, tm, LANES), bt),
        pl.BlockSpec((1, tm, LANES), bt),
        pl.BlockSpec((1, G, tm, LANES), hb),
        pl.BlockSpec((1, G, tm // nsa_tk, HEAD_DIM, nsa_tk), vb),
        pl.BlockSpec((1, G, tm, LANES), hb),
        pl.BlockSpec((1, G, tm // nsa_tk, HEAD_DIM, nsa_tk), vb),
        pl.BlockSpec((1, N_HEADS_FOX, tm, LANES), hb),
        pl.BlockSpec((1, N_HEADS_FOX, tm, LANES), hb),
        pl.BlockSpec((1, N_HEADS_FOX, tm // fox_tk, HEAD_DIM, fox_tk), vb),
        pl.BlockSpec((1, _GATE_COLS, tm), lambda b, t: (b, 0, t)),
    ]
    row = lambda b, t: (0, 0)
    return pl.pallas_call(
        functools.partial(_inproj_kernel, tm=tm),
        grid=(B, T // tm),
        in_specs=[
            pl.BlockSpec((1, tm, D), bt),
            pl.BlockSpec((1, N_MOD, D), lambda b, t: (b, 0, 0)),
            pl.BlockSpec((1, D), row),
            pl.BlockSpec((1, tm, 1), bt),
            pl.BlockSpec((1, LANES), row),
            pl.BlockSpec((1, LANES), row),
            pl.BlockSpec((1, LANES), row),
            pl.BlockSpec((D, _W_IN_COLS), row, pipeline_mode=pl.Buffered(1)),
        ],
        out_specs=out_specs,
        out_shape=out_shape,
        scratch_shapes=[pltpu.VMEM((tm, D), BF16), pltpu.VMEM((1, LANES), F32)],
        compiler_params=_params(("arbitrary", "arbitrary")),
        name="inproj",
    )(x, mod3, g.reshape(1, D), pos_f, invf, sgn, bf_row, w)


def _gelu_tanh(x):
    c = np.float32(np.sqrt(2.0 / np.pi))
    return x * (0.5 * (1.0 + jnp.tanh(c * (x + 0.044715 * (x * x * x)))))


def _compress_kernel(zk_ref, zv_ref, pe_ref, wkt_ref, wkb_ref, wvt_ref, wvb_ref, w2k_ref, w2v_ref, kc_ref, vcT_ref):
    nsub = zk_ref.shape[1]

    def mlp(z, pe_top, pe_bot, wt_ref, wb_ref):
        a = _mm((z + pe_top).astype(BF16), wt_ref[...])
        b = _mm((z + pe_bot).astype(BF16), wb_ref[...])
        return _gelu_tanh(a + pltpu.roll(b, nsub - 1, 0))

    hk = mlp(zk_ref[0], pe_ref[0:1, :], pe_ref[1:2, :], wkt_ref, wkb_ref)
    hv = mlp(zv_ref[0], pe_ref[2:3, :], pe_ref[3:4, :], wvt_ref, wvb_ref)
    for gi in range(N_KV_NSA):
        kc_ref[0, gi] = _mm(hk[:, gi * CMP_HIDDEN:(gi + 1) * CMP_HIDDEN].astype(BF16), w2k_ref[...]).astype(BF16)
        vc = _mm(hv[:, gi * CMP_HIDDEN:(gi + 1) * CMP_HIDDEN].astype(BF16), w2v_ref[...])
        vcT_ref[0, gi] = vc.T[:HEAD_DIM, :].astype(BF16)


def _compress(zk, zv, pe4, wkt, wkb, wvt, wvb, w2k, w2v):
    B, nsub, zc = zk.shape
    G = N_KV_NSA
    c2 = lambda b: (0, 0)
    zspec = pl.BlockSpec((1, nsub, zc), lambda b: (b, 0, 0))
    wspec = pl.BlockSpec((zc, G * CMP_HIDDEN), c2)
    w2spec = pl.BlockSpec((CMP_HIDDEN, LANES), c2)
    return pl.pallas_call(
        _compress_kernel,
        grid=(B,),
        in_specs=[zspec, zspec, pl.BlockSpec((4, zc), c2), wspec, wspec, wspec, wspec, w2spec, w2spec],
        out_specs=[
            pl.BlockSpec((1, G, nsub, LANES), lambda b: (b, 0, 0, 0)),
            pl.BlockSpec((1, G, HEAD_DIM, nsub), lambda b: (b, 0, 0, 0)),
        ],
        out_shape=[
            jax.ShapeDtypeStruct((B, G, nsub, LANES), BF16),
            jax.ShapeDtypeStruct((B, G, HEAD_DIM, nsub), BF16),
        ],
        compiler_params=_params(("arbitrary",)),
        name="compress",
    )(zk, zv, pe4, wkt, wkb, wvt, wvb, w2k, w2v)


def _flash_init(m_ref, l_ref, acc_ref):
    m_ref[...] = jnp.full_like(m_ref, NEG_INF)
    l_ref[...] = jnp.zeros_like(l_ref)
    acc_ref[...] = jnp.zeros_like(acc_ref)


def _flash_step(s_lists, v_lists, refs):
    staged = []
    for s_list, (m_ref, l_ref, _) in zip(s_lists, refs):
        m_old = m_ref[...]
        m_new = m_old
        for s in s_list:
            m_new = jnp.maximum(m_new, jnp.max(s, axis=0, keepdims=True))
        alpha = jnp.exp(m_old - m_new)
        ps = [jnp.exp(s - m_new) for s in s_list]
        l_new = alpha * l_ref[...]
        for p in ps:
            l_new = l_new + jnp.sum(p, axis=0, keepdims=True)
        l_ref[...] = l_new
        m_ref[...] = m_new
        staged.append((alpha, [p.astype(BF16) for p in ps]))
    for (alpha, ps), v_list, (_, _, acc_ref) in zip(staged, v_lists, refs):
        pv = _mm(v_list[0], ps[0])
        for v, p in zip(v_list[1:], ps[1:]):
            pv = pv + _mm(v, p)
        acc_ref[...] = alpha * acc_ref[...] + pv


_FOX_HEADS_PER_STEP = 4


def _fox_kernel(q_ref, k_ref, vT_ref, o_ref, m_ref, l_ref, acc_ref, *, tq, tk):
    qi = pl.program_id(2)
    nkc = tq // tk
    nh = _FOX_HEADS_PER_STEP
    for hh in range(nh):
        _flash_init(m_ref.at[hh], l_ref.at[hh], acc_ref.at[hh])

    heads = range(nh)

    def body(kt, carry):
        k0s = [pl.multiple_of(kt * tq + c * tk, tk) for c in range(nkc)]
        s = [[_nt(k_ref[0, hh, pl.ds(k0, tk), :], q_ref[0, hh]) for k0 in k0s] for hh in heads]
        v = [[vT_ref[0, hh, kt * nkc + c] for c in range(nkc)] for hh in heads]
        _flash_step(s, v, [(m_ref.at[hh], l_ref.at[hh], acc_ref.at[hh]) for hh in heads])
        return carry

    lax.fori_loop(0, qi, body, 0)
    for c in range(nkc):
        kc = qi * nkc + c
        k0 = pl.multiple_of(kc * tk, tk)
        w = tq - c * tk
        lanes = pl.ds(c * tk, w)
        causal = lax.broadcasted_iota(jnp.int32, (tk, w), 0) <= lax.broadcasted_iota(jnp.int32, (tk, w), 1)
        s = [[jnp.where(causal, _nt(k_ref[0, hh, pl.ds(k0, tk), :], q_ref[0, hh, pl.ds(c * tk, w), :]), NEG_INF)] for hh in heads]
        v = [[vT_ref[0, hh, kc]] for hh in heads]
        _flash_step(s, v, [(m_ref.at[hh, :, lanes], l_ref.at[hh, :, lanes], acc_ref.at[hh, :, lanes]) for hh in heads])
    for pr in range(nh // 2):
        st = jnp.concatenate([acc_ref[2 * pr] / l_ref[2 * pr], acc_ref[2 * pr + 1] / l_ref[2 * pr + 1]], axis=0)
        o_ref[0, :, pr * LANES:(pr + 1) * LANES] = st.T.astype(BF16)


def _fox(qf, kf, vfT, *, tq):
    B, H, T, _ = qf.shape
    tk = vfT.shape[-1]
    nh = _FOX_HEADS_PER_STEP
    return pl.pallas_call(
        functools.partial(_fox_kernel, tq=tq, tk=tk),
        grid=(B, H // nh, T // tq),
        in_specs=[
            pl.BlockSpec((1, nh, tq, LANES), lambda b, p, i: (b, p, i, 0)),
            pl.BlockSpec((1, nh, T, LANES), lambda b, p, i: (b, p, 0, 0)),
            pl.BlockSpec((1, nh, T // tk, HEAD_DIM, tk), lambda b, p, i: (b, p, 0, 0, 0)),
        ],
        out_specs=pl.BlockSpec((1, tq, nh * HEAD_DIM), lambda b, p, i: (b, i, p)),
        out_shape=jax.ShapeDtypeStruct((B, T, H * HEAD_DIM), BF16),
        scratch_shapes=[
            pltpu.VMEM((nh, 1, tq), F32),
            pltpu.VMEM((nh, 1, tq), F32),
            pltpu.VMEM((nh, HEAD_DIM, tq), F32),
        ],
        compiler_params=_params(("arbitrary", "arbitrary", "arbitrary")),
        name="fox",
    )(qf, kf, vfT)


def _softmax_pv(s_lists, v_lists):
    staged = []
    for s_list in s_lists:
        m = jnp.max(s_list[0], axis=0, keepdims=True)
        for s in s_list[1:]:
            m = jnp.maximum(m, jnp.max(s, axis=0, keepdims=True))
        ps = [jnp.exp(s - m) for s in s_list]
        l = jnp.sum(ps[0], axis=0, keepdims=True)
        for p in ps[1:]:
            l = l + jnp.sum(p, axis=0, keepdims=True)
        staged.append((l, [p.astype(BF16) for p in ps]))
    outs = []
    for (l, ps), v_list in zip(staged, v_lists):
        pv = _mm(v_list[0], ps[0])
        for v, p in zip(v_list[1:], ps[1:]):
            pv = pv + _mm(v, p)
        outs.append(pv / l)
    return outs


_NSA_LANE_SPLITS = 2


def _nsa_kernel(q_ref, kc_ref, vcT_ref, ovT_ref, ks_ref, vsT_ref, kw_ref, vwT_ref, g_ref, o_ref,
                m_ref, l_ref, acc_ref, imp_ref, qa_ref, oc_ref, ow_ref, *, tq, tk):
    qt = pl.program_id(1)
    G = N_KV_NSA
    nq = GROUP * tq
    hw = nq // _NSA_LANE_SPLITS
    heads_per_split = GROUP // _NSA_LANE_SPLITS
    ncp = kc_ref.shape[2]
    ns = ovT_ref.shape[0]
    blk_shift = SLC_BLOCK.bit_length() - 1
    t_row = qt * tq + (lax.broadcasted_iota(jnp.int32, (1, nq), 1) & (tq - 1))
    ql = lax.broadcasted_iota(jnp.int32, (1, hw), 1) & (tq - 1)
    row_k = lax.broadcasted_iota(jnp.int32, (tk, hw), 0)
    causal = row_k <= ql
    n_io = lax.broadcasted_iota(jnp.int32, (ns, tq), 0)
    cur = lax.shift_right_logical(qt * tq + lax.broadcasted_iota(jnp.int32, (ns, tq), 1), blk_shift)
    forced = (n_io == 0) | (n_io == cur) | (n_io == cur - 1)
    visible = n_io <= cur
    c_io = lax.broadcasted_iota(jnp.int32, (ncp, nq), 0)
    cmp_valid = (c_io * CMP_STRIDE + (CMP_BLOCK - 1)) <= t_row
    ovT = ovT_ref[...]
    chains = [(g, h) for g in range(G) for h in range(_NSA_LANE_SPLITS)]

    def q_of(g):
        return q_ref[0, g * GROUP:(g + 1) * GROUP].reshape(nq, LANES)

    def q_split(g, h):
        h0 = g * GROUP + h * heads_per_split
        return q_ref[0, h0:h0 + heads_per_split].reshape(hw, LANES)

    def lanes_of(h):
        return pl.ds(h * hw, hw)

    s_cmp = [_nt(kc_ref[0, g], q_of(g)) for g in range(G)]
    staged = []
    for g in range(G):
        smk = jnp.where(cmp_valid, s_cmp[g], NEG_INF)
        mc = jnp.max(smk, axis=0, keepdims=True)
        p = jnp.where(cmp_valid, jnp.exp(smk - mc), 0.0)
        lc = jnp.sum(p, axis=0, keepdims=True)
        inv = jnp.where(lc > 0.0, 1.0 / lc, 0.0)
        pn = p * inv
        psum = pn[:, 0:tq]
        for r in range(1, GROUP):
            psum = psum + pn[:, r * tq:(r + 1) * tq]
        staged.append((p.astype(BF16), inv, _split3(psum)))
    for g in range(G):
        pb, inv, (p0, p1, p2) = staged[g]
        oc_ref[g] = _mm(vcT_ref[0, g], pb) * inv
        impT = _mm(ovT, p0) + _mm(ovT, p1) + _mm(ovT, p2)
        imp_ref[g] = jnp.where(forced, FORCE_SCORE, jnp.where(visible, impT, -1.0))

    @pl.when(qt >= 2)
    def _():
        base = pl.multiple_of((qt - 2) * tk, tk)
        s_lists = []
        for g, h in chains:
            q = q_split(g, h)
            s0 = jnp.where(row_k > ql, _nt(kw_ref[0, g, pl.ds(base, tk), :], q), NEG_INF)
            s1 = _nt(kw_ref[0, g, pl.ds(base + tk, tk), :], q)
            s2 = jnp.where(causal, _nt(kw_ref[0, g, pl.ds(base + 2 * tk, tk), :], q), NEG_INF)
            s_lists.append([s0, s1, s2])
        v_lists = [[vwT_ref[0, g, qt - 2], vwT_ref[0, g, qt - 1], vwT_ref[0, g, qt]] for g, h in chains]
        for (g, h), o in zip(chains, _softmax_pv(s_lists, v_lists)):
            ow_ref[g, :, lanes_of(h)] = o

    @pl.when(qt < 2)
    def _():
        s_lists = []
        for g, h in chains:
            q = q_split(g, h)
            s_lists.append([jnp.where((row_k + j * tk) <= (qt * tq + ql), _nt(kw_ref[0, g, j * tk:(j + 1) * tk, :], q), NEG_INF)
                            for j in range(2)])
        v_lists = [[vwT_ref[0, g, 0], vwT_ref[0, g, 1]] for g, h in chains]
        for (g, h), o in zip(chains, _softmax_pv(s_lists, v_lists)):
            ow_ref[g, :, lanes_of(h)] = o

    def rank_body(mm, cnts):
        out = []
        for g in range(G):
            imp2 = imp_ref[g]
            row = imp_ref[g, pl.ds(mm, 1), :]
            beats = (row > imp2) | ((row >= imp2) & (n_io > mm))
            out.append(cnts[g] + beats.astype(F32))
        return tuple(out)

    n_vis = jnp.minimum((qt + 1) * (tq // SLC_BLOCK), ns)
    cnts = lax.fori_loop(0, n_vis, rank_body, tuple(jnp.zeros((ns, tq), F32) for _ in range(G)))

    for g in range(G):
        bias = jnp.where(cnts[g] < float(TOP_N), 0.0, NEG_INF)
        parts = [jnp.zeros((HEAD_DIM, tq), F32), bias]
        if ns < LANES - HEAD_DIM:
            parts.append(jnp.zeros((LANES - HEAD_DIM - ns, tq), F32))
        bias_t = jnp.concatenate(parts, axis=0).T.astype(BF16)
        qa_ref[g] = q_of(g) + jnp.concatenate([bias_t] * GROUP, axis=0)
        _flash_init(m_ref.at[g], l_ref.at[g], acc_ref.at[g])

    slc_refs = [(m_ref.at[g, :, lanes_of(h)], l_ref.at[g, :, lanes_of(h)], acc_ref.at[g, :, lanes_of(h)]) for g, h in chains]

    def slc_scores(k0, g, h):
        return _nt(ks_ref[0, g, pl.ds(k0, tk), :], qa_ref[g, lanes_of(h), :])

    def slc_body(kj, carry):
        k0 = pl.multiple_of(kj * tk, tk)
        s = [[slc_scores(k0, g, h)] for g, h in chains]
        _flash_step(s, [[vsT_ref[0, g, kj]] for g, h in chains], slc_refs)
        return carry

    lax.fori_loop(0, qt, slc_body, 0)
    k0 = pl.multiple_of(qt * tk, tk)
    s = [[jnp.where(causal, slc_scores(k0, g, h), NEG_INF)] for g, h in chains]
    _flash_step(s, [[vsT_ref[0, g, qt]] for g, h in chains], slc_refs)

    for g in range(G):
        def gate_row(j):
            return jnp.concatenate([g_ref[0, g, 3 * r + j:3 * r + j + 1, :] for r in range(GROUP)], axis=1)

        oT = gate_row(0) * oc_ref[g] + gate_row(1) * (acc_ref[g] / l_ref[g]) + gate_row(2) * ow_ref[g]
        for pr in range(GROUP // 2):
            st = jnp.concatenate([oT[:, (2 * pr) * tq:(2 * pr + 1) * tq], oT[:, (2 * pr + 1) * tq:(2 * pr + 2) * tq]], axis=0)
            col = (g * GROUP // 2 + pr) * LANES
            o_ref[0, :, col:col + LANES] = st.T.astype(BF16)


def _nsa(qn, kc, vcT, ovT, ks, vsT, kw, vwT, gT4, *, tq):
    B, H, T, _ = qn.shape
    G = N_KV_NSA
    tk = vsT.shape[-1]
    ncp = kc.shape[2]
    ns = ovT.shape[0]
    assert tq == tk and WINDOW == 2 * tk and T >= 2 * tk and ns <= LANES - HEAD_DIM
    nq = GROUP * tq
    kvspec = pl.BlockSpec((1, G, T, LANES), lambda b, i: (b, 0, 0, 0))
    vtspec = pl.BlockSpec((1, G, T // tk, HEAD_DIM, tk), lambda b, i: (b, 0, 0, 0, 0))
    return pl.pallas_call(
        functools.partial(_nsa_kernel, tq=tq, tk=tk),
        grid=(B, T // tq),
        in_specs=[
            pl.BlockSpec((1, H, tq, LANES), lambda b, i: (b, 0, i, 0)),
            pl.BlockSpec((1, G, ncp, LANES), lambda b, i: (b, 0, 0, 0)),
            pl.BlockSpec((1, G, HEAD_DIM, ncp), lambda b, i: (b, 0, 0, 0)),
            pl.BlockSpec((ns, ncp), lambda b, i: (0, 0)),
            kvspec, vtspec, kvspec, vtspec,
            pl.BlockSpec((1, G, 3 * GROUP, tq), lambda b, i: (b, 0, 0, i)),
        ],
        out_specs=pl.BlockSpec((1, tq, H * HEAD_DIM), lambda b, i: (b, i, 0)),
        out_shape=jax.ShapeDtypeStruct((B, T, H * HEAD_DIM), BF16),
        scratch_shapes=[
            pltpu.VMEM((G, 1, nq), F32),
            pltpu.VMEM((G, 1, nq), F32),
            pltpu.VMEM((G, HEAD_DIM, nq), F32),
            pltpu.VMEM((G, ns, tq), F32),
            pltpu.VMEM((G, nq, LANES), BF16),
            pltpu.VMEM((G, HEAD_DIM, nq), F32),
            pltpu.VMEM((G, HEAD_DIM, nq), F32),
        ],
        compiler_params=_params(("arbitrary", "arbitrary")),
        name="nsa",
    )(qn, kc, vcT, ovT, ks, vsT, kw, vwT, gT4)


def _mixout_kernel(x_ref, mod_ref, g_ref, on_ref, of_ref, wgm_ref, wun_ref, wuf_ref, wo_ref, o_ref):
    x = x_ref[0]
    D = x.shape[-1]
    sh = mod_ref[0, 3:4, :]
    sc = mod_ref[0, 4:5, :]
    ga = mod_ref[0, 5:6, :]
    h = _rms_mod(x, g_ref[...], sh, sc).astype(BF16)
    gm = _mm(h, wgm_ref[...])
    un = _mm(on_ref[0], wun_ref[...])
    uf = _mm(of_ref[0], wuf_ref[...])
    merged = jax.nn.sigmoid(gm[:, :D]) * un + jax.nn.sigmoid(gm[:, D:]) * uf
    y = _mm(merged.astype(BF16), wo_ref[...])
    o_ref[0] = x + ga * y


def _mixout(x, mod3, g, o_nsa, o_fox, wgm, wun, wuf, wo, *, tm):
    B, T, D = x.shape
    bt = lambda b, t: (b, t, 0)
    c2 = lambda b, t: (0, 0)
    dn = o_nsa.shape[-1]
    return pl.pallas_call(
        _mixout_kernel,
        grid=(B, T // tm),
        in_specs=[
            pl.BlockSpec((1, tm, D), bt),
            pl.BlockSpec((1, N_MOD, D), lambda b, t: (b, 0, 0)),
            pl.BlockSpec((1, D), c2),
            pl.BlockSpec((1, tm, dn), bt),
            pl.BlockSpec((1, tm, dn), bt),
            pl.BlockSpec((D, 2 * D), c2, pipeline_mode=pl.Buffered(1)),
            pl.BlockSpec((dn, D), c2, pipeline_mode=pl.Buffered(1)),
            pl.BlockSpec((dn, D), c2, pipeline_mode=pl.Buffered(1)),
            pl.BlockSpec((D, D), c2, pipeline_mode=pl.Buffered(1)),
        ],
        out_specs=pl.BlockSpec((1, tm, D), bt),
        out_shape=jax.ShapeDtypeStruct((B, T, D), F32),
        compiler_params=_params(("arbitrary", "arbitrary")),
        name="mixout",
    )(x, mod3, g.reshape(1, D), o_nsa, o_fox, wgm, wun, wuf, wo)


def _chunk_ffn_weights(w_gate, w_up, w_down, ck):
    D, F = w_gate.shape
    n = F // ck
    wg = w_gate.astype(BF16).reshape(D, n, ck).transpose(1, 0, 2)
    wu = w_up.astype(BF16).reshape(D, n, ck).transpose(1, 0, 2)
    wd = w_down.astype(BF16).reshape(n, ck, D)
    return wg, wu, wd


def _compress_weights(pe, w1, w2):
    half = CMP_BLOCK // 2
    eye = jnp.eye(N_KV_NSA, dtype=F32)

    def expand(w_half):
        w3 = w_half.reshape(half, HEAD_DIM, CMP_HIDDEN)
        return jnp.einsum('jdn,gh->jgdhn', w3, eye).reshape(half * N_KV_NSA * HEAD_DIM, N_KV_NSA * CMP_HIDDEN).astype(BF16)

    def pe_row(pe_half):
        return jnp.broadcast_to(pe_half[:, None, :], (half, N_KV_NSA, HEAD_DIM)).reshape(1, -1)

    w2p = jnp.pad(w2, ((0, 0), (0, LANES - HEAD_DIM))).astype(BF16)
    return (pe_row(pe[:half]), pe_row(pe[half:]), expand(w1[:half * HEAD_DIM]), expand(w1[half * HEAD_DIM:]), w2p)


def kernel(x, c, positions, w_ada, b_ada, g_ffn1, w_gate1, w_up1, w_down1, g_mix, w_in, b_forget, pe_ck, w1_ck, w2_ck, pe_cv, w1_cv, w2_cv, w_up_nsa, w_up_fox, w_o, g_ffn2, w_gate2, w_up2, w_down2, g_final):
    B, T, D = x.shape
    depth = w_ada.shape[0]
    tm = 512
    ffn_ck = 256
    nsa_tq, nsa_tk = 256, 256
    fox_tq, fox_tk = 512, 256
    n_slc = T // SLC_BLOCK
    n_sub = T // CMP_STRIDE

    half = ROPE_DIM // 2
    inv_freq = ROPE_THETA ** (-jnp.arange(half, dtype=F32) / half)
    d_in_head = jnp.arange(LANES) % HEAD_DIM
    invf = jnp.where(d_in_head < ROPE_DIM, inv_freq[d_in_head % half], 0.0).astype(F32).reshape(1, LANES)
    sgn = jnp.where(d_in_head < half, -1.0, 1.0).astype(F32).reshape(1, LANES)
    cmp_start = np.arange(n_sub) * CMP_STRIDE
    slc_start = np.arange(n_slc) * SLC_BLOCK
    ov = ((cmp_start[:, None] < slc_start[None, :] + SLC_BLOCK) & (slc_start[None, :] < cmp_start[:, None] + CMP_BLOCK))
    ov[n_sub - CMP_BLOCK // CMP_STRIDE + 1:, :] = False
    ovT = jnp.asarray(ov.T, dtype=BF16)

    pos_f = positions.astype(F32)[..., None]
    c_in = c
    for l in range(depth):
        mod3 = _ada(c_in, w_ada[l], b_ada[l]).reshape(B, N_MOD, D)
        wg1, wu1, wd1 = _chunk_ffn_weights(w_gate1[l], w_up1[l], w_down1[l], ffn_ck)
        wg2, wu2, wd2 = _chunk_ffn_weights(w_gate2[l], w_up2[l], w_down2[l], ffn_ck)
        wl = w_in[l]
        small = jnp.concatenate([wl[:, 1280:1304], wl[:, 2840:2848], jnp.zeros((D, LANES - 32), F32)], axis=1)
        w_proj = jnp.concatenate([wl[:, :1280], wl[:, 1304:2840], small], axis=1).astype(BF16)
        w_gm = wl[:, 2848:].astype(BF16)
        bf_row = jnp.zeros((1, LANES), F32).at[0, _GATE_COLS:_GATE_COLS + N_HEADS_FOX].set(b_forget[l])

        x = _ffn(x, mod3, g_ffn1[l], wg1, wu1, wd1, g_final, k_mod=0, final=False, tm=tm)

        (qn, kcr, vcr, ks, vsT, kw, vwT, qf, kf, vfT, gT) = _inproj(
            x, mod3, g_mix[l], pos_f, invf, sgn, bf_row, w_proj, tm=tm, nsa_tk=nsa_tk, fox_tk=fox_tk)

        pk_t, pk_b, wk_t, wk_b, w2k = _compress_weights(pe_ck[l], w1_ck[l], w2_ck[l])
        pv_t, pv_b, wv_t, wv_b, w2v = _compress_weights(pe_cv[l], w1_cv[l], w2_cv[l])
        pe4 = jnp.concatenate([pk_t, pk_b, pv_t, pv_b], axis=0)
        zc = CMP_STRIDE * LANES
        kc, vcT = _compress(kcr.reshape(B, n_sub, zc), vcr.reshape(B, n_sub, zc), pe4, wk_t, wk_b, wv_t, wv_b, w2k, w2v)

        o_fox = _fox(qf, kf, vfT, tq=fox_tq)
        gT4 = gT.reshape(B, N_KV_NSA, 3 * GROUP, T)
        o_nsa = _nsa(qn, kc, vcT, ovT, ks, vsT, kw, vwT, gT4, tq=nsa_tq)

        x = _mixout(x, mod3, g_mix[l], o_nsa, o_fox, w_gm, w_up_nsa[l].astype(BF16), w_up_fox[l].astype(BF16),
                    w_o[l].astype(BF16), tm=tm)
        last = l == depth - 1
        x = _ffn(x, mod3, g_ffn2[l], wg2, wu2, wd2, g_final, k_mod=6, final=last, tm=tm)
    return x
```

```python
import functools

import numpy as np
import jax
import jax.numpy as jnp
from jax import lax
from jax.experimental import pallas as pl
from jax.experimental.pallas import tpu as pltpu

HEAD_DIM = 64
N_HEADS_NSA = 8
N_KV_NSA = 2
GROUP = N_HEADS_NSA // N_KV_NSA
N_HEADS_FOX = 8
CMP_BLOCK = 32
CMP_STRIDE = 16
CMP_HIDDEN = 128
SLC_BLOCK = 64
TOP_N = 16
WINDOW = 512
ROPE_THETA = 500000.0
ROPE_DIM = HEAD_DIM // 4
N_MOD = 9
RMS_EPS = 1e-6
NEG_INF = -1e30
FORCE_SCORE = 1e4

LANES = 128
LOG2E = 1.4426950408889634
V_ROWS = HEAD_DIM + 16
F32 = jnp.float32
BF16 = jnp.bfloat16
VMEM_LIMIT = 56 * 1024 * 1024

_OFF_QN = 0
_OFF_KV = 512
_OFF_QF = 1280
_OFF_KF = 1792
_OFF_VF = 2304
_OFF_SMALL = 2816
_W_IN_COLS = 2944
_GATE_COLS = 3 * N_HEADS_NSA
_TAIL = HEAD_DIM


def _params(sem):
    return pltpu.CompilerParams(dimension_semantics=sem, vmem_limit_bytes=VMEM_LIMIT)


def _nt(a, b):
    return lax.dot_general(a, b, (((1,), (1,)), ((), ())), preferred_element_type=F32)


def _mm(a, b):
    return jnp.dot(a, b, preferred_element_type=F32)


def _split3(x):
    hi = x.astype(BF16)
    r = x - hi.astype(F32)
    mid = r.astype(BF16)
    lo = (r - mid.astype(F32)).astype(BF16)
    return hi, mid, lo


def _rms_mod(x, g, shift, scale):
    ms = jnp.mean(x * x, axis=-1, keepdims=True)
    y = x * lax.rsqrt(ms + RMS_EPS) * g
    return y * (1.0 + scale) + shift


def _ada_kernel(c_ref, w_ref, b_ref, o_ref):
    c = c_ref[...]
    ca = c * jax.nn.sigmoid(c)
    h0, h1, h2 = _split3(ca)
    w0, w1, w2 = _split3(w_ref[...])
    acc = _mm(h0, w0) + _mm(h0, w1) + _mm(h1, w0)
    acc = acc + (_mm(h1, w1) + _mm(h0, w2) + _mm(h2, w0))
    o_ref[...] = acc + b_ref[...]


def _ada(c, w_ada, b_ada):
    B, D = c.shape
    n = w_ada.shape[1]
    tn = 1024
    return pl.pallas_call(
        _ada_kernel,
        grid=(n // tn,),
        in_specs=[
            pl.BlockSpec((B, D), lambda j: (0, 0)),
            pl.BlockSpec((D, tn), lambda j: (0, j)),
            pl.BlockSpec((1, tn), lambda j: (0, j)),
        ],
        out_specs=pl.BlockSpec((B, tn), lambda j: (0, j)),
        out_shape=jax.ShapeDtypeStruct((B, n), F32),
        compiler_params=_params(("arbitrary",)),
        name="ada",
    )(c, w_ada, b_ada.reshape(1, n))


def _ffn_kernel(x_ref, mod_ref, g_ref, wg_ref, wu_ref, wd_ref, gf_ref, o_ref, h_ref, acc_ref, *, k_mod, final):
    x = x_ref[0]
    sh = mod_ref[0, k_mod:k_mod + 1, :]
    sc = mod_ref[0, k_mod + 1:k_mod + 2, :]
    ga = mod_ref[0, k_mod + 2:k_mod + 3, :]
    h_ref[...] = _rms_mod(x, g_ref[...], sh, sc).astype(BF16)
    acc_ref[...] = jnp.zeros_like(acc_ref)

    def body(j, carry):
        h = h_ref[...]
        g = _mm(h, wg_ref[j])
        u = _mm(h, wu_ref[j])
        a = (g * jax.nn.sigmoid(g) * u).astype(BF16)
        acc_ref[...] += _mm(a, wd_ref[j])
        return carry

    lax.fori_loop(0, wg_ref.shape[0], body, 0)
    out = x + 0.5 * ga * acc_ref[...]
    if final:
        ms = jnp.mean(out * out, axis=-1, keepdims=True)
        out = out * lax.rsqrt(ms + RMS_EPS) * gf_ref[...]
    o_ref[0] = out


def _ffn(x, mod3, g, wg, wu, wd, g_final, *, k_mod, final, tm):
    B, T, D = x.shape
    nch, _, ck = wg.shape
    const3 = lambda b, t: (0, 0, 0)
    return pl.pallas_call(
        functools.partial(_ffn_kernel, k_mod=k_mod, final=final),
        grid=(B, T // tm),
        in_specs=[
            pl.BlockSpec((1, tm, D), lambda b, t: (b, t, 0)),
            pl.BlockSpec((1, N_MOD, D), lambda b, t: (b, 0, 0)),
            pl.BlockSpec((1, D), lambda b, t: (0, 0)),
            pl.BlockSpec((nch, D, ck), const3, pipeline_mode=pl.Buffered(1)),
            pl.BlockSpec((nch, D, ck), const3, pipeline_mode=pl.Buffered(1)),
            pl.BlockSpec((nch, ck, D), const3, pipeline_mode=pl.Buffered(1)),
            pl.BlockSpec((1, D), lambda b, t: (0, 0)),
        ],
        out_specs=pl.BlockSpec((1, tm, D), lambda b, t: (b, t, 0)),
        out_shape=jax.ShapeDtypeStruct((B, T, D), F32),
        scratch_shapes=[pltpu.VMEM((tm, D), BF16), pltpu.VMEM((tm, D), F32)],
        compiler_params=_params(("arbitrary", "arbitrary")),
        name="ffn_final" if final else "ffn",
    )(x, mod3, g.reshape(1, D), wg, wu, wd, g_final.reshape(1, D))


def _inproj_kernel(x_ref, mod_ref, g_ref, pos_ref, invf_ref, sgn_ref, bf_ref, w_ref,
                   qn_ref, kcr_ref, vcr_ref, ks_ref, vsT_ref, kw_ref, vwT_ref, qf_ref, kf_ref, vfT_ref, gT_ref,
                   h_ref, carry_ref, *, tm):
    t_idx = pl.program_id(1)
    sh = mod_ref[0, 3:4, :]
    sc = mod_ref[0, 4:5, :]
    h_ref[...] = _rms_mod(x_ref[0], g_ref[...], sh, sc).astype(BF16)
    h = h_ref[...]

    lane = lax.broadcasted_iota(jnp.int32, (tm, LANES), 1)
    low = lane < HEAD_DIM
    first8 = (lane & (HEAD_DIM - 1)) < (ROPE_DIM // 2)
    ang = pos_ref[0] * invf_ref[...]
    cos_t = jnp.cos(ang)
    sin_t = jnp.sin(ang) * sgn_ref[...]

    def rope(xs):
        partner = jnp.where(first8, pltpu.roll(xs, LANES - ROPE_DIM // 2, 1), pltpu.roll(xs, ROPE_DIM // 2, 1))
        return xs * cos_t + partner * sin_t

    def split_heads(xs):
        return jnp.where(low, xs, 0.0), jnp.where(low, pltpu.roll(xs, HEAD_DIM, 1), 0.0)

    def with_ones(vt):
        extra = (lax.broadcasted_iota(jnp.int32, (V_ROWS - HEAD_DIM, vt.shape[1]), 0) == 0).astype(BF16)
        return jnp.concatenate([vt, extra], axis=0)

    sm = _mm(h, w_ref[:, _OFF_SMALL:_OFF_SMALL + LANES])
    gT_ref[0] = jax.nn.sigmoid(sm).T[:_GATE_COLS, :]
    xl = sm + bf_ref[...]
    logf = jnp.minimum(xl, 0.0) - jnp.log1p(jnp.exp(-jnp.abs(xl)))
    tri = (lax.broadcasted_iota(jnp.int32, (tm, tm), 0) >= lax.broadcasted_iota(jnp.int32, (tm, tm), 1)).astype(BF16)
    l0, l1, l2 = _split3(logf)

    @pl.when(t_idx == 0)
    def _():
        carry_ref[...] = jnp.zeros_like(carry_ref)

    cf = _mm(tri, l0) + _mm(tri, l1) + _mm(tri, l2) + carry_ref[...]
    carry_ref[...] = cf[tm - 1:tm, :]
    cf2 = cf * LOG2E
    c_hi = cf2.astype(BF16).astype(F32)
    c_r = cf2 - c_hi
    c_mid = c_r.astype(BF16).astype(F32)
    c_lo = c_r - c_mid

    def tails(hd):
        col = _GATE_COLS + hd
        hi = jnp.broadcast_to(c_hi[:, col:col + 1], (tm, LANES))
        mid = jnp.broadcast_to(c_mid[:, col:col + 1], (tm, LANES))
        lo = jnp.broadcast_to(c_lo[:, col:col + 1], (tm, LANES))
        ones3 = (lane >= _TAIL) & (lane < _TAIL + 3)
        ones3b = (lane >= _TAIL + 3) & (lane < _TAIL + 6)
        qt = jnp.where(ones3, 1.0, jnp.where(lane == _TAIL + 3, hi, jnp.where(lane == _TAIL + 4, mid, jnp.where(lane == _TAIL + 5, lo, 0.0))))
        kt = jnp.where(ones3b, 1.0, jnp.where(lane == _TAIL, -hi, jnp.where(lane == _TAIL + 1, -mid, jnp.where(lane == _TAIL + 2, -lo, 0.0))))
        return qt, kt

    qn = _mm(h, w_ref[:, _OFF_QN:_OFF_QN + 512])
    scale = HEAD_DIM ** -0.5 * LOG2E
    for j in range(4):
        a, b = split_heads(rope(qn[:, j * LANES:(j + 1) * LANES]) * scale)
        qn_ref[0, 2 * j] = a.astype(BF16)
        qn_ref[0, 2 * j + 1] = b.astype(BF16)

    nsa_tk = vsT_ref.shape[-1]
    kv = _mm(h, w_ref[:, _OFF_KV:_OFF_KV + 768])
    kcr_ref[0] = rope(kv[:, 0:LANES])
    vcr_ref[0] = kv[:, LANES:2 * LANES]
    blk = lax.shift_right_logical(t_idx * tm + lax.broadcasted_iota(jnp.int32, (tm, LANES), 0), SLC_BLOCK.bit_length() - 1)
    onehot = (lane == blk + HEAD_DIM).astype(F32)
    for k_out, v_out, off, tail in ((ks_ref, vsT_ref, 2 * LANES, onehot), (kw_ref, vwT_ref, 4 * LANES, None)):
        a, b = split_heads(rope(kv[:, off:off + LANES]))
        if tail is not None:
            a, b = a + tail, b + tail
        k_out[0, 0] = a.astype(BF16)
        k_out[0, 1] = b.astype(BF16)
        vT = kv[:, off + LANES:off + 2 * LANES].T.astype(BF16)
        for gi in range(N_KV_NSA):
            for c in range(tm // nsa_tk):
                v_out[0, gi, c] = with_ones(vT[gi * HEAD_DIM:(gi + 1) * HEAD_DIM, c * nsa_tk:(c + 1) * nsa_tk])

    qf = _mm(h, w_ref[:, _OFF_QF:_OFF_QF + 512])
    kf = _mm(h, w_ref[:, _OFF_KF:_OFF_KF + 512])
    for j in range(4):
        qa, qb = split_heads(qf[:, j * LANES:(j + 1) * LANES] * scale)
        ka, kb = split_heads(kf[:, j * LANES:(j + 1) * LANES])
        qta, kta = tails(2 * j)
        qtb, ktb = tails(2 * j + 1)
        qf_ref[0, 2 * j] = (qa + qta).astype(BF16)
        qf_ref[0, 2 * j + 1] = (qb + qtb).astype(BF16)
        kf_ref[0, 2 * j] = (ka + kta).astype(BF16)
        kf_ref[0, 2 * j + 1] = (kb + ktb).astype(BF16)
    vf = _mm(h, w_ref[:, _OFF_VF:_OFF_VF + 512])
    fox_tk = vfT_ref.shape[-1]
    for j in range(4):
        vT = vf[:, j * LANES:(j + 1) * LANES].T.astype(BF16)
        for hh in range(2):
            for c in range(tm // fox_tk):
                vfT_ref[0, 2 * j + hh, c] = with_ones(vT[hh * HEAD_DIM:(hh + 1) * HEAD_DIM, c * fox_tk:(c + 1) * fox_tk])


def _inproj(x, mod3, g, pos_f, invf, sgn, bf_row, w, *, tm, nsa_tk, fox_tk):
    B, T, D = x.shape
    H, G = N_HEADS_NSA, N_KV_NSA
    bt = lambda b, t: (b, t, 0)
    hb = lambda b, t: (b, 0, t, 0)
    vb = lambda b, t: (b, 0, t, 0, 0)
    out_shape = [
        jax.ShapeDtypeStruct((B, H, T, LANES), BF16),
        jax.ShapeDtypeStruct((B, T, LANES), F32),
        jax.ShapeDtypeStruct((B, T, LANES), F32),
        jax.ShapeDtypeStruct((B, G, T, LANES), BF16),
        jax.ShapeDtypeStruct((B, G, T // nsa_tk, V_ROWS, nsa_tk), BF16),
        jax.ShapeDtypeStruct((B, G, T, LANES), BF16),
        jax.ShapeDtypeStruct((B, G, T // nsa_tk, V_ROWS, nsa_tk), BF16),
        jax.ShapeDtypeStruct((B, N_HEADS_FOX, T, LANES), BF16),
        jax.ShapeDtypeStruct((B, N_HEADS_FOX, T, LANES), BF16),
        jax.ShapeDtypeStruct((B, N_HEADS_FOX, T // fox_tk, V_ROWS, fox_tk), BF16),
        jax.ShapeDtypeStruct((B, _GATE_COLS, T), F32),
    ]
    out_specs = [
        pl.BlockSpec((1, H, tm, LANES), hb),
        pl.BlockSpec((1, tm, LANES), bt),
        pl.BlockSpec((1, tm, LANES), bt),
        pl.BlockSpec((1, G, tm, LANES), hb),
        pl.BlockSpec((1, G, tm // nsa_tk, V_ROWS, nsa_tk), vb),
        pl.BlockSpec((1, G, tm, LANES), hb),
        pl.BlockSpec((1, G, tm // nsa_tk, V_ROWS, nsa_tk), vb),
        pl.BlockSpec((1, N_HEADS_FOX, tm, LANES), hb),
        pl.BlockSpec((1, N_HEADS_FOX, tm, LANES), hb),
        pl.BlockSpec((1, N_HEADS_FOX, tm // fox_tk, V_ROWS, fox_tk), vb),
        pl.BlockSpec((1, _GATE_COLS, tm), lambda b, t: (b, 0, t)),
    ]
    row = lambda b, t: (0, 0)
    return pl.pallas_call(
        functools.partial(_inproj_kernel, tm=tm),
        grid=(B, T // tm),
        in_specs=[
            pl.BlockSpec((1, tm, D), bt),
            pl.BlockSpec((1, N_MOD, D), lambda b, t: (b, 0, 0)),
            pl.BlockSpec((1, D), row),
            pl.BlockSpec((1, tm, 1), bt),
            pl.BlockSpec((1, LANES), row),
            pl.BlockSpec((1, LANES), row),
            pl.BlockSpec((1, LANES), row),
            pl.BlockSpec((D, _W_IN_COLS), row, pipeline_mode=pl.Buffered(1)),
        ],
        out_specs=out_specs,
        out_shape=out_shape,
        scratch_shapes=[pltpu.VMEM((tm, D), BF16), pltpu.VMEM((1, LANES), F32)],
        compiler_params=_params(("arbitrary", "arbitrary")),
        name="inproj",
    )(x, mod3, g.reshape(1, D), pos_f, invf, sgn, bf_row, w)


def _gelu_tanh(x):
    c = np.float32(np.sqrt(2.0 / np.pi))
    return x * (0.5 * (1.0 + jnp.tanh(c * (x + 0.044715 * (x * x * x)))))


def _compress_kernel(zk_ref, zv_ref, pe_ref, wkt_ref, wkb_ref, wvt_ref, wvb_ref, w2k_ref, w2v_ref, kc_ref, vcT_ref):
    nsub = zk_ref.shape[1]

    def mlp(z, pe_top, pe_bot, wt_ref, wb_ref):
        a = _mm((z + pe_top).astype(BF16), wt_ref[...])
        b = _mm((z + pe_bot).astype(BF16), wb_ref[...])
        return _gelu_tanh(a + pltpu.roll(b, nsub - 1, 0))

    hk = mlp(zk_ref[0], pe_ref[0:1, :], pe_ref[1:2, :], wkt_ref, wkb_ref)
    hv = mlp(zv_ref[0], pe_ref[2:3, :], pe_ref[3:4, :], wvt_ref, wvb_ref)
    for gi in range(N_KV_NSA):
        kc_ref[0, gi] = _mm(hk[:, gi * CMP_HIDDEN:(gi + 1) * CMP_HIDDEN].astype(BF16), w2k_ref[...]).astype(BF16)
        vc = _mm(hv[:, gi * CMP_HIDDEN:(gi + 1) * CMP_HIDDEN].astype(BF16), w2v_ref[...])
        vcT_ref[0, gi] = vc.T[:HEAD_DIM, :].astype(BF16)


def _compress(zk, zv, pe4, wkt, wkb, wvt, wvb, w2k, w2v):
    B, nsub, zc = zk.shape
    G = N_KV_NSA
    c2 = lambda b: (0, 0)
    zspec = pl.BlockSpec((1, nsub, zc), lambda b: (b, 0, 0))
    wspec = pl.BlockSpec((zc, G * CMP_HIDDEN), c2)
    w2spec = pl.BlockSpec((CMP_HIDDEN, LANES), c2)
    return pl.pallas_call(
        _compress_kernel,
        grid=(B,),
        in_specs=[zspec, zspec, pl.BlockSpec((4, zc), c2), wspec, wspec, wspec, wspec, w2spec, w2spec],
        out_specs=[
            pl.BlockSpec((1, G, nsub, LANES), lambda b: (b, 0, 0, 0)),
            pl.BlockSpec((1, G, HEAD_DIM, nsub), lambda b: (b, 0, 0, 0)),
        ],
        out_shape=[
            jax.ShapeDtypeStruct((B, G, nsub, LANES), BF16),
            jax.ShapeDtypeStruct((B, G, HEAD_DIM, nsub), BF16),
        ],
        compiler_params=_params(("arbitrary",)),
        name="compress",
    )(zk, zv, pe4, wkt, wkb, wvt, wvb, w2k, w2v)


def _flash_init(m_ref, acc_ref):
    m_ref[...] = jnp.full_like(m_ref, NEG_INF)
    acc_ref[...] = jnp.zeros_like(acc_ref)


def _chain_softmax(s_list, m_ref):
    m_old = m_ref[...]
    m_new = m_old
    for s in s_list:
        m_new = jnp.maximum(m_new, jnp.max(s, axis=0, keepdims=True))
    m_ref[...] = m_new
    return jnp.exp2(m_old - m_new), [jnp.exp2(s - m_new).astype(BF16) for s in s_list]


def _chain_values(staged, v_list, acc_ref):
    alpha, ps = staged
    pv = _mm(v_list[0], ps[0])
    for v, p in zip(v_list[1:], ps[1:]):
        pv = pv + _mm(v, p)
    acc_ref[...] = alpha * acc_ref[...] + pv


def _normalized(acc):
    return acc[:HEAD_DIM] / acc[HEAD_DIM:HEAD_DIM + 1]


def _pipelined_sweep(n, n_chains, store, softmax, values):
    chains = range(n_chains)

    def by_parity(i, fn):
        @pl.when((i & 1) == 0)
        def _():
            fn(0)

        @pl.when((i & 1) == 1)
        def _():
            fn(1)

    def step(nxt, masked, cur, par):
        staged = None
        for c in chains:
            if nxt is not None:
                store(nxt, 1 - par, masked, c)
            new = softmax(cur, par, c)
            if staged is not None:
                values(cur, staged, c - 1)
            staged = new
        values(cur, staged, n_chains - 1)

    @pl.when(n == 0)
    def _():
        for c in chains:
            store(0, 0, True, c)

    @pl.when(n > 0)
    def _():
        for c in chains:
            store(0, 0, False, c)

    def body(i, carry):
        by_parity(i, lambda par: step(i + 1, False, i, par))
        return carry

    lax.fori_loop(0, n - 1, body, 0)

    @pl.when(n > 0)
    def _():
        by_parity(n - 1, lambda par: step(n, True, n - 1, par))

    by_parity(n, lambda par: step(None, False, n, par))


_FOX_HEADS_PER_STEP = 4


def _fox_kernel(q_ref, k_ref, vT_ref, o_ref, m_ref, acc_ref, s0_ref, s1_ref, *, tq, tk):
    qi = pl.program_id(2)
    nkc = tq // tk
    heads = range(_FOX_HEADS_PER_STEP)
    s_bufs = (s0_ref, s1_ref)
    row = lax.broadcasted_iota(jnp.int32, (tk, tq), 0)
    lane = lax.broadcasted_iota(jnp.int32, (tk, tq), 1)
    for hh in heads:
        _flash_init(m_ref.at[hh], acc_ref.at[hh])

    def store(kt, buf, masked, hh):
        for c in range(nkc):
            k0 = pl.multiple_of(kt * tq + c * tk, tk)
            s = _nt(k_ref[0, hh, pl.ds(k0, tk), :], q_ref[0, hh])
            if masked:
                s = jnp.where(row + c * tk <= lane, s, NEG_INF)
            s_bufs[buf][hh, c * tk:(c + 1) * tk, :] = s

    def softmax(kt, buf, hh):
        return _chain_softmax([s_bufs[buf][hh, c * tk:(c + 1) * tk, :] for c in range(nkc)], m_ref.at[hh])

    def values(kt, staged, hh):
        _chain_values(staged, [vT_ref[0, hh, kt * nkc + c] for c in range(nkc)], acc_ref.at[hh])

    _pipelined_sweep(qi, len(heads), store, softmax, values)
    for pr in range(_FOX_HEADS_PER_STEP // 2):
        st = jnp.concatenate([_normalized(acc_ref[2 * pr]), _normalized(acc_ref[2 * pr + 1])], axis=0)
        o_ref[0, :, pr * LANES:(pr + 1) * LANES] = st.T.astype(BF16)


def _fox(qf, kf, vfT, *, tq):
    B, H, T, _ = qf.shape
    tk = vfT.shape[-1]
    nh = _FOX_HEADS_PER_STEP
    return pl.pallas_call(
        functools.partial(_fox_kernel, tq=tq, tk=tk),
        grid=(B, H // nh, T // tq),
        in_specs=[
            pl.BlockSpec((1, nh, tq, LANES), lambda b, p, i: (b, p, i, 0)),
            pl.BlockSpec((1, nh, T, LANES), lambda b, p, i: (b, p, 0, 0)),
            pl.BlockSpec((1, nh, T // tk, V_ROWS, tk), lambda b, p, i: (b, p, 0, 0, 0)),
        ],
        out_specs=pl.BlockSpec((1, tq, nh * HEAD_DIM), lambda b, p, i: (b, i, p)),
        out_shape=jax.ShapeDtypeStruct((B, T, H * HEAD_DIM), BF16),
        scratch_shapes=[
            pltpu.VMEM((nh, 1, tq), F32),
            pltpu.VMEM((nh, V_ROWS, tq), F32),
            pltpu.VMEM((nh, tq, tq), F32),
            pltpu.VMEM((nh, tq, tq), F32),
        ],
        compiler_params=_params(("arbitrary", "arbitrary", "arbitrary")),
        name="fox",
    )(qf, kf, vfT)


def _softmax_pv(s_lists, v_lists):
    staged = []
    for s_list in s_lists:
        m = jnp.max(s_list[0], axis=0, keepdims=True)
        for s in s_list[1:]:
            m = jnp.maximum(m, jnp.max(s, axis=0, keepdims=True))
        staged.append([jnp.exp2(s - m).astype(BF16) for s in s_list])
    outs = []
    for ps, v_list in zip(staged, v_lists):
        pv = _mm(v_list[0], ps[0])
        for v, p in zip(v_list[1:], ps[1:]):
            pv = pv + _mm(v, p)
        outs.append(_normalized(pv))
    return outs


_NSA_LANE_SPLITS = 2


def _nsa_kernel(q_ref, kc_ref, vcT_ref, ovT_ref, ks_ref, vsT_ref, kw_ref, vwT_ref, g_ref, o_ref,
                m_ref, acc_ref, imp_ref, qa_ref, oc_ref, ow_ref, s0_ref, s1_ref, *, tq, tk):
    qt = pl.program_id(1)
    G = N_KV_NSA
    nq = GROUP * tq
    hw = nq // _NSA_LANE_SPLITS
    heads_per_split = GROUP // _NSA_LANE_SPLITS
    ncp = kc_ref.shape[2]
    ns = ovT_ref.shape[0]
    blk_shift = SLC_BLOCK.bit_length() - 1
    t_row = qt * tq + (lax.broadcasted_iota(jnp.int32, (1, nq), 1) & (tq - 1))
    ql = lax.broadcasted_iota(jnp.int32, (1, hw), 1) & (tq - 1)
    row_k = lax.broadcasted_iota(jnp.int32, (tk, hw), 0)
    causal = row_k <= ql
    n_io = lax.broadcasted_iota(jnp.int32, (ns, tq), 0)
    cur = lax.shift_right_logical(qt * tq + lax.broadcasted_iota(jnp.int32, (ns, tq), 1), blk_shift)
    forced = (n_io == 0) | (n_io == cur) | (n_io == cur - 1)
    visible = n_io <= cur
    c_io = lax.broadcasted_iota(jnp.int32, (ncp, nq), 0)
    cmp_valid = (c_io * CMP_STRIDE + (CMP_BLOCK - 1)) <= t_row
    ovT = ovT_ref[...]
    chains = [(g, h) for g in range(G) for h in range(_NSA_LANE_SPLITS)]

    def q_of(g):
        return q_ref[0, g * GROUP:(g + 1) * GROUP].reshape(nq, LANES)

    def q_split(g, h):
        h0 = g * GROUP + h * heads_per_split
        return q_ref[0, h0:h0 + heads_per_split].reshape(hw, LANES)

    def lanes_of(h):
        return pl.ds(h * hw, hw)

    s_cmp = [_nt(kc_ref[0, g], q_of(g)) for g in range(G)]
    staged = []
    for g in range(G):
        smk = jnp.where(cmp_valid, s_cmp[g], NEG_INF)
        mc = jnp.max(smk, axis=0, keepdims=True)
        p = jnp.where(cmp_valid, jnp.exp2(smk - mc), 0.0)
        lc = jnp.sum(p, axis=0, keepdims=True)
        inv = jnp.where(lc > 0.0, 1.0 / lc, 0.0)
        pn = p * inv
        psum = pn[:, 0:tq]
        for r in range(1, GROUP):
            psum = psum + pn[:, r * tq:(r + 1) * tq]
        staged.append((p.astype(BF16), inv, _split3(psum)))
    for g in range(G):
        pb, inv, (p0, p1, p2) = staged[g]
        oc_ref[g] = _mm(vcT_ref[0, g], pb) * inv
        impT = _mm(ovT, p0) + _mm(ovT, p1) + _mm(ovT, p2)
        imp_ref[g] = jnp.where(forced, FORCE_SCORE, jnp.where(visible, impT, -1.0))

    @pl.when(qt >= 2)
    def _():
        base = pl.multiple_of((qt - 2) * tk, tk)
        s_lists = []
        for g, h in chains:
            q = q_split(g, h)
            s0 = jnp.where(row_k > ql, _nt(kw_ref[0, g, pl.ds(base, tk), :], q), NEG_INF)
            s1 = _nt(kw_ref[0, g, pl.ds(base + tk, tk), :], q)
            s2 = jnp.where(causal, _nt(kw_ref[0, g, pl.ds(base + 2 * tk, tk), :], q), NEG_INF)
            s_lists.append([s0, s1, s2])
        v_lists = [[vwT_ref[0, g, qt - 2], vwT_ref[0, g, qt - 1], vwT_ref[0, g, qt]] for g, h in chains]
        for (g, h), o in zip(chains, _softmax_pv(s_lists, v_lists)):
            ow_ref[g, :, lanes_of(h)] = o

    @pl.when(qt < 2)
    def _():
        s_lists = []
        for g, h in chains:
            q = q_split(g, h)
            s_lists.append([jnp.where((row_k + j * tk) <= (qt * tq + ql), _nt(kw_ref[0, g, j * tk:(j + 1) * tk, :], q), NEG_INF)
                            for j in range(2)])
        v_lists = [[vwT_ref[0, g, 0], vwT_ref[0, g, 1]] for g, h in chains]
        for (g, h), o in zip(chains, _softmax_pv(s_lists, v_lists)):
            ow_ref[g, :, lanes_of(h)] = o

    def rank_body(mm, cnts):
        out = []
        for g in range(G):
            imp2 = imp_ref[g]
            row = imp_ref[g, pl.ds(mm, 1), :]
            beats = (row > imp2) | ((row >= imp2) & (n_io > mm))
            out.append(cnts[g] + beats.astype(F32))
        return tuple(out)

    n_vis = jnp.minimum((qt + 1) * (tq // SLC_BLOCK), ns)
    cnts = lax.fori_loop(0, n_vis, rank_body, tuple(jnp.zeros((ns, tq), F32) for _ in range(G)))

    for g in range(G):
        bias = jnp.where(cnts[g] < float(TOP_N), 0.0, NEG_INF)
        parts = [jnp.zeros((HEAD_DIM, tq), F32), bias]
        if ns < LANES - HEAD_DIM:
            parts.append(jnp.zeros((LANES - HEAD_DIM - ns, tq), F32))
        bias_t = jnp.concatenate(parts, axis=0).T.astype(BF16)
        qa_ref[g] = q_of(g) + jnp.concatenate([bias_t] * GROUP, axis=0)
        _flash_init(m_ref.at[g], acc_ref.at[g])

    slc_refs = [(m_ref.at[g, :, lanes_of(h)], acc_ref.at[g, :, lanes_of(h)]) for g, h in chains]

    s_bufs = (s0_ref, s1_ref)

    def slc_store(kj, buf, masked, c):
        g, h = chains[c]
        k0 = pl.multiple_of(kj * tk, tk)
        s = _nt(ks_ref[0, g, pl.ds(k0, tk), :], qa_ref[g, lanes_of(h), :])
        if masked:
            s = jnp.where(causal, s, NEG_INF)
        s_bufs[buf][g, :, lanes_of(h)] = s

    def slc_softmax(kj, buf, c):
        g, h = chains[c]
        return _chain_softmax([s_bufs[buf][g, :, lanes_of(h)]], slc_refs[c][0])

    def slc_values(kj, staged, c):
        _chain_values(staged, [vsT_ref[0, chains[c][0], kj]], slc_refs[c][1])

    _pipelined_sweep(qt, len(chains), slc_store, slc_softmax, slc_values)

    for g in range(G):
        def gate_row(j):
            return jnp.concatenate([g_ref[0, g, 3 * r + j:3 * r + j + 1, :] for r in range(GROUP)], axis=1)

        oT = gate_row(0) * oc_ref[g] + gate_row(1) * _normalized(acc_ref[g]) + gate_row(2) * ow_ref[g]
        for pr in range(GROUP // 2):
            st = jnp.concatenate([oT[:, (2 * pr) * tq:(2 * pr + 1) * tq], oT[:, (2 * pr + 1) * tq:(2 * pr + 2) * tq]], axis=0)
            col = (g * GROUP // 2 + pr) * LANES
            o_ref[0, :, col:col + LANES] = st.T.astype(BF16)


def _nsa(qn, kc, vcT, ovT, ks, vsT, kw, vwT, gT4, *, tq):
    B, H, T, _ = qn.shape
    G = N_KV_NSA
    tk = vsT.shape[-1]
    ncp = kc.shape[2]
    ns = ovT.shape[0]
    assert tq == tk and WINDOW == 2 * tk and T >= 2 * tk and ns <= LANES - HEAD_DIM
    nq = GROUP * tq
    kvspec = pl.BlockSpec((1, G, T, LANES), lambda b, i: (b, 0, 0, 0))
    vtspec = pl.BlockSpec((1, G, T // tk, V_ROWS, tk), lambda b, i: (b, 0, 0, 0, 0))
    return pl.pallas_call(
        functools.partial(_nsa_kernel, tq=tq, tk=tk),
        grid=(B, T // tq),
        in_specs=[
            pl.BlockSpec((1, H, tq, LANES), lambda b, i: (b, 0, i, 0)),
            pl.BlockSpec((1, G, ncp, LANES), lambda b, i: (b, 0, 0, 0)),
            pl.BlockSpec((1, G, HEAD_DIM, ncp), lambda b, i: (b, 0, 0, 0)),
            pl.BlockSpec((ns, ncp), lambda b, i: (0, 0)),
            kvspec, vtspec, kvspec, vtspec,
            pl.BlockSpec((1, G, 3 * GROUP, tq), lambda b, i: (b, 0, 0, i)),
        ],
        out_specs=pl.BlockSpec((1, tq, H * HEAD_DIM), lambda b, i: (b, i, 0)),
        out_shape=jax.ShapeDtypeStruct((B, T, H * HEAD_DIM), BF16),
        scratch_shapes=[
            pltpu.VMEM((G, 1, nq), F32),
            pltpu.VMEM((G, V_ROWS, nq), F32),
            pltpu.VMEM((G, ns, tq), F32),
            pltpu.VMEM((G, nq, LANES), BF16),
            pltpu.VMEM((G, HEAD_DIM, nq), F32),
            pltpu.VMEM((G, HEAD_DIM, nq), F32),
            pltpu.VMEM((G, tk, nq), F32),
            pltpu.VMEM((G, tk, nq), F32),
        ],
        compiler_params=_params(("arbitrary", "arbitrary")),
        name="nsa",
    )(qn, kc, vcT, ovT, ks, vsT, kw, vwT, gT4)


def _mixout_kernel(x_ref, mod_ref, g_ref, on_ref, of_ref, wgm_ref, wun_ref, wuf_ref, wo_ref, o_ref):
    x = x_ref[0]
    D = x.shape[-1]
    sh = mod_ref[0, 3:4, :]
    sc = mod_ref[0, 4:5, :]
    ga = mod_ref[0, 5:6, :]
    h = _rms_mod(x, g_ref[...], sh, sc).astype(BF16)
    gm = _mm(h, wgm_ref[...])
    un = _mm(on_ref[0], wun_ref[...])
    uf = _mm(of_ref[0], wuf_ref[...])
    merged = jax.nn.sigmoid(gm[:, :D]) * un + jax.nn.sigmoid(gm[:, D:]) * uf
    y = _mm(merged.astype(BF16), wo_ref[...])
    o_ref[0] = x + ga * y


def _mixout(x, mod3, g, o_nsa, o_fox, wgm, wun, wuf, wo, *, tm):
    B, T, D = x.shape
    bt = lambda b, t: (b, t, 0)
    c2 = lambda b, t: (0, 0)
    dn = o_nsa.shape[-1]
    return pl.pallas_call(
        _mixout_kernel,
        grid=(B, T // tm),
        in_specs=[
            pl.BlockSpec((1, tm, D), bt),
            pl.BlockSpec((1, N_MOD, D), lambda b, t: (b, 0, 0)),
            pl.BlockSpec((1, D), c2),
            pl.BlockSpec((1, tm, dn), bt),
            pl.BlockSpec((1, tm, dn), bt),
            pl.BlockSpec((D, 2 * D), c2, pipeline_mode=pl.Buffered(1)),
            pl.BlockSpec((dn, D), c2, pipeline_mode=pl.Buffered(1)),
            pl.BlockSpec((dn, D), c2, pipeline_mode=pl.Buffered(1)),
            pl.BlockSpec((D, D), c2, pipeline_mode=pl.Buffered(1)),
        ],
        out_specs=pl.BlockSpec((1, tm, D), bt),
        out_shape=jax.ShapeDtypeStruct((B, T, D), F32),
        compiler_params=_params(("arbitrary", "arbitrary")),
        name="mixout",
    )(x, mod3, g.reshape(1, D), o_nsa, o_fox, wgm, wun, wuf, wo)


def _chunk_ffn_weights(w_gate, w_up, w_down, ck):
    D, F = w_gate.shape
    n = F // ck
    wg = w_gate.astype(BF16).reshape(D, n, ck).transpose(1, 0, 2)
    wu = w_up.astype(BF16).reshape(D, n, ck).transpose(1, 0, 2)
    wd = w_down.astype(BF16).reshape(n, ck, D)
    return wg, wu, wd


def _compress_weights(pe, w1, w2):
    half = CMP_BLOCK // 2
    eye = jnp.eye(N_KV_NSA, dtype=F32)

    def expand(w_half):
        w3 = w_half.reshape(half, HEAD_DIM, CMP_HIDDEN)
        return jnp.einsum('jdn,gh->jgdhn', w3, eye).reshape(half * N_KV_NSA * HEAD_DIM, N_KV_NSA * CMP_HIDDEN).astype(BF16)

    def pe_row(pe_half):
        return jnp.broadcast_to(pe_half[:, None, :], (half, N_KV_NSA, HEAD_DIM)).reshape(1, -1)

    w2p = jnp.pad(w2, ((0, 0), (0, LANES - HEAD_DIM))).astype(BF16)
    return (pe_row(pe[:half]), pe_row(pe[half:]), expand(w1[:half * HEAD_DIM]), expand(w1[half * HEAD_DIM:]), w2p)


def kernel(x, c, positions, w_ada, b_ada, g_ffn1, w_gate1, w_up1, w_down1, g_mix, w_in, b_forget, pe_ck, w1_ck, w2_ck, pe_cv, w1_cv, w2_cv, w_up_nsa, w_up_fox, w_o, g_ffn2, w_gate2, w_up2, w_down2, g_final):
    B, T, D = x.shape
    depth = w_ada.shape[0]
    tm = 512
    ffn_ck = 256
    nsa_tq, nsa_tk = 256, 256
    fox_tq, fox_tk = 512, 256
    n_slc = T // SLC_BLOCK
    n_sub = T // CMP_STRIDE

    half = ROPE_DIM // 2
    inv_freq = ROPE_THETA ** (-jnp.arange(half, dtype=F32) / half)
    d_in_head = jnp.arange(LANES) % HEAD_DIM
    invf = jnp.where(d_in_head < ROPE_DIM, inv_freq[d_in_head % half], 0.0).astype(F32).reshape(1, LANES)
    sgn = jnp.where(d_in_head < half, -1.0, 1.0).astype(F32).reshape(1, LANES)
    cmp_start = np.arange(n_sub) * CMP_STRIDE
    slc_start = np.arange(n_slc) * SLC_BLOCK
    ov = ((cmp_start[:, None] < slc_start[None, :] + SLC_BLOCK) & (slc_start[None, :] < cmp_start[:, None] + CMP_BLOCK))
    ov[n_sub - CMP_BLOCK // CMP_STRIDE + 1:, :] = False
    ovT = jnp.asarray(ov.T, dtype=BF16)

    pos_f = positions.astype(F32)[..., None]
    c_in = c
    for l in range(depth):
        mod3 = _ada(c_in, w_ada[l], b_ada[l]).reshape(B, N_MOD, D)
        wg1, wu1, wd1 = _chunk_ffn_weights(w_gate1[l], w_up1[l], w_down1[l], ffn_ck)
        wg2, wu2, wd2 = _chunk_ffn_weights(w_gate2[l], w_up2[l], w_down2[l], ffn_ck)
        wl = w_in[l]
        small = jnp.concatenate([wl[:, 1280:1304], wl[:, 2840:2848], jnp.zeros((D, LANES - 32), F32)], axis=1)
        w_proj = jnp.concatenate([wl[:, :1280], wl[:, 1304:2840], small], axis=1).astype(BF16)
        w_gm = wl[:, 2848:].astype(BF16)
        bf_row = jnp.zeros((1, LANES), F32).at[0, _GATE_COLS:_GATE_COLS + N_HEADS_FOX].set(b_forget[l])

        x = _ffn(x, mod3, g_ffn1[l], wg1, wu1, wd1, g_final, k_mod=0, final=False, tm=tm)

        (qn, kcr, vcr, ks, vsT, kw, vwT, qf, kf, vfT, gT) = _inproj(
            x, mod3, g_mix[l], pos_f, invf, sgn, bf_row, w_proj, tm=tm, nsa_tk=nsa_tk, fox_tk=fox_tk)

        pk_t, pk_b, wk_t, wk_b, w2k = _compress_weights(pe_ck[l], w1_ck[l], w2_ck[l])
        pv_t, pv_b, wv_t, wv_b, w2v = _compress_weights(pe_cv[l], w1_cv[l], w2_cv[l])
        pe4 = jnp.concatenate([pk_t, pk_b, pv_t, pv_b], axis=0)
        zc = CMP_STRIDE * LANES
        kc, vcT = _compress(kcr.reshape(B, n_sub, zc), vcr.reshape(B, n_sub, zc), pe4, wk_t, wk_b, wv_t, wv_b, w2k, w2v)

        o_fox = _fox(qf, kf, vfT, tq=fox_tq)
        gT4 = gT.reshape(B, N_KV_NSA, 3 * GROUP, T)
        o_nsa = _nsa(qn, kc, vcT, ovT, ks, vsT, kw, vwT, gT4, tq=nsa_tq)

        x = _mixout(x, mod3, g_mix[l], o_nsa, o_fox, w_gm, w_up_nsa[l].astype(BF16), w_up_fox[l].astype(BF16),
                    w_o[l].astype(BF16), tm=tm)
        last = l == depth - 1
        x = _ffn(x, mod3, g_ffn2[l], wg2, wu2, wd2, g_final, k_mod=6, final=last, tm=tm)
    return x
```

```python
import functools

import numpy as np
import jax
import jax.numpy as jnp
from jax import lax
from jax.experimental import pallas as pl
from jax.experimental.pallas import tpu as pltpu

HEAD_DIM = 64
N_HEADS_NSA = 8
N_KV_NSA = 2
GROUP = N_HEADS_NSA // N_KV_NSA
N_HEADS_FOX = 8
CMP_BLOCK = 32
CMP_STRIDE = 16
CMP_HIDDEN = 128
SLC_BLOCK = 64
TOP_N = 16
WINDOW = 512
ROPE_THETA = 500000.0
ROPE_DIM = HEAD_DIM // 4
N_MOD = 9
RMS_EPS = 1e-6
NEG_INF = -1e30
FORCE_SCORE = 1e4

LANES = 128
LOG2E = 1.4426950408889634
V_ROWS = HEAD_DIM + 16
F32 = jnp.float32
BF16 = jnp.bfloat16
VMEM_LIMIT = 56 * 1024 * 1024

_OFF_QN = 0
_OFF_KV = 512
_OFF_QF = 1280
_OFF_KF = 1792
_OFF_VF = 2304
_OFF_SMALL = 2816
_W_IN_COLS = 2944
_GATE_COLS = 3 * N_HEADS_NSA
_TAIL = HEAD_DIM
_TAIL2 = HEAD_DIM + 36


def _params(sem):
    return pltpu.CompilerParams(dimension_semantics=sem, vmem_limit_bytes=VMEM_LIMIT)


def _nt(a, b):
    return lax.dot_general(a, b, (((1,), (1,)), ((), ())), preferred_element_type=F32)


def _mm(a, b):
    return jnp.dot(a, b, preferred_element_type=F32)


def _split3(x):
    hi = x.astype(BF16)
    r = x - hi.astype(F32)
    mid = r.astype(BF16)
    lo = (r - mid.astype(F32)).astype(BF16)
    return hi, mid, lo


def _rms_mod(x, g, shift, scale):
    ms = jnp.mean(x * x, axis=-1, keepdims=True)
    y = x * lax.rsqrt(ms + RMS_EPS) * g
    return y * (1.0 + scale) + shift


def _ada_kernel(c_ref, w_ref, b_ref, o_ref):
    c = c_ref[...]
    ca = c * jax.nn.sigmoid(c)
    h0, h1, h2 = _split3(ca)
    w0, w1, w2 = _split3(w_ref[...])
    acc = _mm(h0, w0) + _mm(h0, w1) + _mm(h1, w0)
    acc = acc + (_mm(h1, w1) + _mm(h0, w2) + _mm(h2, w0))
    o_ref[...] = acc + b_ref[...]


def _ada(c, w_ada, b_ada):
    B, D = c.shape
    n = w_ada.shape[1]
    tn = 1024
    return pl.pallas_call(
        _ada_kernel,
        grid=(n // tn,),
        in_specs=[
            pl.BlockSpec((B, D), lambda j: (0, 0)),
            pl.BlockSpec((D, tn), lambda j: (0, j)),
            pl.BlockSpec((1, tn), lambda j: (0, j)),
        ],
        out_specs=pl.BlockSpec((B, tn), lambda j: (0, j)),
        out_shape=jax.ShapeDtypeStruct((B, n), F32),
        compiler_params=_params(("arbitrary",)),
        name="ada",
    )(c, w_ada, b_ada.reshape(1, n))


def _ffn_kernel(x_ref, mod_ref, g_ref, wg_ref, wu_ref, wd_ref, gf_ref, o_ref, h_ref, acc_ref, *, k_mod, final, ck):
    x = x_ref[0]
    sh = mod_ref[0, k_mod:k_mod + 1, :]
    sc = mod_ref[0, k_mod + 1:k_mod + 2, :]
    ga = mod_ref[0, k_mod + 2:k_mod + 3, :]
    h_ref[...] = _rms_mod(x, g_ref[...], sh, sc).astype(BF16)
    acc_ref[...] = jnp.zeros_like(acc_ref)

    n_chunks = wg_ref.shape[1] // ck
    h = h_ref[...]

    def gate_up(j):
        return _mm(h, wg_ref[:, j * ck:(j + 1) * ck]), _mm(h, wu_ref[:, j * ck:(j + 1) * ck])

    gu = gate_up(0)
    for j in range(n_chunks):
        g, u = gu
        if j + 1 < n_chunks:
            gu = gate_up(j + 1)
        a = (g * jax.nn.sigmoid(g) * u).astype(BF16)
        acc_ref[...] += _mm(a, wd_ref[j * ck:(j + 1) * ck, :])
    out = x + 0.5 * ga * acc_ref[...]
    if final:
        ms = jnp.mean(out * out, axis=-1, keepdims=True)
        out = out * lax.rsqrt(ms + RMS_EPS) * gf_ref[...]
    o_ref[0] = out


def _ffn(x, mod3, g, wg, wu, wd, g_final, *, k_mod, final, tm, ck):
    B, T, D = x.shape
    F = wg.shape[1]
    assert F % ck == 0
    const2 = lambda b, t: (0, 0)
    return pl.pallas_call(
        functools.partial(_ffn_kernel, k_mod=k_mod, final=final, ck=ck),
        grid=(B, T // tm),
        in_specs=[
            pl.BlockSpec((1, tm, D), lambda b, t: (b, t, 0)),
            pl.BlockSpec((1, N_MOD, D), lambda b, t: (b, 0, 0)),
            pl.BlockSpec((1, D), lambda b, t: (0, 0)),
            pl.BlockSpec((D, F), const2, pipeline_mode=pl.Buffered(1)),
            pl.BlockSpec((D, F), const2, pipeline_mode=pl.Buffered(1)),
            pl.BlockSpec((F, D), const2, pipeline_mode=pl.Buffered(1)),
            pl.BlockSpec((1, D), lambda b, t: (0, 0)),
        ],
        out_specs=pl.BlockSpec((1, tm, D), lambda b, t: (b, t, 0)),
        out_shape=jax.ShapeDtypeStruct((B, T, D), F32),
        scratch_shapes=[pltpu.VMEM((tm, D), BF16), pltpu.VMEM((tm, D), F32)],
        compiler_params=_params(("arbitrary", "arbitrary")),
        name="ffn_final" if final else "ffn",
    )(x, mod3, g.reshape(1, D), wg, wu, wd, g_final.reshape(1, D))


def _inproj_kernel(x_ref, mod_ref, g_ref, pos_ref, invf_ref, bf_ref, sel_ref, oneq_ref, onek_ref, tri_ref, w_ref,
                   qn_ref, kcr_ref, vcr_ref, ks_ref, vsT_ref, kw_ref, vwT_ref, qf_ref, kf_ref, vfT_ref, gT_ref,
                   h_ref, carry_ref, *, tm):
    t_idx = pl.program_id(1)
    sh = mod_ref[0, 3:4, :]
    sc = mod_ref[0, 4:5, :]
    h_ref[...] = _rms_mod(x_ref[0], g_ref[...], sh, sc).astype(BF16)
    h = h_ref[...]

    lane = lax.broadcasted_iota(jnp.int32, (tm, LANES), 1)
    low = lane < HEAD_DIM
    first8 = (lane & (HEAD_DIM - 1)) < (ROPE_DIM // 2)
    ang = invf_ref[...] * pos_ref[0]
    cos8 = jnp.cos(ang)
    sin8 = jnp.sin(ang)
    rest = HEAD_DIM - ROPE_DIM
    cos_t = jnp.concatenate([cos8, cos8, jnp.ones((rest, tm), F32)] * 2, axis=0).T
    sin_t = jnp.concatenate([-sin8, sin8, jnp.zeros((rest, tm), F32)] * 2, axis=0).T

    def rope(xs):
        partner = jnp.where(first8, pltpu.roll(xs, LANES - ROPE_DIM // 2, 1), pltpu.roll(xs, ROPE_DIM // 2, 1))
        return xs * cos_t + partner * sin_t

    def split_heads(xs):
        return jnp.where(low, xs, 0.0), jnp.where(low, pltpu.roll(xs, HEAD_DIM, 1), 0.0)

    def with_ones(vt):
        extra = (lax.broadcasted_iota(jnp.int32, (V_ROWS - HEAD_DIM, vt.shape[1]), 0) == 0).astype(BF16)
        return jnp.concatenate([vt, extra], axis=0)

    sm = _mm(h, w_ref[:, _OFF_SMALL:_OFF_SMALL + LANES])
    gT_ref[0] = jax.nn.sigmoid(sm).T[:_GATE_COLS, :]
    xl = sm + bf_ref[...]
    logf = jnp.minimum(xl, 0.0) - jnp.log1p(jnp.exp(-jnp.abs(xl)))
    tri = tri_ref[...]
    l0, l1, l2 = _split3(logf)

    @pl.when(t_idx == 0)
    def _():
        carry_ref[...] = jnp.zeros_like(carry_ref)

    cf = _mm(tri, l0) + _mm(tri, l1) + _mm(tri, l2) + carry_ref[...]
    carry_ref[...] = cf[tm - 1:tm, :]
    pieces = _split3(cf * LOG2E)
    placed = _mm(pieces[0], sel_ref[0]) + _mm(pieces[1], sel_ref[1]) + _mm(pieces[2], sel_ref[2])
    q_tail = placed[:, :LANES]
    k_tail = placed[:, LANES:]

    qn = _mm(h, w_ref[:, _OFF_QN:_OFF_QN + 512])
    scale = HEAD_DIM ** -0.5 * LOG2E
    for j in range(4):
        a, b = split_heads(rope(qn[:, j * LANES:(j + 1) * LANES]) * scale)
        qn_ref[0, 2 * j] = a.astype(BF16)
        qn_ref[0, 2 * j + 1] = b.astype(BF16)

    nsa_tk = vsT_ref.shape[-1]
    kv = _mm(h, w_ref[:, _OFF_KV:_OFF_KV + 768])
    kcr_ref[0] = rope(kv[:, 0:LANES])
    vcr_ref[0] = kv[:, LANES:2 * LANES]
    blk = lax.shift_right_logical(t_idx * tm + lax.broadcasted_iota(jnp.int32, (tm, LANES), 0), SLC_BLOCK.bit_length() - 1)
    onehot = (lane == blk + HEAD_DIM).astype(F32)
    for k_out, v_out, off, tail in ((ks_ref, vsT_ref, 2 * LANES, onehot), (kw_ref, vwT_ref, 4 * LANES, None)):
        a, b = split_heads(rope(kv[:, off:off + LANES]))
        if tail is not None:
            a, b = a + tail, b + tail
        k_out[0, 0] = a.astype(BF16)
        k_out[0, 1] = b.astype(BF16)
        vT = kv[:, off + LANES:off + 2 * LANES].T.astype(BF16)
        for gi in range(N_KV_NSA):
            for c in range(tm // nsa_tk):
                v_out[0, gi, c] = with_ones(vT[gi * HEAD_DIM:(gi + 1) * HEAD_DIM, c * nsa_tk:(c + 1) * nsa_tk])

    qf = _mm(h, w_ref[:, _OFF_QF:_OFF_QF + 512])
    kf = _mm(h, w_ref[:, _OFF_KF:_OFF_KF + 512])
    for j in range(4):
        qa, qb = split_heads(qf[:, j * LANES:(j + 1) * LANES] * scale)
        ka, kb = split_heads(kf[:, j * LANES:(j + 1) * LANES])
        for hd, qh, kh in ((2 * j, qa, ka), (2 * j + 1, qb, kb)):
            qf_ref[0, hd] = (qh + q_tail + oneq_ref[hd:hd + 1, :]).astype(BF16)
            kf_ref[0, hd] = (kh + k_tail + onek_ref[hd:hd + 1, :]).astype(BF16)
    vf = _mm(h, w_ref[:, _OFF_VF:_OFF_VF + 512])
    fox_tk = vfT_ref.shape[-1]
    for j in range(4):
        vT = vf[:, j * LANES:(j + 1) * LANES].T.astype(BF16)
        for hh in range(2):
            for c in range(tm // fox_tk):
                vfT_ref[0, 2 * j + hh, c] = with_ones(vT[hh * HEAD_DIM:(hh + 1) * HEAD_DIM, c * fox_tk:(c + 1) * fox_tk])


def _inproj(x, mod3, g, pos_row, invf, bf_row, sel, oneq, onek, tri, w, *, tm, nsa_tk, fox_tk):
    B, T, D = x.shape
    H, G = N_HEADS_NSA, N_KV_NSA
    bt = lambda b, t: (b, t, 0)
    hb = lambda b, t: (b, 0, t, 0)
    vb = lambda b, t: (b, 0, t, 0, 0)
    out_shape = [
        jax.ShapeDtypeStruct((B, H, T, LANES), BF16),
        jax.ShapeDtypeStruct((B, T, LANES), F32),
        jax.ShapeDtypeStruct((B, T, LANES), F32),
        jax.ShapeDtypeStruct((B, G, T, LANES), BF16),
        jax.ShapeDtypeStruct((B, G, T // nsa_tk, V_ROWS, nsa_tk), BF16),
        jax.ShapeDtypeStruct((B, G, T, LANES), BF16),
        jax.ShapeDtypeStruct((B, G, T // nsa_tk, V_ROWS, nsa_tk), BF16),
        jax.ShapeDtypeStruct((B, N_HEADS_FOX, T, LANES), BF16),
        jax.ShapeDtypeStruct((B, N_HEADS_FOX, T, LANES), BF16),
        jax.ShapeDtypeStruct((B, N_HEADS_FOX, T // fox_tk, V_ROWS, fox_tk), BF16),
        jax.ShapeDtypeStruct((B, _GATE_COLS, T), F32),
    ]
    out_specs = [
        pl.BlockSpec((1, H, tm, LANES), hb),
        pl.BlockSpec((1, tm, LANES), bt),
        pl.BlockSpec((1, tm, LANES), bt),
        pl.BlockSpec((1, G, tm, LANES), hb),
        pl.BlockSpec((1, G, tm // nsa_tk, V_ROWS, nsa_tk), vb),
        pl.BlockSpec((1, G, tm, LANES), hb),
        pl.BlockSpec((1, G, tm // nsa_tk, V_ROWS, nsa_tk), vb),
        pl.BlockSpec((1, N_HEADS_FOX, tm, LANES), hb),
        pl.BlockSpec((1, N_HEADS_FOX, tm, LANES), hb),
        pl.BlockSpec((1, N_HEADS_FOX, tm // fox_tk, V_ROWS, fox_tk), vb),
        pl.BlockSpec((1, _GATE_COLS, tm), lambda b, t: (b, 0, t)),
    ]
    row = lambda b, t: (0, 0)
    return pl.pallas_call(
        functools.partial(_inproj_kernel, tm=tm),
        grid=(B, T // tm),
        in_specs=[
            pl.BlockSpec((1, tm, D), bt),
            pl.BlockSpec((1, N_MOD, D), lambda b, t: (b, 0, 0)),
            pl.BlockSpec((1, D), row),
            pl.BlockSpec((1, 1, tm), lambda b, t: (b, 0, t)),
            pl.BlockSpec((ROPE_DIM // 2, tm), row),
            pl.BlockSpec((1, LANES), row),
            pl.BlockSpec((3, LANES, 2 * LANES), lambda b, t: (0, 0, 0)),
            pl.BlockSpec((N_HEADS_FOX, LANES), row),
            pl.BlockSpec((N_HEADS_FOX, LANES), row),
            pl.BlockSpec((tm, tm), row),
            pl.BlockSpec((D, _W_IN_COLS), row, pipeline_mode=pl.Buffered(1)),
        ],
        out_specs=out_specs,
        out_shape=out_shape,
        scratch_shapes=[pltpu.VMEM((tm, D), BF16), pltpu.VMEM((1, LANES), F32)],
        compiler_params=_params(("arbitrary", "arbitrary")),
        name="inproj",
    )(x, mod3, g.reshape(1, D), pos_row, invf, bf_row, sel, oneq, onek, tri, w)


def _gelu_tanh(x):
    c = np.float32(np.sqrt(2.0 / np.pi))
    return x * (0.5 * (1.0 + jnp.tanh(c * (x + 0.044715 * (x * x * x)))))


def _compress_kernel(zk_ref, zv_ref, pe_ref, wkt_ref, wkb_ref, wvt_ref, wvb_ref, w2k_ref, w2v_ref, ovT_ref, kc_ref, lhs_ref):
    nsub = zk_ref.shape[1] // CMP_STRIDE

    def mlp(z_ref, pe_top, pe_bot, wt_ref, wb_ref):
        a = b = None
        for j in range(CMP_STRIDE):
            xj = z_ref[0, pl.ds(j, nsub, stride=CMP_STRIDE), :]
            lanes = slice(j * LANES, (j + 1) * LANES)
            aj = _mm((xj + pe_ref[pe_top:pe_top + 1, lanes]).astype(BF16), wt_ref[lanes, :])
            bj = _mm((xj + pe_ref[pe_bot:pe_bot + 1, lanes]).astype(BF16), wb_ref[lanes, :])
            a = aj if a is None else a + aj
            b = bj if b is None else b + bj
        return _gelu_tanh(a + pltpu.roll(b, nsub - 1, 0))

    hk = mlp(zk_ref, 0, 1, wkt_ref, wkb_ref)
    hv = mlp(zv_ref, 2, 3, wvt_ref, wvb_ref)
    for gi in range(N_KV_NSA):
        kc_ref[0, gi] = _mm(hk[:, gi * CMP_HIDDEN:(gi + 1) * CMP_HIDDEN].astype(BF16), w2k_ref[...]).astype(BF16)
        vc = _mm(hv[:, gi * CMP_HIDDEN:(gi + 1) * CMP_HIDDEN].astype(BF16), w2v_ref[...])
        ones = (lax.broadcasted_iota(jnp.int32, (V_ROWS - HEAD_DIM, nsub), 0) == 0).astype(BF16)
        lhs_ref[0, gi] = jnp.concatenate([vc.T[:HEAD_DIM, :].astype(BF16), ones, ovT_ref[...]], axis=0)


def _compress(zk, zv, pe4, wkt, wkb, wvt, wvb, w2k, w2v, ovT):
    B, T, _ = zk.shape
    nsub = T // CMP_STRIDE
    zc = CMP_STRIDE * LANES
    G = N_KV_NSA
    c2 = lambda b: (0, 0)
    zspec = pl.BlockSpec((1, T, LANES), lambda b: (b, 0, 0))
    wspec = pl.BlockSpec((zc, G * CMP_HIDDEN), c2)
    w2spec = pl.BlockSpec((CMP_HIDDEN, LANES), c2)
    return pl.pallas_call(
        _compress_kernel,
        grid=(B,),
        in_specs=[zspec, zspec, pl.BlockSpec((4, zc), c2), wspec, wspec, wspec, wspec, w2spec, w2spec,
                  pl.BlockSpec(ovT.shape, c2)],
        out_specs=[
            pl.BlockSpec((1, G, nsub, LANES), lambda b: (b, 0, 0, 0)),
            pl.BlockSpec((1, G, V_ROWS + ovT.shape[0], nsub), lambda b: (b, 0, 0, 0)),
        ],
        out_shape=[
            jax.ShapeDtypeStruct((B, G, nsub, LANES), BF16),
            jax.ShapeDtypeStruct((B, G, V_ROWS + ovT.shape[0], nsub), BF16),
        ],
        compiler_params=_params(("arbitrary",)),
        name="compress",
    )(zk, zv, pe4, wkt, wkb, wvt, wvb, w2k, w2v, ovT)


def _flash_init(m_ref, acc_ref):
    m_ref[...] = jnp.full_like(m_ref, NEG_INF)
    acc_ref[...] = jnp.zeros_like(acc_ref)


def _chain_softmax(s_list, m_ref):
    m_old = m_ref[...]
    m_new = m_old
    for s in s_list:
        m_new = jnp.maximum(m_new, jnp.max(s, axis=0, keepdims=True))
    m_ref[...] = m_new
    return jnp.exp2(m_old - m_new), [jnp.exp2(s - m_new).astype(BF16) for s in s_list]


def _chain_values(staged, v_list, acc_ref):
    alpha, ps = staged
    pv = _mm(v_list[0], ps[0])
    for v, p in zip(v_list[1:], ps[1:]):
        pv = pv + _mm(v, p)
    acc_ref[...] = alpha * acc_ref[...] + pv


def _normalized(acc):
    return acc[:HEAD_DIM] / acc[HEAD_DIM:HEAD_DIM + 1]


def _pipelined_sweep(n, n_chains, store, softmax, values):
    chains = range(n_chains)

    def by_parity(i, fn):
        @pl.when((i & 1) == 0)
        def _():
            fn(0)

        @pl.when((i & 1) == 1)
        def _():
            fn(1)

    def step(nxt, masked, cur, par):
        staged = None
        for c in chains:
            if nxt is not None:
                store(nxt, 1 - par, masked, c)
            new = softmax(cur, par, c)
            if staged is not None:
                values(cur, staged, c - 1)
            staged = new
        values(cur, staged, n_chains - 1)

    @pl.when(n == 0)
    def _():
        for c in chains:
            store(0, 0, True, c)

    @pl.when(n > 0)
    def _():
        for c in chains:
            store(0, 0, False, c)

    def body(i, carry):
        by_parity(i, lambda par: step(i + 1, False, i, par))
        return carry

    lax.fori_loop(0, n - 1, body, 0)

    @pl.when(n > 0)
    def _():
        by_parity(n - 1, lambda par: step(n, True, n - 1, par))

    by_parity(n, lambda par: step(None, False, n, par))


_FOX_HEADS_PER_STEP = 8


def _fox_kernel(q_ref, k_ref, vT_ref, o_ref, m_ref, acc_ref, s0_ref, s1_ref, *, tq, tk):
    qi = pl.program_id(2)
    nkc = tq // tk
    heads = range(_FOX_HEADS_PER_STEP)
    s_bufs = (s0_ref, s1_ref)
    row = lax.broadcasted_iota(jnp.int32, (tk, tq), 0)
    lane = lax.broadcasted_iota(jnp.int32, (tk, tq), 1)
    for hh in heads:
        _flash_init(m_ref.at[hh], acc_ref.at[hh])

    def store(kt, buf, masked, hh):
        for c in range(nkc):
            k0 = pl.multiple_of(kt * tq + c * tk, tk)
            s = _nt(k_ref[0, hh, pl.ds(k0, tk), :], q_ref[0, hh])
            if masked:
                s = jnp.where(row + c * tk <= lane, s, NEG_INF)
            s_bufs[buf][hh, c * tk:(c + 1) * tk, :] = s

    def softmax(kt, buf, hh):
        return _chain_softmax([s_bufs[buf][hh, c * tk:(c + 1) * tk, :] for c in range(nkc)], m_ref.at[hh])

    def values(kt, staged, hh):
        _chain_values(staged, [vT_ref[0, hh, kt * nkc + c] for c in range(nkc)], acc_ref.at[hh])

    _pipelined_sweep(qi, len(heads), store, softmax, values)
    for pr in range(_FOX_HEADS_PER_STEP // 2):
        st = jnp.concatenate([_normalized(acc_ref[2 * pr]), _normalized(acc_ref[2 * pr + 1])], axis=0)
        o_ref[0, :, pr * LANES:(pr + 1) * LANES] = st.T.astype(BF16)


def _fox(qf, kf, vfT, *, tq):
    B, H, T, _ = qf.shape
    tk = vfT.shape[-1]
    nh = _FOX_HEADS_PER_STEP
    return pl.pallas_call(
        functools.partial(_fox_kernel, tq=tq, tk=tk),
        grid=(B, H // nh, T // tq),
        in_specs=[
            pl.BlockSpec((1, nh, tq, LANES), lambda b, p, i: (b, p, i, 0)),
            pl.BlockSpec((1, nh, T, LANES), lambda b, p, i: (b, p, 0, 0)),
            pl.BlockSpec((1, nh, T // tk, V_ROWS, tk), lambda b, p, i: (b, p, 0, 0, 0)),
        ],
        out_specs=pl.BlockSpec((1, tq, nh * HEAD_DIM), lambda b, p, i: (b, i, p)),
        out_shape=jax.ShapeDtypeStruct((B, T, H * HEAD_DIM), BF16),
        scratch_shapes=[
            pltpu.VMEM((nh, 1, tq), F32),
            pltpu.VMEM((nh, V_ROWS, tq), F32),
            pltpu.VMEM((nh, tq, tq), F32),
            pltpu.VMEM((nh, tq, tq), F32),
        ],
        compiler_params=_params(("arbitrary", "arbitrary", "arbitrary")),
        name="fox",
    )(qf, kf, vfT)


def _softmax_pv(s_lists, v_lists):
    staged = []
    for s_list in s_lists:
        m = jnp.max(s_list[0], axis=0, keepdims=True)
        for s in s_list[1:]:
            m = jnp.maximum(m, jnp.max(s, axis=0, keepdims=True))
        staged.append([jnp.exp2(s - m).astype(BF16) for s in s_list])
    outs = []
    for ps, v_list in zip(staged, v_lists):
        pv = _mm(v_list[0], ps[0])
        for v, p in zip(v_list[1:], ps[1:]):
            pv = pv + _mm(v, p)
        outs.append(_normalized(pv))
    return outs


_NSA_LANE_SPLITS = GROUP


def _nsa_kernel(q_ref, kc_ref, cmp_lhs_ref, ks_ref, vsT_ref, kw_ref, vwT_ref, g_ref, o_ref,
                m_ref, acc_ref, imp_ref, cnt_ref, qa_ref, oc_ref, ow_ref, s0_ref, s1_ref, *, tq, tk):
    qt = pl.program_id(1)
    G = N_KV_NSA
    nq = GROUP * tq
    hw = nq // _NSA_LANE_SPLITS
    heads_per_split = GROUP // _NSA_LANE_SPLITS
    ncp = kc_ref.shape[2]
    ns = cmp_lhs_ref.shape[2] - V_ROWS
    blk_shift = SLC_BLOCK.bit_length() - 1
    t_row = qt * tq + (lax.broadcasted_iota(jnp.int32, (1, nq), 1) & (tq - 1))
    ql = lax.broadcasted_iota(jnp.int32, (1, hw), 1) & (tq - 1)
    row_k = lax.broadcasted_iota(jnp.int32, (tk, hw), 0)
    causal = row_k <= ql
    n_io = lax.broadcasted_iota(jnp.int32, (ns, tq), 0)
    cur = lax.shift_right_logical(qt * tq + lax.broadcasted_iota(jnp.int32, (ns, tq), 1), blk_shift)
    forced = (n_io == 0) | (n_io == cur) | (n_io == cur - 1)
    visible = n_io <= cur
    c_io = lax.broadcasted_iota(jnp.int32, (ncp, nq), 0)
    cmp_valid = (c_io * CMP_STRIDE + (CMP_BLOCK - 1)) <= t_row
    chains = [(g, h) for g in range(G) for h in range(_NSA_LANE_SPLITS)]

    def q_of(g):
        return q_ref[0, g * GROUP:(g + 1) * GROUP].reshape(nq, LANES)

    def q_split(g, h):
        h0 = g * GROUP + h * heads_per_split
        return q_ref[0, h0:h0 + heads_per_split].reshape(hw, LANES)

    def lanes_of(h):
        return pl.ds(h * hw, hw)

    s_cmp = [_nt(kc_ref[0, g], q_of(g)) for g in range(G)]
    p_cmp = []
    for g in range(G):
        smk = jnp.where(cmp_valid, s_cmp[g], NEG_INF)
        p_cmp.append(jnp.exp2(smk - jnp.max(smk, axis=0, keepdims=True)).astype(BF16))
    any_valid = t_row >= (CMP_BLOCK - 1)
    for g in range(G):
        r_all = _mm(cmp_lhs_ref[0, g], p_cmp[g])
        inv = jnp.where(any_valid, 1.0 / r_all[HEAD_DIM:HEAD_DIM + 1], 0.0)
        oc_ref[g] = r_all[:HEAD_DIM] * inv
        imp_heads = r_all[V_ROWS:] * inv
        impT = imp_heads[:, 0:tq]
        for r in range(1, GROUP):
            impT = impT + imp_heads[:, r * tq:(r + 1) * tq]
        imp_ref[g] = jnp.where(forced, FORCE_SCORE, jnp.where(visible, impT, -1.0))

    n_back = WINDOW // tk

    @pl.when(qt >= n_back)
    def _():
        base = pl.multiple_of((qt - n_back) * tk, tk)
        s_lists = []
        for g, h in chains:
            q = q_split(g, h)
            tiles = []
            for j in range(n_back + 1):
                sj = _nt(kw_ref[0, g, pl.ds(base + j * tk, tk), :], q)
                if j == 0:
                    sj = jnp.where(row_k > ql, sj, NEG_INF)
                elif j == n_back:
                    sj = jnp.where(causal, sj, NEG_INF)
                tiles.append(sj)
            s_lists.append(tiles)
        v_lists = [[vwT_ref[0, g, qt - n_back + j] for j in range(n_back + 1)] for g, h in chains]
        for (g, h), o in zip(chains, _softmax_pv(s_lists, v_lists)):
            ow_ref[g, :, lanes_of(h)] = o

    @pl.when(qt < n_back)
    def _():
        s_lists = []
        for g, h in chains:
            q = q_split(g, h)
            s_lists.append([jnp.where((row_k + j * tk) <= (qt * tq + ql), _nt(kw_ref[0, g, j * tk:(j + 1) * tk, :], q), NEG_INF)
                            for j in range(n_back)])
        v_lists = [[vwT_ref[0, g, j] for j in range(n_back)] for g, h in chains]
        for (g, h), o in zip(chains, _softmax_pv(s_lists, v_lists)):
            ow_ref[g, :, lanes_of(h)] = o

    n_vis = jnp.minimum((qt + 1) * (tq // SLC_BLOCK), ns)
    sub = 8
    sub_io = lax.broadcasted_iota(jnp.int32, (sub, tq), 0)
    cnt_ref[...] = jnp.zeros_like(cnt_ref)
    for mb in range(ns // sub):
        @pl.when(mb * sub < n_vis)
        def _():
            for g in range(G):
                groups = [imp_ref[g, j * sub:(j + 1) * sub, :] for j in range(ns // sub)]
                counts = [cnt_ref[g, j * sub:(j + 1) * sub, :] for j in range(ns // sub)]
                for mi in range(sub):
                    row = jnp.broadcast_to(groups[mb][mi:mi + 1, :], (sub, tq))
                    for j in range(ns // sub):
                        if j < mb:
                            beats = row > groups[j]
                        elif j > mb:
                            beats = row >= groups[j]
                        else:
                            beats = (row > groups[j]) | ((row >= groups[j]) & (sub_io > mi))
                        counts[j] = counts[j] + beats.astype(F32)
                for j in range(ns // sub):
                    cnt_ref[g, j * sub:(j + 1) * sub, :] = counts[j]

    for g in range(G):
        bias = jnp.where(cnt_ref[g] < float(TOP_N), 0.0, NEG_INF)
        parts = [jnp.zeros((HEAD_DIM, tq), F32), bias]
        if ns < LANES - HEAD_DIM:
            parts.append(jnp.zeros((LANES - HEAD_DIM - ns, tq), F32))
        bias_t = jnp.concatenate(parts, axis=0).T.astype(BF16)
        qa_ref[g] = q_of(g) + jnp.concatenate([bias_t] * GROUP, axis=0)
        _flash_init(m_ref.at[g], acc_ref.at[g])

    slc_refs = [(m_ref.at[g, :, lanes_of(h)], acc_ref.at[g, :, lanes_of(h)]) for g, h in chains]

    s_bufs = (s0_ref, s1_ref)

    def slc_store(kj, buf, masked, c):
        g, h = chains[c]
        k0 = pl.multiple_of(kj * tk, tk)
        s = _nt(ks_ref[0, g, pl.ds(k0, tk), :], qa_ref[g, lanes_of(h), :])
        if masked:
            s = jnp.where(causal, s, NEG_INF)
        s_bufs[buf][g, :, lanes_of(h)] = s

    def slc_softmax(kj, buf, c):
        g, h = chains[c]
        return _chain_softmax([s_bufs[buf][g, :, lanes_of(h)]], slc_refs[c][0])

    def slc_values(kj, staged, c):
        _chain_values(staged, [vsT_ref[0, chains[c][0], kj]], slc_refs[c][1])

    _pipelined_sweep(qt, len(chains), slc_store, slc_softmax, slc_values)

    for g in range(G):
        def gate_row(j):
            return jnp.concatenate([g_ref[0, g, 3 * r + j:3 * r + j + 1, :] for r in range(GROUP)], axis=1)

        oT = gate_row(0) * oc_ref[g] + gate_row(1) * _normalized(acc_ref[g]) + gate_row(2) * ow_ref[g]
        for pr in range(GROUP // 2):
            st = jnp.concatenate([oT[:, (2 * pr) * tq:(2 * pr + 1) * tq], oT[:, (2 * pr + 1) * tq:(2 * pr + 2) * tq]], axis=0)
            col = (g * GROUP // 2 + pr) * LANES
            o_ref[0, :, col:col + LANES] = st.T.astype(BF16)


def _nsa(qn, kc, cmp_lhs, ks, vsT, kw, vwT, gT4, *, tq):
    B, H, T, _ = qn.shape
    G = N_KV_NSA
    tk = vsT.shape[-1]
    ncp = kc.shape[2]
    ns = cmp_lhs.shape[2] - V_ROWS
    assert tq == tk and WINDOW % tk == 0 and T >= WINDOW and ns <= LANES - HEAD_DIM and ns % 8 == 0
    nq = GROUP * tq
    kvspec = pl.BlockSpec((1, G, T, LANES), lambda b, i: (b, 0, 0, 0))
    vtspec = pl.BlockSpec((1, G, T // tk, V_ROWS, tk), lambda b, i: (b, 0, 0, 0, 0))
    return pl.pallas_call(
        functools.partial(_nsa_kernel, tq=tq, tk=tk),
        grid=(B, T // tq),
        in_specs=[
            pl.BlockSpec((1, H, tq, LANES), lambda b, i: (b, 0, i, 0)),
            pl.BlockSpec((1, G, ncp, LANES), lambda b, i: (b, 0, 0, 0)),
            pl.BlockSpec((1, G, V_ROWS + ns, ncp), lambda b, i: (b, 0, 0, 0)),
            kvspec, vtspec, kvspec, vtspec,
            pl.BlockSpec((1, G, 3 * GROUP, tq), lambda b, i: (b, 0, 0, i)),
        ],
        out_specs=pl.BlockSpec((1, tq, H * HEAD_DIM), lambda b, i: (b, i, 0)),
        out_shape=jax.ShapeDtypeStruct((B, T, H * HEAD_DIM), BF16),
        scratch_shapes=[
            pltpu.VMEM((G, 1, nq), F32),
            pltpu.VMEM((G, V_ROWS, nq), F32),
            pltpu.VMEM((G, ns, tq), F32),
            pltpu.VMEM((G, ns, tq), F32),
            pltpu.VMEM((G, nq, LANES), BF16),
            pltpu.VMEM((G, HEAD_DIM, nq), F32),
            pltpu.VMEM((G, HEAD_DIM, nq), F32),
            pltpu.VMEM((G, tk, nq), F32),
            pltpu.VMEM((G, tk, nq), F32),
        ],
        compiler_params=_params(("arbitrary", "arbitrary")),
        name="nsa",
    )(qn, kc, cmp_lhs, ks, vsT, kw, vwT, gT4)


def _mixout_kernel(x_ref, mod_ref, g_ref, on_ref, of_ref, wgm_ref, wun_ref, wuf_ref, wo_ref, o_ref):
    x = x_ref[0]
    D = x.shape[-1]
    sh = mod_ref[0, 3:4, :]
    sc = mod_ref[0, 4:5, :]
    ga = mod_ref[0, 5:6, :]
    h = _rms_mod(x, g_ref[...], sh, sc).astype(BF16)
    gm = _mm(h, wgm_ref[...])
    un = _mm(on_ref[0], wun_ref[...])
    uf = _mm(of_ref[0], wuf_ref[...])
    merged = jax.nn.sigmoid(gm[:, :D]) * un + jax.nn.sigmoid(gm[:, D:]) * uf
    y = _mm(merged.astype(BF16), wo_ref[...])
    o_ref[0] = x + ga * y


def _mixout(x, mod3, g, o_nsa, o_fox, wgm, wun, wuf, wo, *, tm):
    B, T, D = x.shape
    bt = lambda b, t: (b, t, 0)
    c2 = lambda b, t: (0, 0)
    dn = o_nsa.shape[-1]
    return pl.pallas_call(
        _mixout_kernel,
        grid=(B, T // tm),
        in_specs=[
            pl.BlockSpec((1, tm, D), bt),
            pl.BlockSpec((1, N_MOD, D), lambda b, t: (b, 0, 0)),
            pl.BlockSpec((1, D), c2),
            pl.BlockSpec((1, tm, dn), bt),
            pl.BlockSpec((1, tm, dn), bt),
            pl.BlockSpec((D, 2 * D), c2, pipeline_mode=pl.Buffered(1)),
            pl.BlockSpec((dn, D), c2, pipeline_mode=pl.Buffered(1)),
            pl.BlockSpec((dn, D), c2, pipeline_mode=pl.Buffered(1)),
            pl.BlockSpec((D, D), c2, pipeline_mode=pl.Buffered(1)),
        ],
        out_specs=pl.BlockSpec((1, tm, D), bt),
        out_shape=jax.ShapeDtypeStruct((B, T, D), F32),
        compiler_params=_params(("arbitrary", "arbitrary")),
        name="mixout",
    )(x, mod3, g.reshape(1, D), o_nsa, o_fox, wgm, wun, wuf, wo)


def _compress_weights(pe, w1, w2):
    half = CMP_BLOCK // 2
    eye = jnp.eye(N_KV_NSA, dtype=F32)

    def expand(w_half):
        w3 = w_half.reshape(half, HEAD_DIM, CMP_HIDDEN)
        return jnp.einsum('jdn,gh->jgdhn', w3, eye).reshape(half * N_KV_NSA * HEAD_DIM, N_KV_NSA * CMP_HIDDEN).astype(BF16)

    def pe_row(pe_half):
        return jnp.broadcast_to(pe_half[:, None, :], (half, N_KV_NSA, HEAD_DIM)).reshape(1, -1)

    w2p = jnp.pad(w2, ((0, 0), (0, LANES - HEAD_DIM))).astype(BF16)
    return (pe_row(pe[:half]), pe_row(pe[half:]), expand(w1[:half * HEAD_DIM]), expand(w1[half * HEAD_DIM:]), w2p)


def kernel(x, c, positions, w_ada, b_ada, g_ffn1, w_gate1, w_up1, w_down1, g_mix, w_in, b_forget, pe_ck, w1_ck, w2_ck, pe_cv, w1_cv, w2_cv, w_up_nsa, w_up_fox, w_o, g_ffn2, w_gate2, w_up2, w_down2, g_final):
    B, T, D = x.shape
    depth = w_ada.shape[0]
    tm = 512
    ffn_ck = 256
    nsa_tq, nsa_tk = 512, 512
    fox_tq, fox_tk = 512, 512
    n_slc = T // SLC_BLOCK
    n_sub = T // CMP_STRIDE

    half = ROPE_DIM // 2
    inv_freq = ROPE_THETA ** (-jnp.arange(half, dtype=F32) / half)
    invf = jnp.broadcast_to(inv_freq[:, None], (half, tm))
    cmp_start = np.arange(n_sub) * CMP_STRIDE
    slc_start = np.arange(n_slc) * SLC_BLOCK
    ov = ((cmp_start[:, None] < slc_start[None, :] + SLC_BLOCK) & (slc_start[None, :] < cmp_start[:, None] + CMP_BLOCK))
    ov[n_sub - CMP_BLOCK // CMP_STRIDE + 1:, :] = False
    ovT = jnp.asarray(ov.T, dtype=BF16)

    pos_row = positions.astype(F32)[:, None, :]
    sel_np = np.zeros((3, LANES, 2 * LANES), np.float32)
    oneq_np = np.zeros((N_HEADS_FOX, LANES), np.float32)
    onek_np = np.zeros((N_HEADS_FOX, LANES), np.float32)
    for hd in range(N_HEADS_FOX):
        for j in range(3):
            sel_np[j, _GATE_COLS + hd, _TAIL + 3 * hd + j] = 1.0
            sel_np[j, _GATE_COLS + hd, LANES + _TAIL2 + 3 * hd + j] = -1.0
            oneq_np[hd, _TAIL2 + 3 * hd + j] = 1.0
            onek_np[hd, _TAIL + 3 * hd + j] = 1.0
    sel, oneq, onek = jnp.asarray(sel_np, BF16), jnp.asarray(oneq_np), jnp.asarray(onek_np)
    tri = jnp.asarray(np.tril(np.ones((tm, tm), np.float32)), BF16)
    c_in = c
    for l in range(depth):
        mod3 = _ada(c_in, w_ada[l], b_ada[l]).reshape(B, N_MOD, D)
        wg1, wu1, wd1 = w_gate1[l].astype(BF16), w_up1[l].astype(BF16), w_down1[l].astype(BF16)
        wg2, wu2, wd2 = w_gate2[l].astype(BF16), w_up2[l].astype(BF16), w_down2[l].astype(BF16)
        wl = w_in[l]
        small = jnp.concatenate([wl[:, 1280:1304], wl[:, 2840:2848], jnp.zeros((D, LANES - 32), F32)], axis=1)
        w_proj = jnp.concatenate([wl[:, :1280], wl[:, 1304:2840], small], axis=1).astype(BF16)
        w_gm = wl[:, 2848:].astype(BF16)
        bf_row = jnp.zeros((1, LANES), F32).at[0, _GATE_COLS:_GATE_COLS + N_HEADS_FOX].set(b_forget[l])

        x = _ffn(x, mod3, g_ffn1[l], wg1, wu1, wd1, g_final, k_mod=0, final=False, tm=tm, ck=ffn_ck)

        (qn, kcr, vcr, ks, vsT, kw, vwT, qf, kf, vfT, gT) = _inproj(
            x, mod3, g_mix[l], pos_row, invf, bf_row, sel, oneq, onek, tri, w_proj, tm=tm, nsa_tk=nsa_tk, fox_tk=fox_tk)

        pk_t, pk_b, wk_t, wk_b, w2k = _compress_weights(pe_ck[l], w1_ck[l], w2_ck[l])
        pv_t, pv_b, wv_t, wv_b, w2v = _compress_weights(pe_cv[l], w1_cv[l], w2_cv[l])
        pe4 = jnp.concatenate([pk_t, pk_b, pv_t, pv_b], axis=0)
        kc, cmp_lhs = _compress(kcr, vcr, pe4, wk_t, wk_b, wv_t, wv_b, w2k, w2v, ovT)

        o_fox = _fox(qf, kf, vfT, tq=fox_tq)
        gT4 = gT.reshape(B, N_KV_NSA, 3 * GROUP, T)
        o_nsa = _nsa(qn, kc, cmp_lhs, ks, vsT, kw, vwT, gT4, tq=nsa_tq)

        x = _mixout(x, mod3, g_mix[l], o_nsa, o_fox, w_gm, w_up_nsa[l].astype(BF16), w_up_fox[l].astype(BF16),
                    w_o[l].astype(BF16), tm=tm)
        last = l == depth - 1
        x = _ffn(x, mod3, g_ffn2[l], wg2, wu2, wd2, g_final, k_mod=6, final=last, tm=tm, ck=ffn_ck)
    return x
```

```python
import functools

import numpy as np
import jax
import jax.numpy as jnp
from jax import lax
from jax.experimental import pallas as pl
from jax.experimental.pallas import tpu as pltpu

HEAD_DIM = 64
N_HEADS_NSA = 8
N_KV_NSA = 2
GROUP = N_HEADS_NSA // N_KV_NSA
N_HEADS_FOX = 8
CMP_BLOCK = 32
CMP_STRIDE = 16
CMP_HIDDEN = 128
SLC_BLOCK = 64
TOP_N = 16
WINDOW = 512
ROPE_THETA = 500000.0
ROPE_DIM = HEAD_DIM // 4
N_MOD = 9
RMS_EPS = 1e-6
NEG_INF = -1e30
FORCE_SCORE = 1e4

LANES = 128
LOG2E = 1.4426950408889634
V_ROWS = HEAD_DIM + 16
F32 = jnp.float32
BF16 = jnp.bfloat16
VMEM_LIMIT = 56 * 1024 * 1024

_OFF_QN = 0
_OFF_KV = 512
_OFF_QF = 1280
_OFF_KF = 1792
_OFF_VF = 2304
_OFF_SMALL = 2816
_W_IN_COLS = 2944
_GATE_COLS = 3 * N_HEADS_NSA
_TAIL = HEAD_DIM
_TAIL2 = HEAD_DIM + 36


def _params(sem):
    return pltpu.CompilerParams(dimension_semantics=sem, vmem_limit_bytes=VMEM_LIMIT)


def _nt(a, b):
    return lax.dot_general(a, b, (((1,), (1,)), ((), ())), preferred_element_type=F32)


def _mm(a, b):
    return jnp.dot(a, b, preferred_element_type=F32)


def _split3(x):
    hi = x.astype(BF16)
    r = x - hi.astype(F32)
    mid = r.astype(BF16)
    lo = (r - mid.astype(F32)).astype(BF16)
    return hi, mid, lo


def _rms_mod(x, g, shift, scale):
    ms = jnp.mean(x * x, axis=-1, keepdims=True)
    y = x * lax.rsqrt(ms + RMS_EPS) * g
    return y * (1.0 + scale) + shift


def _ada_kernel(c_ref, w_ref, b_ref, o_ref):
    c = c_ref[...]
    ca = c * jax.nn.sigmoid(c)
    h0, h1, h2 = _split3(ca)
    w0, w1, w2 = _split3(w_ref[...])
    acc = _mm(h0, w0) + _mm(h0, w1) + _mm(h1, w0)
    acc = acc + (_mm(h1, w1) + _mm(h0, w2) + _mm(h2, w0))
    o_ref[...] = acc + b_ref[...]


def _ada(c, w_ada, b_ada):
    B, D = c.shape
    n = w_ada.shape[1]
    tn = 1024
    return pl.pallas_call(
        _ada_kernel,
        grid=(n // tn,),
        in_specs=[
            pl.BlockSpec((B, D), lambda j: (0, 0)),
            pl.BlockSpec((D, tn), lambda j: (0, j)),
            pl.BlockSpec((1, tn), lambda j: (0, j)),
        ],
        out_specs=pl.BlockSpec((B, tn), lambda j: (0, j)),
        out_shape=jax.ShapeDtypeStruct((B, n), F32),
        compiler_params=_params(("arbitrary",)),
        name="ada",
    )(c, w_ada, b_ada.reshape(1, n))


def _ffn_kernel(x_ref, mod_ref, g_ref, wg_ref, wu_ref, wd_ref, gf_ref, o_ref, h_ref, acc_ref, *, k_mod, final, ck):
    x = x_ref[0]
    sh = mod_ref[0, k_mod:k_mod + 1, :]
    sc = mod_ref[0, k_mod + 1:k_mod + 2, :]
    ga = mod_ref[0, k_mod + 2:k_mod + 3, :]
    h_ref[...] = _rms_mod(x, g_ref[...], sh, sc).astype(BF16)
    acc_ref[...] = jnp.zeros_like(acc_ref)

    n_chunks = wg_ref.shape[1] // ck
    h = h_ref[...]

    def gate_up(j):
        return _mm(h, wg_ref[:, j * ck:(j + 1) * ck]), _mm(h, wu_ref[:, j * ck:(j + 1) * ck])

    gu = gate_up(0)
    for j in range(n_chunks):
        g, u = gu
        if j + 1 < n_chunks:
            gu = gate_up(j + 1)
        a = (g * jax.nn.sigmoid(g) * u).astype(BF16)
        acc_ref[...] += _mm(a, wd_ref[j * ck:(j + 1) * ck, :])
    out = x + 0.5 * ga * acc_ref[...]
    if final:
        ms = jnp.mean(out * out, axis=-1, keepdims=True)
        out = out * lax.rsqrt(ms + RMS_EPS) * gf_ref[...]
    o_ref[0] = out


def _ffn(x, mod3, g, wg, wu, wd, g_final, *, k_mod, final, tm, ck):
    B, T, D = x.shape
    F = wg.shape[1]
    assert F % ck == 0
    const2 = lambda b, t: (0, 0)
    return pl.pallas_call(
        functools.partial(_ffn_kernel, k_mod=k_mod, final=final, ck=ck),
        grid=(B, T // tm),
        in_specs=[
            pl.BlockSpec((1, tm, D), lambda b, t: (b, t, 0)),
            pl.BlockSpec((1, N_MOD, D), lambda b, t: (b, 0, 0)),
            pl.BlockSpec((1, D), lambda b, t: (0, 0)),
            pl.BlockSpec((D, F), const2, pipeline_mode=pl.Buffered(1)),
            pl.BlockSpec((D, F), const2, pipeline_mode=pl.Buffered(1)),
            pl.BlockSpec((F, D), const2, pipeline_mode=pl.Buffered(1)),
            pl.BlockSpec((1, D), lambda b, t: (0, 0)),
        ],
        out_specs=pl.BlockSpec((1, tm, D), lambda b, t: (b, t, 0)),
        out_shape=jax.ShapeDtypeStruct((B, T, D), F32),
        scratch_shapes=[pltpu.VMEM((tm, D), BF16), pltpu.VMEM((tm, D), F32)],
        compiler_params=_params(("arbitrary", "arbitrary")),
        name="ffn_final" if final else "ffn",
    )(x, mod3, g.reshape(1, D), wg, wu, wd, g_final.reshape(1, D))


def _inproj_kernel(x_ref, mod_ref, g_ref, pos_ref, invf_ref, bf_ref, sel_ref, oneq_ref, onek_ref, tri_ref, w_ref,
                   qn_ref, kcr_ref, vcr_ref, ks_ref, vsT_ref, kw_ref, vwT_ref, qf_ref, kf_ref, vfT_ref, gT_ref,
                   h_ref, carry_ref, *, tm):
    t_idx = pl.program_id(1)
    sh = mod_ref[0, 3:4, :]
    sc = mod_ref[0, 4:5, :]
    h_ref[...] = _rms_mod(x_ref[0], g_ref[...], sh, sc).astype(BF16)
    h = h_ref[...]

    lane = lax.broadcasted_iota(jnp.int32, (tm, LANES), 1)
    low = lane < HEAD_DIM
    first8 = (lane & (HEAD_DIM - 1)) < (ROPE_DIM // 2)
    ang = invf_ref[...] * pos_ref[0]
    cos8 = jnp.cos(ang)
    sin8 = jnp.sin(ang)
    rest = HEAD_DIM - ROPE_DIM
    cos_t = jnp.concatenate([cos8, cos8, jnp.ones((rest, tm), F32)] * 2, axis=0).T
    sin_t = jnp.concatenate([-sin8, sin8, jnp.zeros((rest, tm), F32)] * 2, axis=0).T

    def rope(xs):
        partner = jnp.where(first8, pltpu.roll(xs, LANES - ROPE_DIM // 2, 1), pltpu.roll(xs, ROPE_DIM // 2, 1))
        return xs * cos_t + partner * sin_t

    def split_heads(xs):
        return jnp.where(low, xs, 0.0), jnp.where(low, pltpu.roll(xs, HEAD_DIM, 1), 0.0)

    def with_ones(vt):
        extra = (lax.broadcasted_iota(jnp.int32, (V_ROWS - HEAD_DIM, vt.shape[1]), 0) == 0).astype(BF16)
        return jnp.concatenate([vt, extra], axis=0)

    sm = _mm(h, w_ref[:, _OFF_SMALL:_OFF_SMALL + LANES])
    gT_ref[0] = jax.nn.sigmoid(sm).T[:_GATE_COLS, :]
    xl = sm + bf_ref[...]
    logf = jnp.minimum(xl, 0.0) - jnp.log1p(jnp.exp(-jnp.abs(xl)))
    tri = tri_ref[...]
    l0, l1, l2 = _split3(logf)

    @pl.when(t_idx == 0)
    def _():
        carry_ref[...] = jnp.zeros_like(carry_ref)

    cf = _mm(tri, l0) + _mm(tri, l1) + _mm(tri, l2) + carry_ref[...]
    carry_ref[...] = cf[tm - 1:tm, :]
    pieces = _split3(cf * LOG2E)
    placed = _mm(pieces[0], sel_ref[0]) + _mm(pieces[1], sel_ref[1]) + _mm(pieces[2], sel_ref[2])
    q_tail = placed[:, :LANES]
    k_tail = placed[:, LANES:]

    qn = _mm(h, w_ref[:, _OFF_QN:_OFF_QN + 512])
    scale = HEAD_DIM ** -0.5 * LOG2E
    for j in range(4):
        a, b = split_heads(rope(qn[:, j * LANES:(j + 1) * LANES]) * scale)
        qn_ref[0, 2 * j] = a.astype(BF16)
        qn_ref[0, 2 * j + 1] = b.astype(BF16)

    nsa_tk = vsT_ref.shape[-1]
    kv = _mm(h, w_ref[:, _OFF_KV:_OFF_KV + 768])
    kcr_ref[0] = rope(kv[:, 0:LANES])
    vcr_ref[0] = kv[:, LANES:2 * LANES]
    blk = lax.shift_right_logical(t_idx * tm + lax.broadcasted_iota(jnp.int32, (tm, LANES), 0), SLC_BLOCK.bit_length() - 1)
    onehot = (lane == blk + HEAD_DIM).astype(F32)
    for k_out, v_out, off, tail in ((ks_ref, vsT_ref, 2 * LANES, onehot), (kw_ref, vwT_ref, 4 * LANES, None)):
        a, b = split_heads(rope(kv[:, off:off + LANES]))
        if tail is not None:
            a, b = a + tail, b + tail
        k_out[0, 0] = a.astype(BF16)
        k_out[0, 1] = b.astype(BF16)
        vT = kv[:, off + LANES:off + 2 * LANES].T.astype(BF16)
        for gi in range(N_KV_NSA):
            for c in range(tm // nsa_tk):
                v_out[0, gi, c] = with_ones(vT[gi * HEAD_DIM:(gi + 1) * HEAD_DIM, c * nsa_tk:(c + 1) * nsa_tk])

    qf = _mm(h, w_ref[:, _OFF_QF:_OFF_QF + 512])
    kf = _mm(h, w_ref[:, _OFF_KF:_OFF_KF + 512])
    for j in range(4):
        qa, qb = split_heads(qf[:, j * LANES:(j + 1) * LANES] * scale)
        ka, kb = split_heads(kf[:, j * LANES:(j + 1) * LANES])
        for hd, qh, kh in ((2 * j, qa, ka), (2 * j + 1, qb, kb)):
            qf_ref[0, hd] = (qh + q_tail + oneq_ref[hd:hd + 1, :]).astype(BF16)
            kf_ref[0, hd] = (kh + k_tail + onek_ref[hd:hd + 1, :]).astype(BF16)
    vf = _mm(h, w_ref[:, _OFF_VF:_OFF_VF + 512])
    fox_tk = vfT_ref.shape[-1]
    for j in range(4):
        vT = vf[:, j * LANES:(j + 1) * LANES].T.astype(BF16)
        for hh in range(2):
            for c in range(tm // fox_tk):
                vfT_ref[0, 2 * j + hh, c] = with_ones(vT[hh * HEAD_DIM:(hh + 1) * HEAD_DIM, c * fox_tk:(c + 1) * fox_tk])


def _inproj(x, mod3, g, pos_row, invf, bf_row, sel, oneq, onek, tri, w, *, tm, nsa_tk, fox_tk):
    B, T, D = x.shape
    H, G = N_HEADS_NSA, N_KV_NSA
    bt = lambda b, t: (b, t, 0)
    hb = lambda b, t: (b, 0, t, 0)
    vb = lambda b, t: (b, 0, t, 0, 0)
    out_shape = [
        jax.ShapeDtypeStruct((B, H, T, LANES), BF16),
        jax.ShapeDtypeStruct((B, T, LANES), F32),
        jax.ShapeDtypeStruct((B, T, LANES), F32),
        jax.ShapeDtypeStruct((B, G, T, LANES), BF16),
        jax.ShapeDtypeStruct((B, G, T // nsa_tk, V_ROWS, nsa_tk), BF16),
        jax.ShapeDtypeStruct((B, G, T, LANES), BF16),
        jax.ShapeDtypeStruct((B, G, T // nsa_tk, V_ROWS, nsa_tk), BF16),
        jax.ShapeDtypeStruct((B, N_HEADS_FOX, T, LANES), BF16),
        jax.ShapeDtypeStruct((B, N_HEADS_FOX, T, LANES), BF16),
        jax.ShapeDtypeStruct((B, N_HEADS_FOX, T // fox_tk, V_ROWS, fox_tk), BF16),
        jax.ShapeDtypeStruct((B, _GATE_COLS, T), F32),
    ]
    out_specs = [
        pl.BlockSpec((1, H, tm, LANES), hb),
        pl.BlockSpec((1, tm, LANES), bt),
        pl.BlockSpec((1, tm, LANES), bt),
        pl.BlockSpec((1, G, tm, LANES), hb),
        pl.BlockSpec((1, G, tm // nsa_tk, V_ROWS, nsa_tk), vb),
        pl.BlockSpec((1, G, tm, LANES), hb),
        pl.BlockSpec((1, G, tm // nsa_tk, V_ROWS, nsa_tk), vb),
        pl.BlockSpec((1, N_HEADS_FOX, tm, LANES), hb),
        pl.BlockSpec((1, N_HEADS_FOX, tm, LANES), hb),
        pl.BlockSpec((1, N_HEADS_FOX, tm // fox_tk, V_ROWS, fox_tk), vb),
        pl.BlockSpec((1, _GATE_COLS, tm), lambda b, t: (b, 0, t)),
    ]
    row = lambda b, t: (0, 0)
    return pl.pallas_call(
        functools.partial(_inproj_kernel, tm=tm),
        grid=(B, T // tm),
        in_specs=[
            pl.BlockSpec((1, tm, D), bt),
            pl.BlockSpec((1, N_MOD, D), lambda b, t: (b, 0, 0)),
            pl.BlockSpec((1, D), row),
            pl.BlockSpec((1, 1, tm), lambda b, t: (b, 0, t)),
            pl.BlockSpec((ROPE_DIM // 2, tm), row),
            pl.BlockSpec((1, LANES), row),
            pl.BlockSpec((3, LANES, 2 * LANES), lambda b, t: (0, 0, 0)),
            pl.BlockSpec((N_HEADS_FOX, LANES), row),
            pl.BlockSpec((N_HEADS_FOX, LANES), row),
            pl.BlockSpec((tm, tm), row),
            pl.BlockSpec((D, _W_IN_COLS), row, pipeline_mode=pl.Buffered(1)),
        ],
        out_specs=out_specs,
        out_shape=out_shape,
        scratch_shapes=[pltpu.VMEM((tm, D), BF16), pltpu.VMEM((1, LANES), F32)],
        compiler_params=_params(("arbitrary", "arbitrary")),
        name="inproj",
    )(x, mod3, g.reshape(1, D), pos_row, invf, bf_row, sel, oneq, onek, tri, w)


def _gelu_tanh(x):
    c = np.float32(np.sqrt(2.0 / np.pi))
    return x * (0.5 * (1.0 + jnp.tanh(c * (x + 0.044715 * (x * x * x)))))


def _compress_kernel(zk_ref, zv_ref, pe_ref, wkt_ref, wkb_ref, wvt_ref, wvb_ref, w2k_ref, w2v_ref, ovT_ref, kc_ref, lhs_ref):
    nsub = zk_ref.shape[1] // CMP_STRIDE

    def mlp(z_ref, pe_top, pe_bot, wt_ref, wb_ref):
        a = b = None
        for j in range(CMP_STRIDE):
            xj = z_ref[0, pl.ds(j, nsub, stride=CMP_STRIDE), :]
            lanes = slice(j * LANES, (j + 1) * LANES)
            aj = _mm((xj + pe_ref[pe_top:pe_top + 1, lanes]).astype(BF16), wt_ref[lanes, :])
            bj = _mm((xj + pe_ref[pe_bot:pe_bot + 1, lanes]).astype(BF16), wb_ref[lanes, :])
            a = aj if a is None else a + aj
            b = bj if b is None else b + bj
        return _gelu_tanh(a + pltpu.roll(b, nsub - 1, 0))

    hk = mlp(zk_ref, 0, 1, wkt_ref, wkb_ref)
    hv = mlp(zv_ref, 2, 3, wvt_ref, wvb_ref)
    for gi in range(N_KV_NSA):
        kc_ref[0, gi] = _mm(hk[:, gi * CMP_HIDDEN:(gi + 1) * CMP_HIDDEN].astype(BF16), w2k_ref[...]).astype(BF16)
        vc = _mm(hv[:, gi * CMP_HIDDEN:(gi + 1) * CMP_HIDDEN].astype(BF16), w2v_ref[...])
        ones = (lax.broadcasted_iota(jnp.int32, (V_ROWS - HEAD_DIM, nsub), 0) == 0).astype(BF16)
        lhs_ref[0, gi] = jnp.concatenate([vc.T[:HEAD_DIM, :].astype(BF16), ones, ovT_ref[...]], axis=0)


def _compress(zk, zv, pe4, wkt, wkb, wvt, wvb, w2k, w2v, ovT):
    B, T, _ = zk.shape
    nsub = T // CMP_STRIDE
    zc = CMP_STRIDE * LANES
    G = N_KV_NSA
    c2 = lambda b: (0, 0)
    zspec = pl.BlockSpec((1, T, LANES), lambda b: (b, 0, 0))
    wspec = pl.BlockSpec((zc, G * CMP_HIDDEN), c2)
    w2spec = pl.BlockSpec((CMP_HIDDEN, LANES), c2)
    return pl.pallas_call(
        _compress_kernel,
        grid=(B,),
        in_specs=[zspec, zspec, pl.BlockSpec((4, zc), c2), wspec, wspec, wspec, wspec, w2spec, w2spec,
                  pl.BlockSpec(ovT.shape, c2)],
        out_specs=[
            pl.BlockSpec((1, G, nsub, LANES), lambda b: (b, 0, 0, 0)),
            pl.BlockSpec((1, G, V_ROWS + ovT.shape[0], nsub), lambda b: (b, 0, 0, 0)),
        ],
        out_shape=[
            jax.ShapeDtypeStruct((B, G, nsub, LANES), BF16),
            jax.ShapeDtypeStruct((B, G, V_ROWS + ovT.shape[0], nsub), BF16),
        ],
        compiler_params=_params(("arbitrary",)),
        name="compress",
    )(zk, zv, pe4, wkt, wkb, wvt, wvb, w2k, w2v, ovT)


def _flash_init(m_ref, acc_ref):
    m_ref[...] = jnp.full_like(m_ref, NEG_INF)
    acc_ref[...] = jnp.zeros_like(acc_ref)


def _chain_softmax(s_list, m_ref, tile_max=None):
    m_old = m_ref[...]
    m_new = m_old
    if tile_max is not None:
        m_new = jnp.maximum(m_new, tile_max)
    else:
        for s in s_list:
            m_new = jnp.maximum(m_new, jnp.max(s, axis=0, keepdims=True))
    m_ref[...] = m_new
    return jnp.exp2(m_old - m_new), [jnp.exp2(s - m_new).astype(BF16) for s in s_list]


def _chain_values(staged, v_list, acc_ref):
    alpha, ps = staged
    pv = _mm(v_list[0], ps[0])
    for v, p in zip(v_list[1:], ps[1:]):
        pv = pv + _mm(v, p)
    acc_ref[...] = alpha * acc_ref[...] + pv


def _normalized(acc):
    return acc[:HEAD_DIM] / acc[HEAD_DIM:HEAD_DIM + 1]


def _pipelined_sweep(n, n_chains, store, softmax, values):
    chains = range(n_chains)

    def by_parity(i, fn):
        @pl.when((i & 1) == 0)
        def _():
            fn(0)

        @pl.when((i & 1) == 1)
        def _():
            fn(1)

    def step(nxt, masked, cur, par):
        staged = None
        for c in chains:
            if nxt is not None:
                store(nxt, 1 - par, masked, c)
            new = softmax(cur, par, c)
            if staged is not None:
                values(cur, staged, c - 1)
            staged = new
        values(cur, staged, n_chains - 1)

    @pl.when(n == 0)
    def _():
        for c in chains:
            store(0, 0, True, c)

    @pl.when(n > 0)
    def _():
        for c in chains:
            store(0, 0, False, c)

    n_plain = n - 1

    def body(j, carry):
        step(2 * j + 1, False, 2 * j, 0)
        step(2 * j + 2, False, 2 * j + 1, 1)
        return carry

    lax.fori_loop(0, n_plain // 2, body, 0)

    @pl.when((n_plain >= 1) & ((n_plain & 1) == 1))
    def _():
        step(n_plain, False, n_plain - 1, 0)

    @pl.when(n > 0)
    def _():
        by_parity(n - 1, lambda par: step(n, True, n - 1, par))

    by_parity(n, lambda par: step(None, False, n, par))


_FOX_HEADS_PER_STEP = 8


def _fox_kernel(q_ref, k_ref, vT_ref, o_ref, m_ref, acc_ref, s0_ref, s1_ref, x0_ref, x1_ref, *, tq, tk):
    qi = pl.program_id(2)
    nkc = tq // tk
    heads = range(_FOX_HEADS_PER_STEP)
    s_bufs = (s0_ref, s1_ref)
    x_bufs = (x0_ref, x1_ref)
    row = lax.broadcasted_iota(jnp.int32, (tk, tq), 0)
    lane = lax.broadcasted_iota(jnp.int32, (tk, tq), 1)
    for hh in heads:
        _flash_init(m_ref.at[hh], acc_ref.at[hh])

    def store(kt, buf, masked, hh):
        tile_max = None
        for c in range(nkc):
            k0 = pl.multiple_of(kt * tq + c * tk, tk)
            s = _nt(k_ref[0, hh, pl.ds(k0, tk), :], q_ref[0, hh])
            if masked:
                s = jnp.where(row + c * tk <= lane, s, NEG_INF)
            s_bufs[buf][hh, c * tk:(c + 1) * tk, :] = s
            cm = jnp.max(s, axis=0, keepdims=True)
            tile_max = cm if tile_max is None else jnp.maximum(tile_max, cm)
        x_bufs[buf][hh] = tile_max

    def softmax(kt, buf, hh):
        return _chain_softmax([s_bufs[buf][hh, c * tk:(c + 1) * tk, :] for c in range(nkc)], m_ref.at[hh], x_bufs[buf][hh])

    def values(kt, staged, hh):
        _chain_values(staged, [vT_ref[0, hh, kt * nkc + c] for c in range(nkc)], acc_ref.at[hh])

    _pipelined_sweep(qi, len(heads), store, softmax, values)
    for pr in range(_FOX_HEADS_PER_STEP // 2):
        st = jnp.concatenate([_normalized(acc_ref[2 * pr]), _normalized(acc_ref[2 * pr + 1])], axis=0)
        o_ref[0, :, pr * LANES:(pr + 1) * LANES] = st.T.astype(BF16)


def _fox(qf, kf, vfT, *, tq):
    B, H, T, _ = qf.shape
    tk = vfT.shape[-1]
    nh = _FOX_HEADS_PER_STEP
    return pl.pallas_call(
        functools.partial(_fox_kernel, tq=tq, tk=tk),
        grid=(B, H // nh, T // tq),
        in_specs=[
            pl.BlockSpec((1, nh, tq, LANES), lambda b, p, i: (b, p, i, 0)),
            pl.BlockSpec((1, nh, T, LANES), lambda b, p, i: (b, p, 0, 0)),
            pl.BlockSpec((1, nh, T // tk, V_ROWS, tk), lambda b, p, i: (b, p, 0, 0, 0)),
        ],
        out_specs=pl.BlockSpec((1, tq, nh * HEAD_DIM), lambda b, p, i: (b, i, p)),
        out_shape=jax.ShapeDtypeStruct((B, T, H * HEAD_DIM), BF16),
        scratch_shapes=[
            pltpu.VMEM((nh, 1, tq), F32),
            pltpu.VMEM((nh, V_ROWS, tq), F32),
            pltpu.VMEM((nh, tq, tq), F32),
            pltpu.VMEM((nh, tq, tq), F32),
            pltpu.VMEM((nh, 1, tq), F32),
            pltpu.VMEM((nh, 1, tq), F32),
        ],
        compiler_params=_params(("arbitrary", "arbitrary", "arbitrary")),
        name="fox",
    )(qf, kf, vfT)


def _softmax_pv(s_lists, v_lists):
    staged = []
    for s_list in s_lists:
        m = jnp.max(s_list[0], axis=0, keepdims=True)
        for s in s_list[1:]:
            m = jnp.maximum(m, jnp.max(s, axis=0, keepdims=True))
        staged.append([jnp.exp2(s - m).astype(BF16) for s in s_list])
    outs = []
    for ps, v_list in zip(staged, v_lists):
        pv = _mm(v_list[0], ps[0])
        for v, p in zip(v_list[1:], ps[1:]):
            pv = pv + _mm(v, p)
        outs.append(_normalized(pv))
    return outs


_NSA_LANE_SPLITS = GROUP


def _nsa_kernel(q_ref, kc_ref, cmp_lhs_ref, ks_ref, vsT_ref, kw_ref, vwT_ref, g_ref, o_ref,
                m_ref, acc_ref, imp_ref, cnt_ref, qa_ref, oc_ref, ow_ref, s0_ref, s1_ref, x0_ref, x1_ref, *, tq, tk):
    qt = pl.program_id(1)
    G = N_KV_NSA
    nq = GROUP * tq
    hw = nq // _NSA_LANE_SPLITS
    heads_per_split = GROUP // _NSA_LANE_SPLITS
    ncp = kc_ref.shape[2]
    ns = cmp_lhs_ref.shape[2] - V_ROWS
    blk_shift = SLC_BLOCK.bit_length() - 1
    t_row = qt * tq + (lax.broadcasted_iota(jnp.int32, (1, nq), 1) & (tq - 1))
    ql = lax.broadcasted_iota(jnp.int32, (1, hw), 1) & (tq - 1)
    row_k = lax.broadcasted_iota(jnp.int32, (tk, hw), 0)
    causal = row_k <= ql
    n_io = lax.broadcasted_iota(jnp.int32, (ns, tq), 0)
    cur = lax.shift_right_logical(qt * tq + lax.broadcasted_iota(jnp.int32, (ns, tq), 1), blk_shift)
    forced = (n_io == 0) | (n_io == cur) | (n_io == cur - 1)
    visible = n_io <= cur
    c_io = lax.broadcasted_iota(jnp.int32, (ncp, nq), 0)
    cmp_valid = (c_io * CMP_STRIDE + (CMP_BLOCK - 1)) <= t_row
    chains = [(g, h) for g in range(G) for h in range(_NSA_LANE_SPLITS)]

    def q_of(g):
        return q_ref[0, g * GROUP:(g + 1) * GROUP].reshape(nq, LANES)

    def q_split(g, h):
        h0 = g * GROUP + h * heads_per_split
        return q_ref[0, h0:h0 + heads_per_split].reshape(hw, LANES)

    def lanes_of(h):
        return pl.ds(h * hw, hw)

    cmp_ok = cmp_valid[:, :hw]
    s_cmp = [_nt(kc_ref[0, g], q_split(g, h)) for g, h in chains]
    p_cmp = []
    for sc in s_cmp:
        smk = jnp.where(cmp_ok, sc, NEG_INF)
        p_cmp.append(jnp.exp2(smk - jnp.max(smk, axis=0, keepdims=True)).astype(BF16))
    any_valid = t_row[:, :hw] >= (CMP_BLOCK - 1)
    imp_sum = [None] * G
    for (g, h), pc in zip(chains, p_cmp):
        r_all = _mm(cmp_lhs_ref[0, g], pc)
        inv = jnp.where(any_valid, 1.0 / r_all[HEAD_DIM:HEAD_DIM + 1], 0.0)
        oc_ref[g, :, lanes_of(h)] = r_all[:HEAD_DIM] * inv
        imp_h = r_all[V_ROWS:] * inv
        imp_sum[g] = imp_h if imp_sum[g] is None else imp_sum[g] + imp_h
    for g in range(G):
        imp_ref[g] = jnp.where(forced, FORCE_SCORE, jnp.where(visible, imp_sum[g], -1.0))

    n_back = WINDOW // tk
    hq = tq // 2
    row_h = lax.broadcasted_iota(jnp.int32, (hq, hq), 0)
    lane_h = lax.broadcasted_iota(jnp.int32, (hq, hq), 1)

    @pl.when(qt >= n_back)
    def _():
        base = pl.multiple_of((qt - 1) * tk, tk)
        s_lists, v_lists, where_to = [], [], []
        for g, h in chains:
            for sub in range(2):
                q = q_ref[0, g * GROUP + h, sub * hq:(sub + 1) * hq, :]
                s3 = _nt(kw_ref[0, g, pl.ds(base + sub * hq, 3 * hq), :], q)
                s_lists.append([jnp.where(row_h > lane_h, s3[:hq], NEG_INF),
                                s3[hq:2 * hq],
                                jnp.where(row_h <= lane_h, s3[2 * hq:], NEG_INF)])
                halves = [vwT_ref[0, g, qt - 1, :, :hq], vwT_ref[0, g, qt - 1, :, hq:],
                          vwT_ref[0, g, qt, :, :hq], vwT_ref[0, g, qt, :, hq:]]
                v_lists.append(halves[sub:sub + 3])
                where_to.append((g, pl.ds(h * tq + sub * hq, hq)))
        for (g, lanes), o in zip(where_to, _softmax_pv(s_lists, v_lists)):
            ow_ref[g, :, lanes] = o

    @pl.when(qt < n_back)
    def _():
        s_lists = []
        for g, h in chains:
            q = q_split(g, h)
            s_lists.append([jnp.where((row_k + j * tk) <= (qt * tq + ql), _nt(kw_ref[0, g, j * tk:(j + 1) * tk, :], q), NEG_INF)
                            for j in range(n_back)])
        v_lists = [[vwT_ref[0, g, j] for j in range(n_back)] for g, h in chains]
        for (g, h), o in zip(chains, _softmax_pv(s_lists, v_lists)):
            ow_ref[g, :, lanes_of(h)] = o

    n_vis = jnp.minimum((qt + 1) * (tq // SLC_BLOCK), ns)
    sub = 8
    sub_io = lax.broadcasted_iota(jnp.int32, (sub, tq), 0)
    cnt_ref[...] = jnp.zeros_like(cnt_ref)
    for mb in range(ns // sub):
        @pl.when(mb * sub < n_vis)
        def _():
            for g in range(G):
                groups = [imp_ref[g, j * sub:(j + 1) * sub, :] for j in range(ns // sub)]
                counts = [cnt_ref[g, j * sub:(j + 1) * sub, :] for j in range(ns // sub)]
                for mi in range(sub):
                    row = jnp.broadcast_to(groups[mb][mi:mi + 1, :], (sub, tq))
                    for j in range(ns // sub):
                        if j < mb:
                            beats = row > groups[j]
                        elif j > mb:
                            beats = row >= groups[j]
                        else:
                            beats = (row > groups[j]) | ((row >= groups[j]) & (sub_io > mi))
                        counts[j] = counts[j] + beats.astype(F32)
                for j in range(ns // sub):
                    cnt_ref[g, j * sub:(j + 1) * sub, :] = counts[j]

    for g in range(G):
        bias = jnp.where(cnt_ref[g] < float(TOP_N), 0.0, NEG_INF)
        parts = [jnp.zeros((HEAD_DIM, tq), F32), bias]
        if ns < LANES - HEAD_DIM:
            parts.append(jnp.zeros((LANES - HEAD_DIM - ns, tq), F32))
        bias_t = jnp.concatenate(parts, axis=0).T.astype(BF16)
        qa_ref[g] = q_of(g) + jnp.concatenate([bias_t] * GROUP, axis=0)
        _flash_init(m_ref.at[g], acc_ref.at[g])

    slc_refs = [(m_ref.at[g, :, lanes_of(h)], acc_ref.at[g, :, lanes_of(h)]) for g, h in chains]

    s_bufs = (s0_ref, s1_ref)
    x_bufs = (x0_ref, x1_ref)

    def slc_store(kj, buf, masked, c):
        g, h = chains[c]
        k0 = pl.multiple_of(kj * tk, tk)
        s = _nt(ks_ref[0, g, pl.ds(k0, tk), :], qa_ref[g, lanes_of(h), :])
        if masked:
            s = jnp.where(causal, s, NEG_INF)
        s_bufs[buf][g, :, lanes_of(h)] = s
        x_bufs[buf][g, :, lanes_of(h)] = jnp.max(s, axis=0, keepdims=True)

    def slc_softmax(kj, buf, c):
        g, h = chains[c]
        return _chain_softmax([s_bufs[buf][g, :, lanes_of(h)]], slc_refs[c][0], x_bufs[buf][g, :, lanes_of(h)])

    def slc_values(kj, staged, c):
        _chain_values(staged, [vsT_ref[0, chains[c][0], kj]], slc_refs[c][1])

    _pipelined_sweep(qt, len(chains), slc_store, slc_softmax, slc_values)

    for g in range(G):
        def gate_row(j):
            return jnp.concatenate([g_ref[0, g, 3 * r + j:3 * r + j + 1, :] for r in range(GROUP)], axis=1)

        oT = gate_row(0) * oc_ref[g] + gate_row(1) * _normalized(acc_ref[g]) + gate_row(2) * ow_ref[g]
        for pr in range(GROUP // 2):
            st = jnp.concatenate([oT[:, (2 * pr) * tq:(2 * pr + 1) * tq], oT[:, (2 * pr + 1) * tq:(2 * pr + 2) * tq]], axis=0)
            col = (g * GROUP // 2 + pr) * LANES
            o_ref[0, :, col:col + LANES] = st.T.astype(BF16)


def _nsa(qn, kc, cmp_lhs, ks, vsT, kw, vwT, gT4, *, tq):
    B, H, T, _ = qn.shape
    G = N_KV_NSA
    tk = vsT.shape[-1]
    ncp = kc.shape[2]
    ns = cmp_lhs.shape[2] - V_ROWS
    assert tq == tk and WINDOW == tk and _NSA_LANE_SPLITS == GROUP and T >= WINDOW and ns <= LANES - HEAD_DIM and ns % 8 == 0
    nq = GROUP * tq
    kvspec = pl.BlockSpec((1, G, T, LANES), lambda b, i: (b, 0, 0, 0))
    vtspec = pl.BlockSpec((1, G, T // tk, V_ROWS, tk), lambda b, i: (b, 0, 0, 0, 0))
    return pl.pallas_call(
        functools.partial(_nsa_kernel, tq=tq, tk=tk),
        grid=(B, T // tq),
        in_specs=[
            pl.BlockSpec((1, H, tq, LANES), lambda b, i: (b, 0, i, 0)),
            pl.BlockSpec((1, G, ncp, LANES), lambda b, i: (b, 0, 0, 0)),
            pl.BlockSpec((1, G, V_ROWS + ns, ncp), lambda b, i: (b, 0, 0, 0)),
            kvspec, vtspec, kvspec, vtspec,
            pl.BlockSpec((1, G, 3 * GROUP, tq), lambda b, i: (b, 0, 0, i)),
        ],
        out_specs=pl.BlockSpec((1, tq, H * HEAD_DIM), lambda b, i: (b, i, 0)),
        out_shape=jax.ShapeDtypeStruct((B, T, H * HEAD_DIM), BF16),
        scratch_shapes=[
            pltpu.VMEM((G, 1, nq), F32),
            pltpu.VMEM((G, V_ROWS, nq), F32),
            pltpu.VMEM((G, ns, tq), F32),
            pltpu.VMEM((G, ns, tq), F32),
            pltpu.VMEM((G, nq, LANES), BF16),
            pltpu.VMEM((G, HEAD_DIM, nq), F32),
            pltpu.VMEM((G, HEAD_DIM, nq), F32),
            pltpu.VMEM((G, tk, nq), F32),
            pltpu.VMEM((G, tk, nq), F32),
            pltpu.VMEM((G, 1, nq), F32),
            pltpu.VMEM((G, 1, nq), F32),
        ],
        compiler_params=_params(("arbitrary", "arbitrary")),
        name="nsa",
    )(qn, kc, cmp_lhs, ks, vsT, kw, vwT, gT4)


def _mixout_kernel(x_ref, mod_ref, g_ref, on_ref, of_ref, wgm_ref, wun_ref, wuf_ref, wo_ref, o_ref):
    x = x_ref[0]
    D = x.shape[-1]
    sh = mod_ref[0, 3:4, :]
    sc = mod_ref[0, 4:5, :]
    ga = mod_ref[0, 5:6, :]
    h = _rms_mod(x, g_ref[...], sh, sc).astype(BF16)
    gm = _mm(h, wgm_ref[...])
    un = _mm(on_ref[0], wun_ref[...])
    uf = _mm(of_ref[0], wuf_ref[...])
    merged = jax.nn.sigmoid(gm[:, :D]) * un + jax.nn.sigmoid(gm[:, D:]) * uf
    y = _mm(merged.astype(BF16), wo_ref[...])
    o_ref[0] = x + ga * y


def _mixout(x, mod3, g, o_nsa, o_fox, wgm, wun, wuf, wo, *, tm):
    B, T, D = x.shape
    bt = lambda b, t: (b, t, 0)
    c2 = lambda b, t: (0, 0)
    dn = o_nsa.shape[-1]
    return pl.pallas_call(
        _mixout_kernel,
        grid=(B, T // tm),
        in_specs=[
            pl.BlockSpec((1, tm, D), bt),
            pl.BlockSpec((1, N_MOD, D), lambda b, t: (b, 0, 0)),
            pl.BlockSpec((1, D), c2),
            pl.BlockSpec((1, tm, dn), bt),
            pl.BlockSpec((1, tm, dn), bt),
            pl.BlockSpec((D, 2 * D), c2, pipeline_mode=pl.Buffered(1)),
            pl.BlockSpec((dn, D), c2, pipeline_mode=pl.Buffered(1)),
            pl.BlockSpec((dn, D), c2, pipeline_mode=pl.Buffered(1)),
            pl.BlockSpec((D, D), c2, pipeline_mode=pl.Buffered(1)),
        ],
        out_specs=pl.BlockSpec((1, tm, D), bt),
        out_shape=jax.ShapeDtypeStruct((B, T, D), F32),
        compiler_params=_params(("arbitrary", "arbitrary")),
        name="mixout",
    )(x, mod3, g.reshape(1, D), o_nsa, o_fox, wgm, wun, wuf, wo)


def _compress_weights(pe, w1, w2):
    half = CMP_BLOCK // 2
    eye = jnp.eye(N_KV_NSA, dtype=F32)

    def expand(w_half):
        w3 = w_half.reshape(half, HEAD_DIM, CMP_HIDDEN)
        return jnp.einsum('jdn,gh->jgdhn', w3, eye).reshape(half * N_KV_NSA * HEAD_DIM, N_KV_NSA * CMP_HIDDEN).astype(BF16)

    def pe_row(pe_half):
        return jnp.broadcast_to(pe_half[:, None, :], (half, N_KV_NSA, HEAD_DIM)).reshape(1, -1)

    w2p = jnp.pad(w2, ((0, 0), (0, LANES - HEAD_DIM))).astype(BF16)
    return (pe_row(pe[:half]), pe_row(pe[half:]), expand(w1[:half * HEAD_DIM]), expand(w1[half * HEAD_DIM:]), w2p)


def kernel(x, c, positions, w_ada, b_ada, g_ffn1, w_gate1, w_up1, w_down1, g_mix, w_in, b_forget, pe_ck, w1_ck, w2_ck, pe_cv, w1_cv, w2_cv, w_up_nsa, w_up_fox, w_o, g_ffn2, w_gate2, w_up2, w_down2, g_final):
    B, T, D = x.shape
    depth = w_ada.shape[0]
    tm = 512
    ffn_ck = 256
    nsa_tq, nsa_tk = 512, 512
    fox_tq, fox_tk = 512, 512
    n_slc = T // SLC_BLOCK
    n_sub = T // CMP_STRIDE

    half = ROPE_DIM // 2
    inv_freq = ROPE_THETA ** (-jnp.arange(half, dtype=F32) / half)
    invf = jnp.broadcast_to(inv_freq[:, None], (half, tm))
    cmp_start = np.arange(n_sub) * CMP_STRIDE
    slc_start = np.arange(n_slc) * SLC_BLOCK
    ov = ((cmp_start[:, None] < slc_start[None, :] + SLC_BLOCK) & (slc_start[None, :] < cmp_start[:, None] + CMP_BLOCK))
    ov[n_sub - CMP_BLOCK // CMP_STRIDE + 1:, :] = False
    ovT = jnp.asarray(ov.T, dtype=BF16)

    pos_row = positions.astype(F32)[:, None, :]
    sel_np = np.zeros((3, LANES, 2 * LANES), np.float32)
    oneq_np = np.zeros((N_HEADS_FOX, LANES), np.float32)
    onek_np = np.zeros((N_HEADS_FOX, LANES), np.float32)
    for hd in range(N_HEADS_FOX):
        for j in range(3):
            sel_np[j, _GATE_COLS + hd, _TAIL + 3 * hd + j] = 1.0
            sel_np[j, _GATE_COLS + hd, LANES + _TAIL2 + 3 * hd + j] = -1.0
            oneq_np[hd, _TAIL2 + 3 * hd + j] = 1.0
            onek_np[hd, _TAIL + 3 * hd + j] = 1.0
    sel, oneq, onek = jnp.asarray(sel_np, BF16), jnp.asarray(oneq_np), jnp.asarray(onek_np)
    tri = jnp.asarray(np.tril(np.ones((tm, tm), np.float32)), BF16)
    c_in = c
    for l in range(depth):
        mod3 = _ada(c_in, w_ada[l], b_ada[l]).reshape(B, N_MOD, D)
        wg1, wu1, wd1 = w_gate1[l].astype(BF16), w_up1[l].astype(BF16), w_down1[l].astype(BF16)
        wg2, wu2, wd2 = w_gate2[l].astype(BF16), w_up2[l].astype(BF16), w_down2[l].astype(BF16)
        wl = w_in[l]
        small = jnp.concatenate([wl[:, 1280:1304], wl[:, 2840:2848], jnp.zeros((D, LANES - 32), F32)], axis=1)
        w_proj = jnp.concatenate([wl[:, :1280], wl[:, 1304:2840], small], axis=1).astype(BF16)
        w_gm = wl[:, 2848:].astype(BF16)
        bf_row = jnp.zeros((1, LANES), F32).at[0, _GATE_COLS:_GATE_COLS + N_HEADS_FOX].set(b_forget[l])

        x = _ffn(x, mod3, g_ffn1[l], wg1, wu1, wd1, g_final, k_mod=0, final=False, tm=tm, ck=ffn_ck)

        (qn, kcr, vcr, ks, vsT, kw, vwT, qf, kf, vfT, gT) = _inproj(
            x, mod3, g_mix[l], pos_row, invf, bf_row, sel, oneq, onek, tri, w_proj, tm=tm, nsa_tk=nsa_tk, fox_tk=fox_tk)

        pk_t, pk_b, wk_t, wk_b, w2k = _compress_weights(pe_ck[l], w1_ck[l], w2_ck[l])
        pv_t, pv_b, wv_t, wv_b, w2v = _compress_weights(pe_cv[l], w1_cv[l], w2_cv[l])
        pe4 = jnp.concatenate([pk_t, pk_b, pv_t, pv_b], axis=0)
        kc, cmp_lhs = _compress(kcr, vcr, pe4, wk_t, wk_b, wv_t, wv_b, w2k, w2v, ovT)

        o_fox = _fox(qf, kf, vfT, tq=fox_tq)
        gT4 = gT.reshape(B, N_KV_NSA, 3 * GROUP, T)
        o_nsa = _nsa(qn, kc, cmp_lhs, ks, vsT, kw, vwT, gT4, tq=nsa_tq)

        x = _mixout(x, mod3, g_mix[l], o_nsa, o_fox, w_gm, w_up_nsa[l].astype(BF16), w_up_fox[l].astype(BF16),
                    w_o[l].astype(BF16), tm=tm)
        last = l == depth - 1
        x = _ffn(x, mod3, g_ffn2[l], wg2, wu2, wd2, g_final, k_mod=6, final=last, tm=tm, ck=ffn_ck)
    return x
```

```python
import functools

import numpy as np
import jax
import jax.numpy as jnp
from jax import lax
from jax.experimental import pallas as pl
from jax.experimental.pallas import tpu as pltpu

HEAD_DIM = 64
N_HEADS_NSA = 8
N_KV_NSA = 2
GROUP = N_HEADS_NSA // N_KV_NSA
N_HEADS_FOX = 8
CMP_BLOCK = 32
CMP_STRIDE = 16
CMP_HIDDEN = 128
SLC_BLOCK = 64
TOP_N = 16
WINDOW = 512
ROPE_THETA = 500000.0
ROPE_DIM = HEAD_DIM // 4
N_MOD = 9
RMS_EPS = 1e-6
NEG_INF = -1e30
FORCE_SCORE = 1e4

LANES = 128
LOG2E = 1.4426950408889634
V_ROWS = HEAD_DIM + 16
F32 = jnp.float32
BF16 = jnp.bfloat16
VMEM_LIMIT = 56 * 1024 * 1024

D_NSA = N_HEADS_NSA * HEAD_DIM
D_KV = N_KV_NSA * HEAD_DIM
D_FOX = N_HEADS_FOX * HEAD_DIM
N_KV_STREAMS = 6
_GATE_COLS = 3 * N_HEADS_NSA
_OFF_QN = 0
_OFF_KV = _OFF_QN + D_NSA
_OFF_QF = _OFF_KV + N_KV_STREAMS * D_KV
_OFF_KF = _OFF_QF + D_FOX
_OFF_VF = _OFF_KF + D_FOX
_OFF_SMALL = _OFF_VF + D_FOX
_W_IN_COLS = _OFF_SMALL + LANES
_TAIL = HEAD_DIM
_TAIL2 = HEAD_DIM + 36


def _params(sem):
    return pltpu.CompilerParams(dimension_semantics=sem, vmem_limit_bytes=VMEM_LIMIT)


def _nt(a, b):
    return lax.dot_general(a, b, (((1,), (1,)), ((), ())), preferred_element_type=F32)


def _mm(a, b):
    return jnp.dot(a, b, preferred_element_type=F32)


def _split3(x):
    hi = x.astype(BF16)
    r = x - hi.astype(F32)
    mid = r.astype(BF16)
    lo = (r - mid.astype(F32)).astype(BF16)
    return hi, mid, lo


def _rms_mod(x, g, shift, scale):
    ms = jnp.mean(x * x, axis=-1, keepdims=True)
    y = x * lax.rsqrt(ms + RMS_EPS) * g
    return y * (1.0 + scale) + shift


def _ada_kernel(c_ref, w_ref, b_ref, o_ref):
    c = c_ref[...]
    ca = c * jax.nn.sigmoid(c)
    h0, h1, h2 = _split3(ca)
    w0, w1, w2 = _split3(w_ref[...])
    acc = _mm(h0, w0) + _mm(h0, w1) + _mm(h1, w0)
    acc = acc + (_mm(h1, w1) + _mm(h0, w2) + _mm(h2, w0))
    o_ref[...] = acc + b_ref[...]


def _ada(c, w_ada, b_ada):
    B, D = c.shape
    n = w_ada.shape[1]
    tn = 1024
    return pl.pallas_call(
        _ada_kernel,
        grid=(n // tn,),
        in_specs=[
            pl.BlockSpec((B, D), lambda j: (0, 0)),
            pl.BlockSpec((D, tn), lambda j: (0, j)),
            pl.BlockSpec((1, tn), lambda j: (0, j)),
        ],
        out_specs=pl.BlockSpec((B, tn), lambda j: (0, j)),
        out_shape=jax.ShapeDtypeStruct((B, n), F32),
        compiler_params=_params(("arbitrary",)),
        name="ada",
    )(c, w_ada, b_ada.reshape(1, n))


def _ffn_kernel(x_ref, mod_ref, g_ref, wg_ref, wu_ref, wd_ref, gf_ref, o_ref, h_ref, acc_ref, *, k_mod, final, ck):
    x = x_ref[0]
    sh = mod_ref[0, k_mod:k_mod + 1, :]
    sc = mod_ref[0, k_mod + 1:k_mod + 2, :]
    ga = mod_ref[0, k_mod + 2:k_mod + 3, :]
    h_ref[...] = _rms_mod(x, g_ref[...], sh, sc).astype(BF16)
    acc_ref[...] = jnp.zeros_like(acc_ref)

    n_chunks = wg_ref.shape[1] // ck
    h = h_ref[...]

    def gate_up(j):
        return _mm(h, wg_ref[:, j * ck:(j + 1) * ck]), _mm(h, wu_ref[:, j * ck:(j + 1) * ck])

    gu = gate_up(0)
    for j in range(n_chunks):
        g, u = gu
        if j + 1 < n_chunks:
            gu = gate_up(j + 1)
        a = (g * jax.nn.sigmoid(g) * u).astype(BF16)
        acc_ref[...] += _mm(a, wd_ref[j * ck:(j + 1) * ck, :])
    out = x + 0.5 * ga * acc_ref[...]
    if final:
        ms = jnp.mean(out * out, axis=-1, keepdims=True)
        out = out * lax.rsqrt(ms + RMS_EPS) * gf_ref[...]
    o_ref[0] = out


def _ffn(x, mod3, g, wg, wu, wd, g_final, *, k_mod, final, tm, ck):
    B, T, D = x.shape
    F = wg.shape[1]
    assert F % ck == 0
    const2 = lambda b, t: (0, 0)
    return pl.pallas_call(
        functools.partial(_ffn_kernel, k_mod=k_mod, final=final, ck=ck),
        grid=(B, T // tm),
        in_specs=[
            pl.BlockSpec((1, tm, D), lambda b, t: (b, t, 0)),
            pl.BlockSpec((1, N_MOD, D), lambda b, t: (b, 0, 0)),
            pl.BlockSpec((1, D), lambda b, t: (0, 0)),
            pl.BlockSpec((D, F), const2, pipeline_mode=pl.Buffered(1)),
            pl.BlockSpec((D, F), const2, pipeline_mode=pl.Buffered(1)),
            pl.BlockSpec((F, D), const2, pipeline_mode=pl.Buffered(1)),
            pl.BlockSpec((1, D), lambda b, t: (0, 0)),
        ],
        out_specs=pl.BlockSpec((1, tm, D), lambda b, t: (b, t, 0)),
        out_shape=jax.ShapeDtypeStruct((B, T, D), F32),
        scratch_shapes=[pltpu.VMEM((tm, D), BF16), pltpu.VMEM((tm, D), F32)],
        compiler_params=_params(("arbitrary", "arbitrary")),
        name="ffn_final" if final else "ffn",
    )(x, mod3, g.reshape(1, D), wg, wu, wd, g_final.reshape(1, D))


def _inproj_kernel(x_ref, mod_ref, g_ref, pos_ref, invf_ref, bf_ref, sel_ref, oneq_ref, onek_ref, tri_ref, w_ref,
                   qn_ref, kcr_ref, vcr_ref, ks_ref, vsT_ref, kw_ref, vwT_ref, qf_ref, kf_ref, vfT_ref, gT_ref,
                   h_ref, carry_ref, *, tm):
    t_idx = pl.program_id(1)
    sh = mod_ref[0, 3:4, :]
    sc = mod_ref[0, 4:5, :]
    h_ref[...] = _rms_mod(x_ref[0], g_ref[...], sh, sc).astype(BF16)
    h = h_ref[...]

    lane = lax.broadcasted_iota(jnp.int32, (tm, LANES), 1)
    low = lane < HEAD_DIM
    first8 = (lane & (HEAD_DIM - 1)) < (ROPE_DIM // 2)
    ang = invf_ref[...] * pos_ref[0]
    cos8 = jnp.cos(ang)
    sin8 = jnp.sin(ang)
    rest = HEAD_DIM - ROPE_DIM
    cos_t = jnp.concatenate([cos8, cos8, jnp.ones((rest, tm), F32)] * 2, axis=0).T
    sin_t = jnp.concatenate([-sin8, sin8, jnp.zeros((rest, tm), F32)] * 2, axis=0).T

    def rope(xs):
        partner = jnp.where(first8, pltpu.roll(xs, LANES - ROPE_DIM // 2, 1), pltpu.roll(xs, ROPE_DIM // 2, 1))
        return xs * cos_t + partner * sin_t

    def split_heads(xs):
        return jnp.where(low, xs, 0.0), jnp.where(low, pltpu.roll(xs, HEAD_DIM, 1), 0.0)

    def with_ones(vt):
        extra = (lax.broadcasted_iota(jnp.int32, (V_ROWS - HEAD_DIM, vt.shape[1]), 0) == 0).astype(BF16)
        return jnp.concatenate([vt, extra], axis=0)

    sm = _mm(h, w_ref[:, _OFF_SMALL:_OFF_SMALL + LANES])
    gT_ref[0] = jax.nn.sigmoid(sm).T[:_GATE_COLS, :]
    xl = sm + bf_ref[...]
    logf = jnp.minimum(xl, 0.0) - jnp.log1p(jnp.exp(-jnp.abs(xl)))
    tri = tri_ref[...]
    l0, l1, l2 = _split3(logf)

    @pl.when(t_idx == 0)
    def _():
        carry_ref[...] = jnp.zeros_like(carry_ref)

    cf = _mm(tri, l0) + _mm(tri, l1) + _mm(tri, l2) + carry_ref[...]
    carry_ref[...] = cf[tm - 1:tm, :]
    pieces = _split3(cf * LOG2E)
    placed = _mm(pieces[0], sel_ref[0]) + _mm(pieces[1], sel_ref[1]) + _mm(pieces[2], sel_ref[2])
    q_tail = placed[:, :LANES]
    k_tail = placed[:, LANES:]

    qn = _mm(h, w_ref[:, _OFF_QN:_OFF_QN + D_NSA])
    scale = HEAD_DIM ** -0.5 * LOG2E
    for j in range(D_NSA // LANES):
        a, b = split_heads(rope(qn[:, j * LANES:(j + 1) * LANES]) * scale)
        qn_ref[0, 2 * j] = a.astype(BF16)
        qn_ref[0, 2 * j + 1] = b.astype(BF16)

    nsa_tk = vsT_ref.shape[-1]
    kv = _mm(h, w_ref[:, _OFF_KV:_OFF_KV + N_KV_STREAMS * D_KV])
    kcr_ref[0] = rope(kv[:, 0:LANES])
    vcr_ref[0] = kv[:, LANES:2 * LANES]
    blk = lax.shift_right_logical(t_idx * tm + lax.broadcasted_iota(jnp.int32, (tm, LANES), 0), SLC_BLOCK.bit_length() - 1)
    onehot = (lane == blk + HEAD_DIM).astype(F32)
    for k_out, v_out, off, tail in ((ks_ref, vsT_ref, 2 * LANES, onehot), (kw_ref, vwT_ref, 4 * LANES, None)):
        a, b = split_heads(rope(kv[:, off:off + LANES]))
        if tail is not None:
            a, b = a + tail, b + tail
        k_out[0, 0] = a.astype(BF16)
        k_out[0, 1] = b.astype(BF16)
        vT = kv[:, off + LANES:off + 2 * LANES].T.astype(BF16)
        for gi in range(N_KV_NSA):
            for c in range(tm // nsa_tk):
                v_out[0, gi, c] = with_ones(vT[gi * HEAD_DIM:(gi + 1) * HEAD_DIM, c * nsa_tk:(c + 1) * nsa_tk])

    qf = _mm(h, w_ref[:, _OFF_QF:_OFF_QF + D_FOX])
    kf = _mm(h, w_ref[:, _OFF_KF:_OFF_KF + D_FOX])
    for j in range(D_FOX // LANES):
        qa, qb = split_heads(qf[:, j * LANES:(j + 1) * LANES] * scale)
        ka, kb = split_heads(kf[:, j * LANES:(j + 1) * LANES])
        for hd, qh, kh in ((2 * j, qa, ka), (2 * j + 1, qb, kb)):
            qf_ref[0, hd] = (qh + q_tail + oneq_ref[hd:hd + 1, :]).astype(BF16)
            kf_ref[0, hd] = (kh + k_tail + onek_ref[hd:hd + 1, :]).astype(BF16)
    vf = _mm(h, w_ref[:, _OFF_VF:_OFF_VF + D_FOX])
    fox_tk = vfT_ref.shape[-1]
    for j in range(D_FOX // LANES):
        vT = vf[:, j * LANES:(j + 1) * LANES].T.astype(BF16)
        for hh in range(2):
            for c in range(tm // fox_tk):
                vfT_ref[0, 2 * j + hh, c] = with_ones(vT[hh * HEAD_DIM:(hh + 1) * HEAD_DIM, c * fox_tk:(c + 1) * fox_tk])


def _inproj(x, mod3, g, pos_row, invf, bf_row, sel, oneq, onek, tri, w, *, tm, nsa_tk, fox_tk):
    B, T, D = x.shape
    H, G = N_HEADS_NSA, N_KV_NSA
    bt = lambda b, t: (b, t, 0)
    hb = lambda b, t: (b, 0, t, 0)
    vb = lambda b, t: (b, 0, t, 0, 0)
    out_shape = [
        jax.ShapeDtypeStruct((B, H, T, LANES), BF16),
        jax.ShapeDtypeStruct((B, T, LANES), F32),
        jax.ShapeDtypeStruct((B, T, LANES), F32),
        jax.ShapeDtypeStruct((B, G, T, LANES), BF16),
        jax.ShapeDtypeStruct((B, G, T // nsa_tk, V_ROWS, nsa_tk), BF16),
        jax.ShapeDtypeStruct((B, G, T, LANES), BF16),
        jax.ShapeDtypeStruct((B, G, T // nsa_tk, V_ROWS, nsa_tk), BF16),
        jax.ShapeDtypeStruct((B, N_HEADS_FOX, T, LANES), BF16),
        jax.ShapeDtypeStruct((B, N_HEADS_FOX, T, LANES), BF16),
        jax.ShapeDtypeStruct((B, N_HEADS_FOX, T // fox_tk, V_ROWS, fox_tk), BF16),
        jax.ShapeDtypeStruct((B, _GATE_COLS, T), F32),
    ]
    out_specs = [
        pl.BlockSpec((1, H, tm, LANES), hb),
        pl.BlockSpec((1, tm, LANES), bt),
        pl.BlockSpec((1, tm, LANES), bt),
        pl.BlockSpec((1, G, tm, LANES), hb),
        pl.BlockSpec((1, G, tm // nsa_tk, V_ROWS, nsa_tk), vb),
        pl.BlockSpec((1, G, tm, LANES), hb),
        pl.BlockSpec((1, G, tm // nsa_tk, V_ROWS, nsa_tk), vb),
        pl.BlockSpec((1, N_HEADS_FOX, tm, LANES), hb),
        pl.BlockSpec((1, N_HEADS_FOX, tm, LANES), hb),
        pl.BlockSpec((1, N_HEADS_FOX, tm // fox_tk, V_ROWS, fox_tk), vb),
        pl.BlockSpec((1, _GATE_COLS, tm), lambda b, t: (b, 0, t)),
    ]
    row = lambda b, t: (0, 0)
    return pl.pallas_call(
        functools.partial(_inproj_kernel, tm=tm),
        grid=(B, T // tm),
        in_specs=[
            pl.BlockSpec((1, tm, D), bt),
            pl.BlockSpec((1, N_MOD, D), lambda b, t: (b, 0, 0)),
            pl.BlockSpec((1, D), row),
            pl.BlockSpec((1, 1, tm), lambda b, t: (b, 0, t)),
            pl.BlockSpec((ROPE_DIM // 2, tm), row),
            pl.BlockSpec((1, LANES), row),
            pl.BlockSpec((3, LANES, 2 * LANES), lambda b, t: (0, 0, 0)),
            pl.BlockSpec((N_HEADS_FOX, LANES), row),
            pl.BlockSpec((N_HEADS_FOX, LANES), row),
            pl.BlockSpec((tm, tm), row),
            pl.BlockSpec((D, _W_IN_COLS), row, pipeline_mode=pl.Buffered(1)),
        ],
        out_specs=out_specs,
        out_shape=out_shape,
        scratch_shapes=[pltpu.VMEM((tm, D), BF16), pltpu.VMEM((1, LANES), F32)],
        compiler_params=_params(("arbitrary", "arbitrary")),
        name="inproj",
    )(x, mod3, g.reshape(1, D), pos_row, invf, bf_row, sel, oneq, onek, tri, w)


def _gelu_tanh(x):
    c = np.float32(np.sqrt(2.0 / np.pi))
    return x * (0.5 * (1.0 + jnp.tanh(c * (x + 0.044715 * (x * x * x)))))


def _compress_kernel(zk_ref, zv_ref, pe_ref, wkt_ref, wkb_ref, wvt_ref, wvb_ref, w2k_ref, w2v_ref, ovT_ref, kc_ref, lhs_ref):
    nsub = zk_ref.shape[1] // CMP_STRIDE

    def mlp(z_ref, pe_top, pe_bot, wt_ref, wb_ref):
        a = b = None
        for j in range(CMP_STRIDE):
            xj = z_ref[0, pl.ds(j, nsub, stride=CMP_STRIDE), :]
            lanes = slice(j * LANES, (j + 1) * LANES)
            aj = _mm((xj + pe_ref[pe_top:pe_top + 1, lanes]).astype(BF16), wt_ref[lanes, :])
            bj = _mm((xj + pe_ref[pe_bot:pe_bot + 1, lanes]).astype(BF16), wb_ref[lanes, :])
            a = aj if a is None else a + aj
            b = bj if b is None else b + bj
        return _gelu_tanh(a + pltpu.roll(b, nsub - 1, 0))

    hk = mlp(zk_ref, 0, 1, wkt_ref, wkb_ref)
    hv = mlp(zv_ref, 2, 3, wvt_ref, wvb_ref)
    for gi in range(N_KV_NSA):
        kc_ref[0, gi] = _mm(hk[:, gi * CMP_HIDDEN:(gi + 1) * CMP_HIDDEN].astype(BF16), w2k_ref[...]).astype(BF16)
        vc = _mm(hv[:, gi * CMP_HIDDEN:(gi + 1) * CMP_HIDDEN].astype(BF16), w2v_ref[...])
        ones = (lax.broadcasted_iota(jnp.int32, (V_ROWS - HEAD_DIM, nsub), 0) == 0).astype(BF16)
        lhs_ref[0, gi] = jnp.concatenate([vc.T[:HEAD_DIM, :].astype(BF16), ones, ovT_ref[...]], axis=0)


def _compress(zk, zv, pe4, wkt, wkb, wvt, wvb, w2k, w2v, ovT):
    B, T, _ = zk.shape
    nsub = T // CMP_STRIDE
    zc = CMP_STRIDE * LANES
    G = N_KV_NSA
    c2 = lambda b: (0, 0)
    zspec = pl.BlockSpec((1, T, LANES), lambda b: (b, 0, 0))
    wspec = pl.BlockSpec((zc, G * CMP_HIDDEN), c2)
    w2spec = pl.BlockSpec((CMP_HIDDEN, LANES), c2)
    return pl.pallas_call(
        _compress_kernel,
        grid=(B,),
        in_specs=[zspec, zspec, pl.BlockSpec((4, zc), c2), wspec, wspec, wspec, wspec, w2spec, w2spec,
                  pl.BlockSpec(ovT.shape, c2)],
        out_specs=[
            pl.BlockSpec((1, G, nsub, LANES), lambda b: (b, 0, 0, 0)),
            pl.BlockSpec((1, G, V_ROWS + ovT.shape[0], nsub), lambda b: (b, 0, 0, 0)),
        ],
        out_shape=[
            jax.ShapeDtypeStruct((B, G, nsub, LANES), BF16),
            jax.ShapeDtypeStruct((B, G, V_ROWS + ovT.shape[0], nsub), BF16),
        ],
        compiler_params=_params(("arbitrary",)),
        name="compress",
    )(zk, zv, pe4, wkt, wkb, wvt, wvb, w2k, w2v, ovT)


def _flash_init(m_ref, acc_ref):
    m_ref[...] = jnp.full_like(m_ref, NEG_INF)
    acc_ref[...] = jnp.zeros_like(acc_ref)


def _chain_softmax(s_list, m_ref, tile_max=None):
    m_old = m_ref[...]
    m_new = m_old
    if tile_max is not None:
        m_new = jnp.maximum(m_new, tile_max)
    else:
        for s in s_list:
            m_new = jnp.maximum(m_new, jnp.max(s, axis=0, keepdims=True))
    m_ref[...] = m_new
    return jnp.exp2(m_old - m_new), [jnp.exp2(s - m_new).astype(BF16) for s in s_list]


def _chain_values(staged, v_list, acc_ref):
    alpha, ps = staged
    pv = _mm(v_list[0], ps[0])
    for v, p in zip(v_list[1:], ps[1:]):
        pv = pv + _mm(v, p)
    acc_ref[...] = alpha * acc_ref[...] + pv


def _normalized(acc):
    return acc[:HEAD_DIM] / acc[HEAD_DIM:HEAD_DIM + 1]


def _pipelined_sweep(n, n_chains, store, softmax, values):
    chains = range(n_chains)

    def by_parity(i, fn):
        @pl.when((i & 1) == 0)
        def _():
            fn(0)

        @pl.when((i & 1) == 1)
        def _():
            fn(1)

    def step(nxt, masked, cur, par):
        staged = None
        for c in chains:
            if nxt is not None:
                store(nxt, 1 - par, masked, c)
            new = softmax(cur, par, c)
            if staged is not None:
                values(cur, staged, c - 1)
            staged = new
        values(cur, staged, n_chains - 1)

    def first_scores(masked):
        for c in chains:
            store(0, 0, masked, c)

    @pl.when(n == 0)
    def _():
        first_scores(True)
        step(None, False, 0, 0)

    @pl.when(n == 1)
    def _():
        first_scores(False)
        step(1, True, 0, 0)
        step(None, False, 1, 1)

    @pl.when(n >= 2)
    def _():
        first_scores(False)
        step(1, False, 0, 0)
        n_mid = n - 2

        def body(j, carry):
            step(2 * j + 2, False, 2 * j + 1, 1)
            step(2 * j + 3, False, 2 * j + 2, 0)
            return carry

        lax.fori_loop(0, n_mid // 2, body, 0)

        @pl.when((n_mid & 1) == 1)
        def _():
            step(n_mid + 1, False, n_mid, 1)

        def last_two(par):
            step(n, True, n - 1, par)
            step(None, False, n, 1 - par)

        by_parity(n - 1, last_two)


_FOX_HEADS_PER_STEP = 8


def _fox_kernel(q_ref, k_ref, vT_ref, o_ref, m_ref, acc_ref, s0_ref, s1_ref, x0_ref, x1_ref, *, tq, tk):
    qi = pl.program_id(2)
    nkc = tq // tk
    heads = range(_FOX_HEADS_PER_STEP)
    s_bufs = (s0_ref, s1_ref)
    x_bufs = (x0_ref, x1_ref)
    row = lax.broadcasted_iota(jnp.int32, (tk, tq), 0)
    lane = lax.broadcasted_iota(jnp.int32, (tk, tq), 1)
    for hh in heads:
        _flash_init(m_ref.at[hh], acc_ref.at[hh])

    def store(kt, buf, masked, hh):
        tile_max = None
        for c in range(nkc):
            k0 = pl.multiple_of(kt * tq + c * tk, tk)
            s = _nt(k_ref[0, hh, pl.ds(k0, tk), :], q_ref[0, hh])
            if masked:
                s = jnp.where(row + c * tk <= lane, s, NEG_INF)
            s_bufs[buf][hh, c * tk:(c + 1) * tk, :] = s
            cm = jnp.max(s, axis=0, keepdims=True)
            tile_max = cm if tile_max is None else jnp.maximum(tile_max, cm)
        x_bufs[buf][hh] = tile_max

    def softmax(kt, buf, hh):
        return _chain_softmax([s_bufs[buf][hh, c * tk:(c + 1) * tk, :] for c in range(nkc)], m_ref.at[hh], x_bufs[buf][hh])

    def values(kt, staged, hh):
        _chain_values(staged, [vT_ref[0, hh, kt * nkc + c] for c in range(nkc)], acc_ref.at[hh])

    _pipelined_sweep(qi, len(heads), store, softmax, values)
    for pr in range(_FOX_HEADS_PER_STEP // 2):
        st = jnp.concatenate([_normalized(acc_ref[2 * pr]), _normalized(acc_ref[2 * pr + 1])], axis=0)
        o_ref[0, :, pr * LANES:(pr + 1) * LANES] = st.T.astype(BF16)


def _fox(qf, kf, vfT, *, tq):
    B, H, T, _ = qf.shape
    tk = vfT.shape[-1]
    nh = _FOX_HEADS_PER_STEP
    return pl.pallas_call(
        functools.partial(_fox_kernel, tq=tq, tk=tk),
        grid=(B, H // nh, T // tq),
        in_specs=[
            pl.BlockSpec((1, nh, tq, LANES), lambda b, p, i: (b, p, i, 0)),
            pl.BlockSpec((1, nh, T, LANES), lambda b, p, i: (b, p, 0, 0)),
            pl.BlockSpec((1, nh, T // tk, V_ROWS, tk), lambda b, p, i: (b, p, 0, 0, 0)),
        ],
        out_specs=pl.BlockSpec((1, tq, nh * HEAD_DIM), lambda b, p, i: (b, i, p)),
        out_shape=jax.ShapeDtypeStruct((B, T, H * HEAD_DIM), BF16),
        scratch_shapes=[
            pltpu.VMEM((nh, 1, tq), F32),
            pltpu.VMEM((nh, V_ROWS, tq), F32),
            pltpu.VMEM((nh, tq, tq), F32),
            pltpu.VMEM((nh, tq, tq), F32),
            pltpu.VMEM((nh, 1, tq), F32),
            pltpu.VMEM((nh, 1, tq), F32),
        ],
        compiler_params=_params(("arbitrary", "arbitrary", "arbitrary")),
        name="fox",
    )(qf, kf, vfT)


def _softmax_pv(s_lists, v_lists):
    staged = []
    for s_list in s_lists:
        m = jnp.max(s_list[0], axis=0, keepdims=True)
        for s in s_list[1:]:
            m = jnp.maximum(m, jnp.max(s, axis=0, keepdims=True))
        staged.append([jnp.exp2(s - m).astype(BF16) for s in s_list])
    outs = []
    for ps, v_list in zip(staged, v_lists):
        pv = _mm(v_list[0], ps[0])
        for v, p in zip(v_list[1:], ps[1:]):
            pv = pv + _mm(v, p)
        outs.append(_normalized(pv))
    return outs


_NSA_LANE_SPLITS = GROUP


def _nsa_kernel(q_ref, kc_ref, cmp_lhs_ref, ks_ref, vsT_ref, kw_ref, vwT_ref, g_ref, o_ref,
                m_ref, acc_ref, imp_ref, cnt_ref, qa_ref, oc_ref, ow_ref, s0_ref, s1_ref, x0_ref, x1_ref, *, tq, tk):
    qt = pl.program_id(1)
    G = N_KV_NSA
    nq = GROUP * tq
    hw = nq // _NSA_LANE_SPLITS
    heads_per_split = GROUP // _NSA_LANE_SPLITS
    ncp = kc_ref.shape[2]
    ns = cmp_lhs_ref.shape[2] - V_ROWS
    blk_shift = SLC_BLOCK.bit_length() - 1
    t_row = qt * tq + (lax.broadcasted_iota(jnp.int32, (1, nq), 1) & (tq - 1))
    ql = lax.broadcasted_iota(jnp.int32, (1, hw), 1) & (tq - 1)
    row_k = lax.broadcasted_iota(jnp.int32, (tk, hw), 0)
    causal = row_k <= ql
    n_io = lax.broadcasted_iota(jnp.int32, (ns, tq), 0)
    cur = lax.shift_right_logical(qt * tq + lax.broadcasted_iota(jnp.int32, (ns, tq), 1), blk_shift)
    forced = (n_io == 0) | (n_io == cur) | (n_io == cur - 1)
    visible = n_io <= cur
    c_io = lax.broadcasted_iota(jnp.int32, (ncp, nq), 0)
    cmp_valid = (c_io * CMP_STRIDE + (CMP_BLOCK - 1)) <= t_row
    chains = [(g, h) for g in range(G) for h in range(_NSA_LANE_SPLITS)]

    def q_of(g):
        return q_ref[0, g * GROUP:(g + 1) * GROUP].reshape(nq, LANES)

    def q_split(g, h):
        h0 = g * GROUP + h * heads_per_split
        return q_ref[0, h0:h0 + heads_per_split].reshape(hw, LANES)

    def lanes_of(h):
        return pl.ds(h * hw, hw)

    cmp_ok = cmp_valid[:, :hw]
    s_cmp = [_nt(kc_ref[0, g], q_split(g, h)) for g, h in chains]
    p_cmp = []
    for sc in s_cmp:
        smk = jnp.where(cmp_ok, sc, NEG_INF)
        p_cmp.append(jnp.exp2(smk - jnp.max(smk, axis=0, keepdims=True)).astype(BF16))
    any_valid = t_row[:, :hw] >= (CMP_BLOCK - 1)
    imp_sum = [None] * G
    for (g, h), pc in zip(chains, p_cmp):
        r_all = _mm(cmp_lhs_ref[0, g], pc)
        inv = jnp.where(any_valid, 1.0 / r_all[HEAD_DIM:HEAD_DIM + 1], 0.0)
        oc_ref[g, :, lanes_of(h)] = r_all[:HEAD_DIM] * inv
        imp_h = r_all[V_ROWS:] * inv
        imp_sum[g] = imp_h if imp_sum[g] is None else imp_sum[g] + imp_h
    for g in range(G):
        imp_ref[g] = jnp.where(forced, FORCE_SCORE, jnp.where(visible, imp_sum[g], -1.0))

    n_back = WINDOW // tk
    hq = tq // 2
    row_h = lax.broadcasted_iota(jnp.int32, (hq, hq), 0)
    lane_h = lax.broadcasted_iota(jnp.int32, (hq, hq), 1)

    @pl.when(qt >= n_back)
    def _():
        base = pl.multiple_of((qt - 1) * tk, tk)
        s_lists, v_lists, where_to = [], [], []
        for g, h in chains:
            for sub in range(2):
                q = q_ref[0, g * GROUP + h, sub * hq:(sub + 1) * hq, :]
                s3 = _nt(kw_ref[0, g, pl.ds(base + sub * hq, 3 * hq), :], q)
                s_lists.append([jnp.where(row_h > lane_h, s3[:hq], NEG_INF),
                                s3[hq:2 * hq],
                                jnp.where(row_h <= lane_h, s3[2 * hq:], NEG_INF)])
                halves = [vwT_ref[0, g, qt - 1, :, :hq], vwT_ref[0, g, qt - 1, :, hq:],
                          vwT_ref[0, g, qt, :, :hq], vwT_ref[0, g, qt, :, hq:]]
                v_lists.append(halves[sub:sub + 3])
                where_to.append((g, pl.ds(h * tq + sub * hq, hq)))
        for (g, lanes), o in zip(where_to, _softmax_pv(s_lists, v_lists)):
            ow_ref[g, :, lanes] = o

    @pl.when(qt < n_back)
    def _():
        s_lists = []
        for g, h in chains:
            q = q_split(g, h)
            s_lists.append([jnp.where((row_k + j * tk) <= (qt * tq + ql), _nt(kw_ref[0, g, j * tk:(j + 1) * tk, :], q), NEG_INF)
                            for j in range(n_back)])
        v_lists = [[vwT_ref[0, g, j] for j in range(n_back)] for g, h in chains]
        for (g, h), o in zip(chains, _softmax_pv(s_lists, v_lists)):
            ow_ref[g, :, lanes_of(h)] = o

    n_vis = jnp.minimum((qt + 1) * (tq // SLC_BLOCK), ns)
    sub = 8
    sub_io = lax.broadcasted_iota(jnp.int32, (sub, tq), 0)
    cnt_ref[...] = jnp.zeros_like(cnt_ref)
    for mb in range(ns // sub):
        @pl.when(mb * sub < n_vis)
        def _():
            for g in range(G):
                groups = [imp_ref[g, j * sub:(j + 1) * sub, :] for j in range(ns // sub)]
                counts = [cnt_ref[g, j * sub:(j + 1) * sub, :] for j in range(ns // sub)]
                for mi in range(sub):
                    row = jnp.broadcast_to(groups[mb][mi:mi + 1, :], (sub, tq))
                    for j in range(ns // sub):
                        if j < mb:
                            beats = row > groups[j]
                        elif j > mb:
                            beats = row >= groups[j]
                        else:
                            beats = (row > groups[j]) | ((row >= groups[j]) & (sub_io > mi))
                        counts[j] = counts[j] + beats.astype(F32)
                for j in range(ns // sub):
                    cnt_ref[g, j * sub:(j + 1) * sub, :] = counts[j]

    for g in range(G):
        bias = jnp.where(cnt_ref[g] < float(TOP_N), 0.0, NEG_INF)
        parts = [jnp.zeros((HEAD_DIM, tq), F32), bias]
        if ns < LANES - HEAD_DIM:
            parts.append(jnp.zeros((LANES - HEAD_DIM - ns, tq), F32))
        bias_t = jnp.concatenate(parts, axis=0).T.astype(BF16)
        qa_ref[g] = q_of(g) + jnp.concatenate([bias_t] * GROUP, axis=0)
        _flash_init(m_ref.at[g], acc_ref.at[g])

    slc_refs = [(m_ref.at[g, :, lanes_of(h)], acc_ref.at[g, :, lanes_of(h)]) for g, h in chains]

    s_bufs = (s0_ref, s1_ref)
    x_bufs = (x0_ref, x1_ref)

    def slc_store(kj, buf, masked, c):
        g, h = chains[c]
        k0 = pl.multiple_of(kj * tk, tk)
        s = _nt(ks_ref[0, g, pl.ds(k0, tk), :], qa_ref[g, lanes_of(h), :])
        if masked:
            s = jnp.where(causal, s, NEG_INF)
        s_bufs[buf][g, :, lanes_of(h)] = s
        x_bufs[buf][g, :, lanes_of(h)] = jnp.max(s, axis=0, keepdims=True)

    def slc_softmax(kj, buf, c):
        g, h = chains[c]
        return _chain_softmax([s_bufs[buf][g, :, lanes_of(h)]], slc_refs[c][0], x_bufs[buf][g, :, lanes_of(h)])

    def slc_values(kj, staged, c):
        _chain_values(staged, [vsT_ref[0, chains[c][0], kj]], slc_refs[c][1])

    _pipelined_sweep(qt, len(chains), slc_store, slc_softmax, slc_values)

    for g in range(G):
        def gate_row(j):
            return jnp.concatenate([g_ref[0, g, 3 * r + j:3 * r + j + 1, :] for r in range(GROUP)], axis=1)

        oT = gate_row(0) * oc_ref[g] + gate_row(1) * _normalized(acc_ref[g]) + gate_row(2) * ow_ref[g]
        for pr in range(GROUP // 2):
            st = jnp.concatenate([oT[:, (2 * pr) * tq:(2 * pr + 1) * tq], oT[:, (2 * pr + 1) * tq:(2 * pr + 2) * tq]], axis=0)
            col = (g * GROUP // 2 + pr) * LANES
            o_ref[0, :, col:col + LANES] = st.T.astype(BF16)


def _nsa(qn, kc, cmp_lhs, ks, vsT, kw, vwT, gT4, *, tq):
    B, H, T, _ = qn.shape
    G = N_KV_NSA
    tk = vsT.shape[-1]
    ncp = kc.shape[2]
    ns = cmp_lhs.shape[2] - V_ROWS
    assert tq == tk and WINDOW == tk and _NSA_LANE_SPLITS == GROUP and T >= WINDOW and ns <= LANES - HEAD_DIM and ns % 8 == 0
    nq = GROUP * tq
    kvspec = pl.BlockSpec((1, G, T, LANES), lambda b, i: (b, 0, 0, 0))
    vtspec = pl.BlockSpec((1, G, T // tk, V_ROWS, tk), lambda b, i: (b, 0, 0, 0, 0))
    return pl.pallas_call(
        functools.partial(_nsa_kernel, tq=tq, tk=tk),
        grid=(B, T // tq),
        in_specs=[
            pl.BlockSpec((1, H, tq, LANES), lambda b, i: (b, 0, i, 0)),
            pl.BlockSpec((1, G, ncp, LANES), lambda b, i: (b, 0, 0, 0)),
            pl.BlockSpec((1, G, V_ROWS + ns, ncp), lambda b, i: (b, 0, 0, 0)),
            kvspec, vtspec, kvspec, vtspec,
            pl.BlockSpec((1, G, 3 * GROUP, tq), lambda b, i: (b, 0, 0, i)),
        ],
        out_specs=pl.BlockSpec((1, tq, H * HEAD_DIM), lambda b, i: (b, i, 0)),
        out_shape=jax.ShapeDtypeStruct((B, T, H * HEAD_DIM), BF16),
        scratch_shapes=[
            pltpu.VMEM((G, 1, nq), F32),
            pltpu.VMEM((G, V_ROWS, nq), F32),
            pltpu.VMEM((G, ns, tq), F32),
            pltpu.VMEM((G, ns, tq), F32),
            pltpu.VMEM((G, nq, LANES), BF16),
            pltpu.VMEM((G, HEAD_DIM, nq), F32),
            pltpu.VMEM((G, HEAD_DIM, nq), F32),
            pltpu.VMEM((G, tk, nq), F32),
            pltpu.VMEM((G, tk, nq), F32),
            pltpu.VMEM((G, 1, nq), F32),
            pltpu.VMEM((G, 1, nq), F32),
        ],
        compiler_params=_params(("arbitrary", "arbitrary")),
        name="nsa",
    )(qn, kc, cmp_lhs, ks, vsT, kw, vwT, gT4)


def _mixout_kernel(x_ref, mod_ref, g_ref, on_ref, of_ref, wgm_ref, wun_ref, wuf_ref, wo_ref, o_ref):
    x = x_ref[0]
    D = x.shape[-1]
    sh = mod_ref[0, 3:4, :]
    sc = mod_ref[0, 4:5, :]
    ga = mod_ref[0, 5:6, :]
    h = _rms_mod(x, g_ref[...], sh, sc).astype(BF16)
    gm = _mm(h, wgm_ref[...])
    un = _mm(on_ref[0], wun_ref[...])
    uf = _mm(of_ref[0], wuf_ref[...])
    merged = jax.nn.sigmoid(gm[:, :D]) * un + jax.nn.sigmoid(gm[:, D:]) * uf
    y = _mm(merged.astype(BF16), wo_ref[...])
    o_ref[0] = x + ga * y


def _mixout(x, mod3, g, o_nsa, o_fox, wgm, wun, wuf, wo, *, tm):
    B, T, D = x.shape
    bt = lambda b, t: (b, t, 0)
    c2 = lambda b, t: (0, 0)
    dn = o_nsa.shape[-1]
    return pl.pallas_call(
        _mixout_kernel,
        grid=(B, T // tm),
        in_specs=[
            pl.BlockSpec((1, tm, D), bt),
            pl.BlockSpec((1, N_MOD, D), lambda b, t: (b, 0, 0)),
            pl.BlockSpec((1, D), c2),
            pl.BlockSpec((1, tm, dn), bt),
            pl.BlockSpec((1, tm, dn), bt),
            pl.BlockSpec((D, 2 * D), c2, pipeline_mode=pl.Buffered(1)),
            pl.BlockSpec((dn, D), c2, pipeline_mode=pl.Buffered(1)),
            pl.BlockSpec((dn, D), c2, pipeline_mode=pl.Buffered(1)),
            pl.BlockSpec((D, D), c2, pipeline_mode=pl.Buffered(1)),
        ],
        out_specs=pl.BlockSpec((1, tm, D), bt),
        out_shape=jax.ShapeDtypeStruct((B, T, D), F32),
        compiler_params=_params(("arbitrary", "arbitrary")),
        name="mixout",
    )(x, mod3, g.reshape(1, D), o_nsa, o_fox, wgm, wun, wuf, wo)


def _compress_weights(pe, w1, w2):
    half = CMP_BLOCK // 2
    eye = jnp.eye(N_KV_NSA, dtype=F32)

    def expand(w_half):
        w3 = w_half.reshape(half, HEAD_DIM, CMP_HIDDEN)
        return jnp.einsum('jdn,gh->jgdhn', w3, eye).reshape(half * N_KV_NSA * HEAD_DIM, N_KV_NSA * CMP_HIDDEN).astype(BF16)

    def pe_row(pe_half):
        return jnp.broadcast_to(pe_half[:, None, :], (half, N_KV_NSA, HEAD_DIM)).reshape(1, -1)

    w2p = jnp.pad(w2, ((0, 0), (0, LANES - HEAD_DIM))).astype(BF16)
    return (pe_row(pe[:half]), pe_row(pe[half:]), expand(w1[:half * HEAD_DIM]), expand(w1[half * HEAD_DIM:]), w2p)


def kernel(x, c, positions, w_ada, b_ada, g_ffn1, w_gate1, w_up1, w_down1, g_mix, w_in, b_forget, pe_ck, w1_ck, w2_ck, pe_cv, w1_cv, w2_cv, w_up_nsa, w_up_fox, w_o, g_ffn2, w_gate2, w_up2, w_down2, g_final):
    B, T, D = x.shape
    depth = w_ada.shape[0]
    tm = 512
    ffn_ck = 256
    nsa_tq, nsa_tk = 512, 512
    fox_tq, fox_tk = 512, 512
    n_slc = T // SLC_BLOCK
    n_sub = T // CMP_STRIDE

    half = ROPE_DIM // 2
    inv_freq = ROPE_THETA ** (-jnp.arange(half, dtype=F32) / half)
    invf = jnp.broadcast_to(inv_freq[:, None], (half, tm))
    cmp_start = np.arange(n_sub) * CMP_STRIDE
    slc_start = np.arange(n_slc) * SLC_BLOCK
    ov = ((cmp_start[:, None] < slc_start[None, :] + SLC_BLOCK) & (slc_start[None, :] < cmp_start[:, None] + CMP_BLOCK))
    ov[n_sub - CMP_BLOCK // CMP_STRIDE + 1:, :] = False
    ovT = jnp.asarray(ov.T, dtype=BF16)

    pos_row = positions.astype(F32)[:, None, :]
    sel_np = np.zeros((3, LANES, 2 * LANES), np.float32)
    oneq_np = np.zeros((N_HEADS_FOX, LANES), np.float32)
    onek_np = np.zeros((N_HEADS_FOX, LANES), np.float32)
    for hd in range(N_HEADS_FOX):
        for j in range(3):
            sel_np[j, _GATE_COLS + hd, _TAIL + 3 * hd + j] = 1.0
            sel_np[j, _GATE_COLS + hd, LANES + _TAIL2 + 3 * hd + j] = -1.0
            oneq_np[hd, _TAIL2 + 3 * hd + j] = 1.0
            onek_np[hd, _TAIL + 3 * hd + j] = 1.0
    sel, oneq, onek = jnp.asarray(sel_np, BF16), jnp.asarray(oneq_np), jnp.asarray(onek_np)
    tri = jnp.asarray(np.tril(np.ones((tm, tm), np.float32)), BF16)
    c_in = c
    for l in range(depth):
        mod3 = _ada(c_in, w_ada[l], b_ada[l]).reshape(B, N_MOD, D)
        wg1, wu1, wd1 = w_gate1[l].astype(BF16), w_up1[l].astype(BF16), w_down1[l].astype(BF16)
        wg2, wu2, wd2 = w_gate2[l].astype(BF16), w_up2[l].astype(BF16), w_down2[l].astype(BF16)
        wl = w_in[l]
        c0 = D_NSA + N_KV_STREAMS * D_KV
        c1 = c0 + _GATE_COLS
        c2 = c1 + 3 * D_FOX
        c3 = c2 + N_HEADS_FOX
        small = jnp.concatenate([wl[:, c0:c1], wl[:, c2:c3], jnp.zeros((D, LANES - _GATE_COLS - N_HEADS_FOX), F32)], axis=1)
        w_proj = jnp.concatenate([wl[:, :c0], wl[:, c1:c2], small], axis=1).astype(BF16)
        w_gm = wl[:, c3:].astype(BF16)
        bf_row = jnp.zeros((1, LANES), F32).at[0, _GATE_COLS:_GATE_COLS + N_HEADS_FOX].set(b_forget[l])

        x = _ffn(x, mod3, g_ffn1[l], wg1, wu1, wd1, g_final, k_mod=0, final=False, tm=tm, ck=ffn_ck)

        (qn, kcr, vcr, ks, vsT, kw, vwT, qf, kf, vfT, gT) = _inproj(
            x, mod3, g_mix[l], pos_row, invf, bf_row, sel, oneq, onek, tri, w_proj, tm=tm, nsa_tk=nsa_tk, fox_tk=fox_tk)

        pk_t, pk_b, wk_t, wk_b, w2k = _compress_weights(pe_ck[l], w1_ck[l], w2_ck[l])
        pv_t, pv_b, wv_t, wv_b, w2v = _compress_weights(pe_cv[l], w1_cv[l], w2_cv[l])
        pe4 = jnp.concatenate([pk_t, pk_b, pv_t, pv_b], axis=0)
        kc, cmp_lhs = _compress(kcr, vcr, pe4, wk_t, wk_b, wv_t, wv_b, w2k, w2v, ovT)

        o_fox = _fox(qf, kf, vfT, tq=fox_tq)
        gT4 = gT.reshape(B, N_KV_NSA, 3 * GROUP, T)
        o_nsa = _nsa(qn, kc, cmp_lhs, ks, vsT, kw, vwT, gT4, tq=nsa_tq)

        x = _mixout(x, mod3, g_mix[l], o_nsa, o_fox, w_gm, w_up_nsa[l].astype(BF16), w_up_fox[l].astype(BF16),
                    w_o[l].astype(BF16), tm=tm)
        last = l == depth - 1
        x = _ffn(x, mod3, g_ffn2[l], wg2, wu2, wd2, g_final, k_mod=6, final=last, tm=tm, ck=ffn_ck)
    return x
```

```python
import functools

import numpy as np
import jax
import jax.numpy as jnp
from jax import lax
from jax.experimental import pallas as pl
from jax.experimental.pallas import tpu as pltpu

HEAD_DIM = 64
N_HEADS_NSA = 8
N_KV_NSA = 2
GROUP = N_HEADS_NSA // N_KV_NSA
N_HEADS_FOX = 8
CMP_BLOCK = 32
CMP_STRIDE = 16
CMP_HIDDEN = 128
SLC_BLOCK = 64
TOP_N = 16
WINDOW = 512
ROPE_THETA = 500000.0
ROPE_DIM = HEAD_DIM // 4
N_MOD = 9
RMS_EPS = 1e-6
NEG_INF = -1e30
FORCE_SCORE = 1e4

LANES = 128
LOG2E = 1.4426950408889634
V_ROWS = HEAD_DIM + 16
F32 = jnp.float32
BF16 = jnp.bfloat16
VMEM_LIMIT = 56 * 1024 * 1024

D_NSA = N_HEADS_NSA * HEAD_DIM
D_KV = N_KV_NSA * HEAD_DIM
D_FOX = N_HEADS_FOX * HEAD_DIM
N_KV_STREAMS = 6
_GATE_COLS = 3 * N_HEADS_NSA
_OFF_QN = 0
_OFF_KV = _OFF_QN + D_NSA
_OFF_QF = _OFF_KV + N_KV_STREAMS * D_KV
_OFF_KF = _OFF_QF + D_FOX
_OFF_VF = _OFF_KF + D_FOX
_OFF_SMALL = _OFF_VF + D_FOX
_W_IN_COLS = _OFF_SMALL + LANES
_TAIL = HEAD_DIM
_TAIL2 = HEAD_DIM + 36


def _params(sem):
    return pltpu.CompilerParams(dimension_semantics=sem, vmem_limit_bytes=VMEM_LIMIT)


def _nt(a, b):
    return lax.dot_general(a, b, (((1,), (1,)), ((), ())), preferred_element_type=F32)


def _mm(a, b):
    return jnp.dot(a, b, preferred_element_type=F32)


def _split3(x):
    hi = x.astype(BF16)
    r = x - hi.astype(F32)
    mid = r.astype(BF16)
    lo = (r - mid.astype(F32)).astype(BF16)
    return hi, mid, lo


def _rms_mod(x, g, shift, scale):
    ms = jnp.mean(x * x, axis=-1, keepdims=True)
    y = x * lax.rsqrt(ms + RMS_EPS) * g
    return y * (1.0 + scale) + shift


def _ada_kernel(c_ref, w_ref, b_ref, o_ref):
    c = c_ref[...]
    ca = c * jax.nn.sigmoid(c)
    h0, h1, h2 = _split3(ca)
    w0, w1, w2 = _split3(w_ref[...])
    acc = _mm(h0, w0) + _mm(h0, w1) + _mm(h1, w0)
    acc = acc + (_mm(h1, w1) + _mm(h0, w2) + _mm(h2, w0))
    o_ref[...] = acc + b_ref[...]


def _ada(c, w_ada, b_ada):
    B, D = c.shape
    n = w_ada.shape[1]
    tn = 1024
    return pl.pallas_call(
        _ada_kernel,
        grid=(n // tn,),
        in_specs=[
            pl.BlockSpec((B, D), lambda j: (0, 0)),
            pl.BlockSpec((D, tn), lambda j: (0, j)),
            pl.BlockSpec((1, tn), lambda j: (0, j)),
        ],
        out_specs=pl.BlockSpec((B, tn), lambda j: (0, j)),
        out_shape=jax.ShapeDtypeStruct((B, n), F32),
        compiler_params=_params(("arbitrary",)),
        name="ada",
    )(c, w_ada, b_ada.reshape(1, n))


def _ffn_kernel(x_ref, mod_ref, g_ref, wg_ref, wu_ref, wd_ref, gf_ref, o_ref, h_ref, acc_ref, *, k_mod, final, ck):
    x = x_ref[0]
    sh = mod_ref[0, k_mod:k_mod + 1, :]
    sc = mod_ref[0, k_mod + 1:k_mod + 2, :]
    ga = mod_ref[0, k_mod + 2:k_mod + 3, :]
    h_ref[...] = _rms_mod(x, g_ref[...], sh, sc).astype(BF16)
    acc_ref[...] = jnp.zeros_like(acc_ref)

    n_chunks = wg_ref.shape[1] // ck
    h = h_ref[...]

    def gate_up(j):
        return _mm(h, wg_ref[:, j * ck:(j + 1) * ck]), _mm(h, wu_ref[:, j * ck:(j + 1) * ck])

    gu = gate_up(0)
    for j in range(n_chunks):
        g, u = gu
        if j + 1 < n_chunks:
            gu = gate_up(j + 1)
        a = (g * jax.nn.sigmoid(g) * u).astype(BF16)
        acc_ref[...] += _mm(a, wd_ref[j * ck:(j + 1) * ck, :])
    out = x + 0.5 * ga * acc_ref[...]
    if final:
        ms = jnp.mean(out * out, axis=-1, keepdims=True)
        out = out * lax.rsqrt(ms + RMS_EPS) * gf_ref[...]
    o_ref[0] = out


def _ffn(x, mod3, g, wg, wu, wd, g_final, *, k_mod, final, tm, ck):
    B, T, D = x.shape
    F = wg.shape[1]
    assert F % ck == 0
    const2 = lambda b, t: (0, 0)
    return pl.pallas_call(
        functools.partial(_ffn_kernel, k_mod=k_mod, final=final, ck=ck),
        grid=(B, T // tm),
        in_specs=[
            pl.BlockSpec((1, tm, D), lambda b, t: (b, t, 0)),
            pl.BlockSpec((1, N_MOD, D), lambda b, t: (b, 0, 0)),
            pl.BlockSpec((1, D), lambda b, t: (0, 0)),
            pl.BlockSpec((D, F), const2, pipeline_mode=pl.Buffered(1)),
            pl.BlockSpec((D, F), const2, pipeline_mode=pl.Buffered(1)),
            pl.BlockSpec((F, D), const2, pipeline_mode=pl.Buffered(1)),
            pl.BlockSpec((1, D), lambda b, t: (0, 0)),
        ],
        out_specs=pl.BlockSpec((1, tm, D), lambda b, t: (b, t, 0)),
        out_shape=jax.ShapeDtypeStruct((B, T, D), F32),
        scratch_shapes=[pltpu.VMEM((tm, D), BF16), pltpu.VMEM((tm, D), F32)],
        compiler_params=_params(("arbitrary", "arbitrary")),
        name="ffn_final" if final else "ffn",
    )(x, mod3, g.reshape(1, D), wg, wu, wd, g_final.reshape(1, D))


def _inproj_kernel(x_ref, mod_ref, g_ref, pos_ref, invf_ref, bf_ref, sel_ref, oneq_ref, onek_ref, tri_ref, w_ref,
                   qn_ref, kcr_ref, vcr_ref, ks_ref, vsT_ref, kw_ref, vwT_ref, qf_ref, kf_ref, vfT_ref, gT_ref,
                   h_ref, carry_ref, *, tm):
    t_idx = pl.program_id(1)
    sh = mod_ref[0, 3:4, :]
    sc = mod_ref[0, 4:5, :]
    h_ref[...] = _rms_mod(x_ref[0], g_ref[...], sh, sc).astype(BF16)
    h = h_ref[...]

    lane = lax.broadcasted_iota(jnp.int32, (tm, LANES), 1)
    low = lane < HEAD_DIM
    first8 = (lane & (HEAD_DIM - 1)) < (ROPE_DIM // 2)
    ang = invf_ref[...] * pos_ref[0]
    cos8 = jnp.cos(ang)
    sin8 = jnp.sin(ang)
    rest = HEAD_DIM - ROPE_DIM
    cos_t = jnp.concatenate([cos8, cos8, jnp.ones((rest, tm), F32)] * 2, axis=0).T
    sin_t = jnp.concatenate([-sin8, sin8, jnp.zeros((rest, tm), F32)] * 2, axis=0).T

    def rope(xs):
        partner = jnp.where(first8, pltpu.roll(xs, LANES - ROPE_DIM // 2, 1), pltpu.roll(xs, ROPE_DIM // 2, 1))
        return xs * cos_t + partner * sin_t

    def split_heads(xs):
        return jnp.where(low, xs, 0.0), jnp.where(low, pltpu.roll(xs, HEAD_DIM, 1), 0.0)

    def with_ones(vt):
        extra = (lax.broadcasted_iota(jnp.int32, (V_ROWS - HEAD_DIM, vt.shape[1]), 0) == 0).astype(BF16)
        return jnp.concatenate([vt, extra], axis=0)

    sm = _mm(h, w_ref[:, _OFF_SMALL:_OFF_SMALL + LANES])
    gT_ref[0] = jax.nn.sigmoid(sm).T[:_GATE_COLS, :]
    xl = sm + bf_ref[...]
    logf = jnp.minimum(xl, 0.0) - jnp.log1p(jnp.exp(-jnp.abs(xl)))
    tri = tri_ref[...]
    l0, l1, l2 = _split3(logf)

    @pl.when(t_idx == 0)
    def _():
        carry_ref[...] = jnp.zeros_like(carry_ref)

    cf = _mm(tri, l0) + _mm(tri, l1) + _mm(tri, l2) + carry_ref[...]
    carry_ref[...] = cf[tm - 1:tm, :]
    pieces = _split3(cf * LOG2E)
    placed = _mm(pieces[0], sel_ref[0]) + _mm(pieces[1], sel_ref[1]) + _mm(pieces[2], sel_ref[2])
    q_tail = placed[:, :LANES]
    k_tail = placed[:, LANES:]

    qn = _mm(h, w_ref[:, _OFF_QN:_OFF_QN + D_NSA])
    scale = HEAD_DIM ** -0.5 * LOG2E
    for j in range(D_NSA // LANES):
        a, b = split_heads(rope(qn[:, j * LANES:(j + 1) * LANES]) * scale)
        qn_ref[0, 2 * j] = a.astype(BF16)
        qn_ref[0, 2 * j + 1] = b.astype(BF16)

    nsa_tk = vsT_ref.shape[-1]
    kv = _mm(h, w_ref[:, _OFF_KV:_OFF_KV + N_KV_STREAMS * D_KV])
    kcr_ref[0] = rope(kv[:, 0:LANES])
    vcr_ref[0] = kv[:, LANES:2 * LANES]
    blk = lax.shift_right_logical(t_idx * tm + lax.broadcasted_iota(jnp.int32, (tm, LANES), 0), SLC_BLOCK.bit_length() - 1)
    onehot = (lane == blk + HEAD_DIM).astype(F32)
    for k_out, v_out, off, tail in ((ks_ref, vsT_ref, 2 * LANES, onehot), (kw_ref, vwT_ref, 4 * LANES, None)):
        a, b = split_heads(rope(kv[:, off:off + LANES]))
        if tail is not None:
            a, b = a + tail, b + tail
        k_out[0, 0] = a.astype(BF16)
        k_out[0, 1] = b.astype(BF16)
        vT = kv[:, off + LANES:off + 2 * LANES].T.astype(BF16)
        for gi in range(N_KV_NSA):
            for c in range(tm // nsa_tk):
                v_out[0, gi, c] = with_ones(vT[gi * HEAD_DIM:(gi + 1) * HEAD_DIM, c * nsa_tk:(c + 1) * nsa_tk])

    qf = _mm(h, w_ref[:, _OFF_QF:_OFF_QF + D_FOX])
    kf = _mm(h, w_ref[:, _OFF_KF:_OFF_KF + D_FOX])
    for j in range(D_FOX // LANES):
        qa, qb = split_heads(qf[:, j * LANES:(j + 1) * LANES] * scale)
        ka, kb = split_heads(kf[:, j * LANES:(j + 1) * LANES])
        for hd, qh, kh in ((2 * j, qa, ka), (2 * j + 1, qb, kb)):
            qf_ref[0, hd] = (qh + q_tail + oneq_ref[hd:hd + 1, :]).astype(BF16)
            kf_ref[0, hd] = (kh + k_tail + onek_ref[hd:hd + 1, :]).astype(BF16)
    vf = _mm(h, w_ref[:, _OFF_VF:_OFF_VF + D_FOX])
    fox_tk = vfT_ref.shape[-1]
    for j in range(D_FOX // LANES):
        vT = vf[:, j * LANES:(j + 1) * LANES].T.astype(BF16)
        for hh in range(2):
            for c in range(tm // fox_tk):
                vfT_ref[0, 2 * j + hh, c] = with_ones(vT[hh * HEAD_DIM:(hh + 1) * HEAD_DIM, c * fox_tk:(c + 1) * fox_tk])


def _inproj(x, mod3, g, pos_row, invf, bf_row, sel, oneq, onek, tri, w, *, tm, nsa_tk, fox_tk):
    B, T, D = x.shape
    H, G = N_HEADS_NSA, N_KV_NSA
    bt = lambda b, t: (b, t, 0)
    hb = lambda b, t: (b, 0, t, 0)
    vb = lambda b, t: (b, 0, t, 0, 0)
    out_shape = [
        jax.ShapeDtypeStruct((B, H, T, LANES), BF16),
        jax.ShapeDtypeStruct((B, T, LANES), F32),
        jax.ShapeDtypeStruct((B, T, LANES), F32),
        jax.ShapeDtypeStruct((B, G, T, LANES), BF16),
        jax.ShapeDtypeStruct((B, G, T // nsa_tk, V_ROWS, nsa_tk), BF16),
        jax.ShapeDtypeStruct((B, G, T, LANES), BF16),
        jax.ShapeDtypeStruct((B, G, T // nsa_tk, V_ROWS, nsa_tk), BF16),
        jax.ShapeDtypeStruct((B, N_HEADS_FOX, T, LANES), BF16),
        jax.ShapeDtypeStruct((B, N_HEADS_FOX, T, LANES), BF16),
        jax.ShapeDtypeStruct((B, N_HEADS_FOX, T // fox_tk, V_ROWS, fox_tk), BF16),
        jax.ShapeDtypeStruct((B, _GATE_COLS, T), F32),
    ]
    out_specs = [
        pl.BlockSpec((1, H, tm, LANES), hb),
        pl.BlockSpec((1, tm, LANES), bt),
        pl.BlockSpec((1, tm, LANES), bt),
        pl.BlockSpec((1, G, tm, LANES), hb),
        pl.BlockSpec((1, G, tm // nsa_tk, V_ROWS, nsa_tk), vb),
        pl.BlockSpec((1, G, tm, LANES), hb),
        pl.BlockSpec((1, G, tm // nsa_tk, V_ROWS, nsa_tk), vb),
        pl.BlockSpec((1, N_HEADS_FOX, tm, LANES), hb),
        pl.BlockSpec((1, N_HEADS_FOX, tm, LANES), hb),
        pl.BlockSpec((1, N_HEADS_FOX, tm // fox_tk, V_ROWS, fox_tk), vb),
        pl.BlockSpec((1, _GATE_COLS, tm), lambda b, t: (b, 0, t)),
    ]
    row = lambda b, t: (0, 0)
    return pl.pallas_call(
        functools.partial(_inproj_kernel, tm=tm),
        grid=(B, T // tm),
        in_specs=[
            pl.BlockSpec((1, tm, D), bt),
            pl.BlockSpec((1, N_MOD, D), lambda b, t: (b, 0, 0)),
            pl.BlockSpec((1, D), row),
            pl.BlockSpec((1, 1, tm), lambda b, t: (b, 0, t)),
            pl.BlockSpec((ROPE_DIM // 2, tm), row),
            pl.BlockSpec((1, LANES), row),
            pl.BlockSpec((3, LANES, 2 * LANES), lambda b, t: (0, 0, 0)),
            pl.BlockSpec((N_HEADS_FOX, LANES), row),
            pl.BlockSpec((N_HEADS_FOX, LANES), row),
            pl.BlockSpec((tm, tm), row),
            pl.BlockSpec((D, _W_IN_COLS), row, pipeline_mode=pl.Buffered(1)),
        ],
        out_specs=out_specs,
        out_shape=out_shape,
        scratch_shapes=[pltpu.VMEM((tm, D), BF16), pltpu.VMEM((1, LANES), F32)],
        compiler_params=_params(("arbitrary", "arbitrary")),
        name="inproj",
    )(x, mod3, g.reshape(1, D), pos_row, invf, bf_row, sel, oneq, onek, tri, w)


def _gelu_tanh(x):
    c = np.float32(np.sqrt(2.0 / np.pi))
    return x * (0.5 * (1.0 + jnp.tanh(c * (x + 0.044715 * (x * x * x)))))


def _compress_kernel(zk_ref, zv_ref, pe_ref, wkt_ref, wkb_ref, wvt_ref, wvb_ref, w2k_ref, w2v_ref, ovT_ref, kc_ref, lhs_ref):
    nsub = zk_ref.shape[1] // CMP_STRIDE

    def mlp(z_ref, pe_top, pe_bot, wt_ref, wb_ref):
        a = b = None
        for j in range(CMP_STRIDE):
            xj = z_ref[0, pl.ds(j, nsub, stride=CMP_STRIDE), :]
            lanes = slice(j * LANES, (j + 1) * LANES)
            aj = _mm((xj + pe_ref[pe_top:pe_top + 1, lanes]).astype(BF16), wt_ref[lanes, :])
            bj = _mm((xj + pe_ref[pe_bot:pe_bot + 1, lanes]).astype(BF16), wb_ref[lanes, :])
            a = aj if a is None else a + aj
            b = bj if b is None else b + bj
        return _gelu_tanh(a + pltpu.roll(b, nsub - 1, 0))

    hk = mlp(zk_ref, 0, 1, wkt_ref, wkb_ref)
    hv = mlp(zv_ref, 2, 3, wvt_ref, wvb_ref)
    for gi in range(N_KV_NSA):
        kc_ref[0, gi] = _mm(hk[:, gi * CMP_HIDDEN:(gi + 1) * CMP_HIDDEN].astype(BF16), w2k_ref[...]).astype(BF16)
        vc = _mm(hv[:, gi * CMP_HIDDEN:(gi + 1) * CMP_HIDDEN].astype(BF16), w2v_ref[...])
        ones = (lax.broadcasted_iota(jnp.int32, (V_ROWS - HEAD_DIM, nsub), 0) == 0).astype(BF16)
        lhs_ref[0, gi] = jnp.concatenate([vc.T[:HEAD_DIM, :].astype(BF16), ones, ovT_ref[...]], axis=0)


def _compress(zk, zv, pe4, wkt, wkb, wvt, wvb, w2k, w2v, ovT):
    B, T, _ = zk.shape
    nsub = T // CMP_STRIDE
    zc = CMP_STRIDE * LANES
    G = N_KV_NSA
    c2 = lambda b: (0, 0)
    zspec = pl.BlockSpec((1, T, LANES), lambda b: (b, 0, 0))
    wspec = pl.BlockSpec((zc, G * CMP_HIDDEN), c2)
    w2spec = pl.BlockSpec((CMP_HIDDEN, LANES), c2)
    return pl.pallas_call(
        _compress_kernel,
        grid=(B,),
        in_specs=[zspec, zspec, pl.BlockSpec((4, zc), c2), wspec, wspec, wspec, wspec, w2spec, w2spec,
                  pl.BlockSpec(ovT.shape, c2)],
        out_specs=[
            pl.BlockSpec((1, G, nsub, LANES), lambda b: (b, 0, 0, 0)),
            pl.BlockSpec((1, G, V_ROWS + ovT.shape[0], nsub), lambda b: (b, 0, 0, 0)),
        ],
        out_shape=[
            jax.ShapeDtypeStruct((B, G, nsub, LANES), BF16),
            jax.ShapeDtypeStruct((B, G, V_ROWS + ovT.shape[0], nsub), BF16),
        ],
        compiler_params=_params(("arbitrary",)),
        name="compress",
    )(zk, zv, pe4, wkt, wkb, wvt, wvb, w2k, w2v, ovT)


def _flash_init(m_ref, acc_ref):
    m_ref[...] = jnp.full_like(m_ref, NEG_INF)
    acc_ref[...] = jnp.zeros_like(acc_ref)


def _chain_softmax(s_list, m_ref, tile_max=None):
    m_old = m_ref[...]
    m_new = m_old
    if tile_max is not None:
        m_new = jnp.maximum(m_new, tile_max)
    else:
        for s in s_list:
            m_new = jnp.maximum(m_new, jnp.max(s, axis=0, keepdims=True))
    m_ref[...] = m_new
    return jnp.exp2(m_old - m_new), [jnp.exp2(s - m_new).astype(BF16) for s in s_list]


def _chain_values(staged, v_list, acc_ref):
    alpha, ps = staged
    pv = _mm(v_list[0], ps[0])
    for v, p in zip(v_list[1:], ps[1:]):
        pv = pv + _mm(v, p)
    acc_ref[...] = alpha * acc_ref[...] + pv


def _normalized(acc):
    return acc[:HEAD_DIM] / acc[HEAD_DIM:HEAD_DIM + 1]


def _pipelined_sweep(n, n_chains, store, softmax, values):
    chains = range(n_chains)

    def by_parity(i, fn):
        @pl.when((i & 1) == 0)
        def _():
            fn(0)

        @pl.when((i & 1) == 1)
        def _():
            fn(1)

    def step(nxt, masked, cur, par):
        for c in chains:
            if nxt is not None:
                store(nxt, 1 - par, masked, c)
            values(cur, softmax(cur, par, c), c)

    def first_scores(masked):
        for c in chains:
            store(0, 0, masked, c)

    @pl.when(n == 0)
    def _():
        first_scores(True)
        step(None, False, 0, 0)

    @pl.when(n == 1)
    def _():
        first_scores(False)
        step(1, True, 0, 0)
        step(None, False, 1, 1)

    @pl.when(n >= 2)
    def _():
        first_scores(False)
        step(1, False, 0, 0)
        n_mid = n - 2

        def body(j, carry):
            step(2 * j + 2, False, 2 * j + 1, 1)
            step(2 * j + 3, False, 2 * j + 2, 0)
            return carry

        lax.fori_loop(0, n_mid // 2, body, 0)

        @pl.when((n_mid & 1) == 1)
        def _():
            step(n_mid + 1, False, n_mid, 1)

        def last_two(par):
            step(n, True, n - 1, par)
            step(None, False, n, 1 - par)

        by_parity(n - 1, last_two)


_FOX_HEADS_PER_STEP = 8


def _fox_kernel(q_ref, k_ref, vT_ref, o_ref, m_ref, acc_ref, s0_ref, s1_ref, x0_ref, x1_ref, *, tq, tk):
    qi = pl.program_id(2)
    nkc = tq // tk
    heads = range(_FOX_HEADS_PER_STEP)
    s_bufs = (s0_ref, s1_ref)
    x_bufs = (x0_ref, x1_ref)
    row = lax.broadcasted_iota(jnp.int32, (tk, tq), 0)
    lane = lax.broadcasted_iota(jnp.int32, (tk, tq), 1)
    for hh in heads:
        _flash_init(m_ref.at[hh], acc_ref.at[hh])

    def store(kt, buf, masked, hh):
        tile_max = None
        for c in range(nkc):
            k0 = pl.multiple_of(kt * tq + c * tk, tk)
            s = _nt(k_ref[0, hh, pl.ds(k0, tk), :], q_ref[0, hh])
            if masked:
                s = jnp.where(row + c * tk <= lane, s, NEG_INF)
            s_bufs[buf][hh, c * tk:(c + 1) * tk, :] = s
            cm = jnp.max(s, axis=0, keepdims=True)
            tile_max = cm if tile_max is None else jnp.maximum(tile_max, cm)
        x_bufs[buf][hh] = tile_max

    def softmax(kt, buf, hh):
        return _chain_softmax([s_bufs[buf][hh, c * tk:(c + 1) * tk, :] for c in range(nkc)], m_ref.at[hh], x_bufs[buf][hh])

    def values(kt, staged, hh):
        _chain_values(staged, [vT_ref[0, hh, kt * nkc + c] for c in range(nkc)], acc_ref.at[hh])

    _pipelined_sweep(qi, len(heads), store, softmax, values)
    for pr in range(_FOX_HEADS_PER_STEP // 2):
        st = jnp.concatenate([_normalized(acc_ref[2 * pr]), _normalized(acc_ref[2 * pr + 1])], axis=0)
        o_ref[0, :, pr * LANES:(pr + 1) * LANES] = st.T.astype(BF16)


def _fox(qf, kf, vfT, *, tq):
    B, H, T, _ = qf.shape
    tk = vfT.shape[-1]
    nh = _FOX_HEADS_PER_STEP
    return pl.pallas_call(
        functools.partial(_fox_kernel, tq=tq, tk=tk),
        grid=(B, H // nh, T // tq),
        in_specs=[
            pl.BlockSpec((1, nh, tq, LANES), lambda b, p, i: (b, p, i, 0)),
            pl.BlockSpec((1, nh, T, LANES), lambda b, p, i: (b, p, 0, 0)),
            pl.BlockSpec((1, nh, T // tk, V_ROWS, tk), lambda b, p, i: (b, p, 0, 0, 0)),
        ],
        out_specs=pl.BlockSpec((1, tq, nh * HEAD_DIM), lambda b, p, i: (b, i, p)),
        out_shape=jax.ShapeDtypeStruct((B, T, H * HEAD_DIM), BF16),
        scratch_shapes=[
            pltpu.VMEM((nh, 1, tq), F32),
            pltpu.VMEM((nh, V_ROWS, tq), F32),
            pltpu.VMEM((nh, tq, tq), F32),
            pltpu.VMEM((nh, tq, tq), F32),
            pltpu.VMEM((nh, 1, tq), F32),
            pltpu.VMEM((nh, 1, tq), F32),
        ],
        compiler_params=_params(("arbitrary", "arbitrary", "arbitrary")),
        name="fox",
    )(qf, kf, vfT)


def _softmax_pv(s_lists, v_lists):
    staged = []
    for s_list in s_lists:
        m = jnp.max(s_list[0], axis=0, keepdims=True)
        for s in s_list[1:]:
            m = jnp.maximum(m, jnp.max(s, axis=0, keepdims=True))
        staged.append([jnp.exp2(s - m).astype(BF16) for s in s_list])
    outs = []
    for ps, v_list in zip(staged, v_lists):
        pv = _mm(v_list[0], ps[0])
        for v, p in zip(v_list[1:], ps[1:]):
            pv = pv + _mm(v, p)
        outs.append(_normalized(pv))
    return outs


_NSA_LANE_SPLITS = GROUP


def _nsa_kernel(q_ref, kc_ref, cmp_lhs_ref, ks_ref, vsT_ref, kw_ref, vwT_ref, g_ref, o_ref,
                m_ref, acc_ref, imp_ref, cnt_ref, qa_ref, oc_ref, ow_ref, s0_ref, s1_ref, x0_ref, x1_ref, *, tq, tk):
    qt = pl.program_id(1)
    G = N_KV_NSA
    nq = GROUP * tq
    hw = nq // _NSA_LANE_SPLITS
    heads_per_split = GROUP // _NSA_LANE_SPLITS
    ncp = kc_ref.shape[2]
    ns = cmp_lhs_ref.shape[2] - V_ROWS
    blk_shift = SLC_BLOCK.bit_length() - 1
    t_row = qt * tq + (lax.broadcasted_iota(jnp.int32, (1, nq), 1) & (tq - 1))
    ql = lax.broadcasted_iota(jnp.int32, (1, hw), 1) & (tq - 1)
    row_k = lax.broadcasted_iota(jnp.int32, (tk, hw), 0)
    causal = row_k <= ql
    n_io = lax.broadcasted_iota(jnp.int32, (ns, tq), 0)
    cur = lax.shift_right_logical(qt * tq + lax.broadcasted_iota(jnp.int32, (ns, tq), 1), blk_shift)
    forced = (n_io == 0) | (n_io == cur) | (n_io == cur - 1)
    visible = n_io <= cur
    c_io = lax.broadcasted_iota(jnp.int32, (ncp, nq), 0)
    cmp_valid = (c_io * CMP_STRIDE + (CMP_BLOCK - 1)) <= t_row
    chains = [(g, h) for g in range(G) for h in range(_NSA_LANE_SPLITS)]

    def q_of(g):
        return q_ref[0, g * GROUP:(g + 1) * GROUP].reshape(nq, LANES)

    def q_split(g, h):
        h0 = g * GROUP + h * heads_per_split
        return q_ref[0, h0:h0 + heads_per_split].reshape(hw, LANES)

    def lanes_of(h):
        return pl.ds(h * hw, hw)

    cmp_ok = cmp_valid[:, :hw]
    s_cmp = [_nt(kc_ref[0, g], q_split(g, h)) for g, h in chains]
    p_cmp = []
    for sc in s_cmp:
        smk = jnp.where(cmp_ok, sc, NEG_INF)
        p_cmp.append(jnp.exp2(smk - jnp.max(smk, axis=0, keepdims=True)).astype(BF16))
    any_valid = t_row[:, :hw] >= (CMP_BLOCK - 1)
    imp_sum = [None] * G
    for (g, h), pc in zip(chains, p_cmp):
        r_all = _mm(cmp_lhs_ref[0, g], pc)
        inv = jnp.where(any_valid, 1.0 / r_all[HEAD_DIM:HEAD_DIM + 1], 0.0)
        oc_ref[g, :, lanes_of(h)] = r_all[:HEAD_DIM] * inv
        imp_h = r_all[V_ROWS:] * inv
        imp_sum[g] = imp_h if imp_sum[g] is None else imp_sum[g] + imp_h
    for g in range(G):
        imp_ref[g] = jnp.where(forced, FORCE_SCORE, jnp.where(visible, imp_sum[g], -1.0))

    n_back = WINDOW // tk
    hq = tq // 2
    row_h = lax.broadcasted_iota(jnp.int32, (hq, hq), 0)
    lane_h = lax.broadcasted_iota(jnp.int32, (hq, hq), 1)

    @pl.when(qt >= n_back)
    def _():
        base = pl.multiple_of((qt - 1) * tk, tk)
        s_lists, v_lists, where_to = [], [], []
        for g, h in chains:
            for sub in range(2):
                q = q_ref[0, g * GROUP + h, sub * hq:(sub + 1) * hq, :]
                s3 = _nt(kw_ref[0, g, pl.ds(base + sub * hq, 3 * hq), :], q)
                s_lists.append([jnp.where(row_h > lane_h, s3[:hq], NEG_INF),
                                s3[hq:2 * hq],
                                jnp.where(row_h <= lane_h, s3[2 * hq:], NEG_INF)])
                halves = [vwT_ref[0, g, qt - 1, :, :hq], vwT_ref[0, g, qt - 1, :, hq:],
                          vwT_ref[0, g, qt, :, :hq], vwT_ref[0, g, qt, :, hq:]]
                v_lists.append(halves[sub:sub + 3])
                where_to.append((g, pl.ds(h * tq + sub * hq, hq)))
        for (g, lanes), o in zip(where_to, _softmax_pv(s_lists, v_lists)):
            ow_ref[g, :, lanes] = o

    @pl.when(qt < n_back)
    def _():
        s_lists = []
        for g, h in chains:
            q = q_split(g, h)
            s_lists.append([jnp.where((row_k + j * tk) <= (qt * tq + ql), _nt(kw_ref[0, g, j * tk:(j + 1) * tk, :], q), NEG_INF)
                            for j in range(n_back)])
        v_lists = [[vwT_ref[0, g, j] for j in range(n_back)] for g, h in chains]
        for (g, h), o in zip(chains, _softmax_pv(s_lists, v_lists)):
            ow_ref[g, :, lanes_of(h)] = o

    n_vis = jnp.minimum((qt + 1) * (tq // SLC_BLOCK), ns)
    sub = 8
    sub_io = lax.broadcasted_iota(jnp.int32, (sub, tq), 0)
    cnt_ref[...] = jnp.zeros_like(cnt_ref)
    for mb in range(ns // sub):
        @pl.when(mb * sub < n_vis)
        def _():
            for g in range(G):
                groups = [imp_ref[g, j * sub:(j + 1) * sub, :] for j in range(ns // sub)]
                counts = [cnt_ref[g, j * sub:(j + 1) * sub, :] for j in range(ns // sub)]
                for mi in range(sub):
                    row = jnp.broadcast_to(groups[mb][mi:mi + 1, :], (sub, tq))
                    for j in range(ns // sub):
                        if j < mb:
                            beats = row > groups[j]
                        elif j > mb:
                            beats = row >= groups[j]
                        else:
                            beats = (row > groups[j]) | ((row >= groups[j]) & (sub_io > mi))
                        counts[j] = counts[j] + beats.astype(F32)
                for j in range(ns // sub):
                    cnt_ref[g, j * sub:(j + 1) * sub, :] = counts[j]

    for g in range(G):
        bias = jnp.where(cnt_ref[g] < float(TOP_N), 0.0, NEG_INF)
        parts = [jnp.zeros((HEAD_DIM, tq), F32), bias]
        if ns < LANES - HEAD_DIM:
            parts.append(jnp.zeros((LANES - HEAD_DIM - ns, tq), F32))
        bias_t = jnp.concatenate(parts, axis=0).T.astype(BF16)
        qa_ref[g] = q_of(g) + jnp.concatenate([bias_t] * GROUP, axis=0)
        _flash_init(m_ref.at[g], acc_ref.at[g])

    slc_refs = [(m_ref.at[g, :, lanes_of(h)], acc_ref.at[g, :, lanes_of(h)]) for g, h in chains]

    s_bufs = (s0_ref, s1_ref)
    x_bufs = (x0_ref, x1_ref)

    def slc_store(kj, buf, masked, c):
        g, h = chains[c]
        k0 = pl.multiple_of(kj * tk, tk)
        s = _nt(ks_ref[0, g, pl.ds(k0, tk), :], qa_ref[g, lanes_of(h), :])
        if masked:
            s = jnp.where(causal, s, NEG_INF)
        s_bufs[buf][g, :, lanes_of(h)] = s
        x_bufs[buf][g, :, lanes_of(h)] = jnp.max(s, axis=0, keepdims=True)

    def slc_softmax(kj, buf, c):
        g, h = chains[c]
        return _chain_softmax([s_bufs[buf][g, :, lanes_of(h)]], slc_refs[c][0], x_bufs[buf][g, :, lanes_of(h)])

    def slc_values(kj, staged, c):
        _chain_values(staged, [vsT_ref[0, chains[c][0], kj]], slc_refs[c][1])

    _pipelined_sweep(qt, len(chains), slc_store, slc_softmax, slc_values)

    for g in range(G):
        def gate_row(j):
            return jnp.concatenate([g_ref[0, g, 3 * r + j:3 * r + j + 1, :] for r in range(GROUP)], axis=1)

        oT = gate_row(0) * oc_ref[g] + gate_row(1) * _normalized(acc_ref[g]) + gate_row(2) * ow_ref[g]
        for pr in range(GROUP // 2):
            st = jnp.concatenate([oT[:, (2 * pr) * tq:(2 * pr + 1) * tq], oT[:, (2 * pr + 1) * tq:(2 * pr + 2) * tq]], axis=0)
            col = (g * GROUP // 2 + pr) * LANES
            o_ref[0, :, col:col + LANES] = st.T.astype(BF16)


def _nsa(qn, kc, cmp_lhs, ks, vsT, kw, vwT, gT4, *, tq):
    B, H, T, _ = qn.shape
    G = N_KV_NSA
    tk = vsT.shape[-1]
    ncp = kc.shape[2]
    ns = cmp_lhs.shape[2] - V_ROWS
    assert tq == tk and WINDOW == tk and _NSA_LANE_SPLITS == GROUP and T >= WINDOW and ns <= LANES - HEAD_DIM and ns % 8 == 0
    nq = GROUP * tq
    kvspec = pl.BlockSpec((1, G, T, LANES), lambda b, i: (b, 0, 0, 0))
    vtspec = pl.BlockSpec((1, G, T // tk, V_ROWS, tk), lambda b, i: (b, 0, 0, 0, 0))
    return pl.pallas_call(
        functools.partial(_nsa_kernel, tq=tq, tk=tk),
        grid=(B, T // tq),
        in_specs=[
            pl.BlockSpec((1, H, tq, LANES), lambda b, i: (b, 0, i, 0)),
            pl.BlockSpec((1, G, ncp, LANES), lambda b, i: (b, 0, 0, 0)),
            pl.BlockSpec((1, G, V_ROWS + ns, ncp), lambda b, i: (b, 0, 0, 0)),
            kvspec, vtspec, kvspec, vtspec,
            pl.BlockSpec((1, G, 3 * GROUP, tq), lambda b, i: (b, 0, 0, i)),
        ],
        out_specs=pl.BlockSpec((1, tq, H * HEAD_DIM), lambda b, i: (b, i, 0)),
        out_shape=jax.ShapeDtypeStruct((B, T, H * HEAD_DIM), BF16),
        scratch_shapes=[
            pltpu.VMEM((G, 1, nq), F32),
            pltpu.VMEM((G, V_ROWS, nq), F32),
            pltpu.VMEM((G, ns, tq), F32),
            pltpu.VMEM((G, ns, tq), F32),
            pltpu.VMEM((G, nq, LANES), BF16),
            pltpu.VMEM((G, HEAD_DIM, nq), F32),
            pltpu.VMEM((G, HEAD_DIM, nq), F32),
            pltpu.VMEM((G, tk, nq), F32),
            pltpu.VMEM((G, tk, nq), F32),
            pltpu.VMEM((G, 1, nq), F32),
            pltpu.VMEM((G, 1, nq), F32),
        ],
        compiler_params=_params(("arbitrary", "arbitrary")),
        name="nsa",
    )(qn, kc, cmp_lhs, ks, vsT, kw, vwT, gT4)


def _mixout_kernel(x_ref, mod_ref, g_ref, on_ref, of_ref, wgm_ref, wun_ref, wuf_ref, wo_ref, o_ref):
    x = x_ref[0]
    D = x.shape[-1]
    sh = mod_ref[0, 3:4, :]
    sc = mod_ref[0, 4:5, :]
    ga = mod_ref[0, 5:6, :]
    h = _rms_mod(x, g_ref[...], sh, sc).astype(BF16)
    gm = _mm(h, wgm_ref[...])
    un = _mm(on_ref[0], wun_ref[...])
    uf = _mm(of_ref[0], wuf_ref[...])
    merged = jax.nn.sigmoid(gm[:, :D]) * un + jax.nn.sigmoid(gm[:, D:]) * uf
    y = _mm(merged.astype(BF16), wo_ref[...])
    o_ref[0] = x + ga * y


def _mixout(x, mod3, g, o_nsa, o_fox, wgm, wun, wuf, wo, *, tm):
    B, T, D = x.shape
    bt = lambda b, t: (b, t, 0)
    c2 = lambda b, t: (0, 0)
    dn = o_nsa.shape[-1]
    return pl.pallas_call(
        _mixout_kernel,
        grid=(B, T // tm),
        in_specs=[
            pl.BlockSpec((1, tm, D), bt),
            pl.BlockSpec((1, N_MOD, D), lambda b, t: (b, 0, 0)),
            pl.BlockSpec((1, D), c2),
            pl.BlockSpec((1, tm, dn), bt),
            pl.BlockSpec((1, tm, dn), bt),
            pl.BlockSpec((D, 2 * D), c2, pipeline_mode=pl.Buffered(1)),
            pl.BlockSpec((dn, D), c2, pipeline_mode=pl.Buffered(1)),
            pl.BlockSpec((dn, D), c2, pipeline_mode=pl.Buffered(1)),
            pl.BlockSpec((D, D), c2, pipeline_mode=pl.Buffered(1)),
        ],
        out_specs=pl.BlockSpec((1, tm, D), bt),
        out_shape=jax.ShapeDtypeStruct((B, T, D), F32),
        compiler_params=_params(("arbitrary", "arbitrary")),
        name="mixout",
    )(x, mod3, g.reshape(1, D), o_nsa, o_fox, wgm, wun, wuf, wo)


def _compress_weights(pe, w1, w2):
    half = CMP_BLOCK // 2
    eye = jnp.eye(N_KV_NSA, dtype=F32)

    def expand(w_half):
        w3 = w_half.reshape(half, HEAD_DIM, CMP_HIDDEN)
        return jnp.einsum('jdn,gh->jgdhn', w3, eye).reshape(half * N_KV_NSA * HEAD_DIM, N_KV_NSA * CMP_HIDDEN).astype(BF16)

    def pe_row(pe_half):
        return jnp.broadcast_to(pe_half[:, None, :], (half, N_KV_NSA, HEAD_DIM)).reshape(1, -1)

    w2p = jnp.pad(w2, ((0, 0), (0, LANES - HEAD_DIM))).astype(BF16)
    return (pe_row(pe[:half]), pe_row(pe[half:]), expand(w1[:half * HEAD_DIM]), expand(w1[half * HEAD_DIM:]), w2p)


def kernel(x, c, positions, w_ada, b_ada, g_ffn1, w_gate1, w_up1, w_down1, g_mix, w_in, b_forget, pe_ck, w1_ck, w2_ck, pe_cv, w1_cv, w2_cv, w_up_nsa, w_up_fox, w_o, g_ffn2, w_gate2, w_up2, w_down2, g_final):
    B, T, D = x.shape
    depth = w_ada.shape[0]
    tm = 512
    ffn_ck = 256
    nsa_tq, nsa_tk = 512, 512
    fox_tq, fox_tk = 512, 512
    n_slc = T // SLC_BLOCK
    n_sub = T // CMP_STRIDE

    half = ROPE_DIM // 2
    inv_freq = ROPE_THETA ** (-jnp.arange(half, dtype=F32) / half)
    invf = jnp.broadcast_to(inv_freq[:, None], (half, tm))
    cmp_start = np.arange(n_sub) * CMP_STRIDE
    slc_start = np.arange(n_slc) * SLC_BLOCK
    ov = ((cmp_start[:, None] < slc_start[None, :] + SLC_BLOCK) & (slc_start[None, :] < cmp_start[:, None] + CMP_BLOCK))
    ov[n_sub - CMP_BLOCK // CMP_STRIDE + 1:, :] = False
    ovT = jnp.asarray(ov.T, dtype=BF16)

    pos_row = positions.astype(F32)[:, None, :]
    sel_np = np.zeros((3, LANES, 2 * LANES), np.float32)
    oneq_np = np.zeros((N_HEADS_FOX, LANES), np.float32)
    onek_np = np.zeros((N_HEADS_FOX, LANES), np.float32)
    for hd in range(N_HEADS_FOX):
        for j in range(3):
            sel_np[j, _GATE_COLS + hd, _TAIL + 3 * hd + j] = 1.0
            sel_np[j, _GATE_COLS + hd, LANES + _TAIL2 + 3 * hd + j] = -1.0
            oneq_np[hd, _TAIL2 + 3 * hd + j] = 1.0
            onek_np[hd, _TAIL + 3 * hd + j] = 1.0
    sel, oneq, onek = jnp.asarray(sel_np, BF16), jnp.asarray(oneq_np), jnp.asarray(onek_np)
    tri = jnp.asarray(np.tril(np.ones((tm, tm), np.float32)), BF16)
    c_in = c
    for l in range(depth):
        mod3 = _ada(c_in, w_ada[l], b_ada[l]).reshape(B, N_MOD, D)
        wg1, wu1, wd1 = w_gate1[l].astype(BF16), w_up1[l].astype(BF16), w_down1[l].astype(BF16)
        wg2, wu2, wd2 = w_gate2[l].astype(BF16), w_up2[l].astype(BF16), w_down2[l].astype(BF16)
        wl = w_in[l]
        c0 = D_NSA + N_KV_STREAMS * D_KV
        c1 = c0 + _GATE_COLS
        c2 = c1 + 3 * D_FOX
        c3 = c2 + N_HEADS_FOX
        small = jnp.concatenate([wl[:, c0:c1], wl[:, c2:c3], jnp.zeros((D, LANES - _GATE_COLS - N_HEADS_FOX), F32)], axis=1)
        w_proj = jnp.concatenate([wl[:, :c0], wl[:, c1:c2], small], axis=1).astype(BF16)
        w_gm = wl[:, c3:].astype(BF16)
        bf_row = jnp.zeros((1, LANES), F32).at[0, _GATE_COLS:_GATE_COLS + N_HEADS_FOX].set(b_forget[l])

        x = _ffn(x, mod3, g_ffn1[l], wg1, wu1, wd1, g_final, k_mod=0, final=False, tm=tm, ck=ffn_ck)

        (qn, kcr, vcr, ks, vsT, kw, vwT, qf, kf, vfT, gT) = _inproj(
            x, mod3, g_mix[l], pos_row, invf, bf_row, sel, oneq, onek, tri, w_proj, tm=tm, nsa_tk=nsa_tk, fox_tk=fox_tk)

        pk_t, pk_b, wk_t, wk_b, w2k = _compress_weights(pe_ck[l], w1_ck[l], w2_ck[l])
        pv_t, pv_b, wv_t, wv_b, w2v = _compress_weights(pe_cv[l], w1_cv[l], w2_cv[l])
        pe4 = jnp.concatenate([pk_t, pk_b, pv_t, pv_b], axis=0)
        kc, cmp_lhs = _compress(kcr, vcr, pe4, wk_t, wk_b, wv_t, wv_b, w2k, w2v, ovT)

        o_fox = _fox(qf, kf, vfT, tq=fox_tq)
        gT4 = gT.reshape(B, N_KV_NSA, 3 * GROUP, T)
        o_nsa = _nsa(qn, kc, cmp_lhs, ks, vsT, kw, vwT, gT4, tq=nsa_tq)

        x = _mixout(x, mod3, g_mix[l], o_nsa, o_fox, w_gm, w_up_nsa[l].astype(BF16), w_up_fox[l].astype(BF16),
                    w_o[l].astype(BF16), tm=tm)
        last = l == depth - 1
        x = _ffn(x, mod3, g_ffn2[l], wg2, wu2, wd2, g_final, k_mod=6, final=last, tm=tm, ck=ffn_ck)
    return x
```

```python
import functools

import numpy as np
import jax
import jax.numpy as jnp
from jax import lax
from jax.experimental import pallas as pl
from jax.experimental.pallas import tpu as pltpu

HEAD_DIM = 64
N_HEADS_NSA = 8
N_KV_NSA = 2
GROUP = N_HEADS_NSA // N_KV_NSA
N_HEADS_FOX = 8
CMP_BLOCK = 32
CMP_STRIDE = 16
CMP_HIDDEN = 128
SLC_BLOCK = 64
TOP_N = 16
WINDOW = 512
ROPE_THETA = 500000.0
ROPE_DIM = HEAD_DIM // 4
N_MOD = 9
RMS_EPS = 1e-6
NEG_INF = -1e30
FORCE_SCORE = 1e4

LANES = 128
LOG2E = 1.4426950408889634
V_ROWS = HEAD_DIM + 16
F32 = jnp.float32
BF16 = jnp.bfloat16
VMEM_LIMIT = 56 * 1024 * 1024

D_NSA = N_HEADS_NSA * HEAD_DIM
D_KV = N_KV_NSA * HEAD_DIM
D_FOX = N_HEADS_FOX * HEAD_DIM
N_KV_STREAMS = 6
_GATE_COLS = 3 * N_HEADS_NSA
_OFF_QN = 0
_OFF_KV = _OFF_QN + D_NSA
_OFF_QF = _OFF_KV + N_KV_STREAMS * D_KV
_OFF_KF = _OFF_QF + D_FOX
_OFF_VF = _OFF_KF + D_FOX
_OFF_SMALL = _OFF_VF + D_FOX
_W_IN_COLS = _OFF_SMALL + LANES
_TAIL = HEAD_DIM
_TAIL2 = HEAD_DIM + 36


def _params(sem):
    return pltpu.CompilerParams(dimension_semantics=sem, vmem_limit_bytes=VMEM_LIMIT)


def _nt(a, b):
    return lax.dot_general(a, b, (((1,), (1,)), ((), ())), preferred_element_type=F32)


def _mm(a, b):
    return jnp.dot(a, b, preferred_element_type=F32)


def _split3(x):
    hi = x.astype(BF16)
    r = x - hi.astype(F32)
    mid = r.astype(BF16)
    lo = (r - mid.astype(F32)).astype(BF16)
    return hi, mid, lo


def _rms_mod(x, g, shift, scale):
    ms = jnp.mean(x * x, axis=-1, keepdims=True)
    y = x * lax.rsqrt(ms + RMS_EPS) * g
    return y * (1.0 + scale) + shift


def _ada_kernel(c_ref, w_ref, b_ref, o_ref):
    c = c_ref[...]
    ca = c * jax.nn.sigmoid(c)
    h0, h1, h2 = _split3(ca)
    w0, w1, w2 = _split3(w_ref[...])
    acc = _mm(h0, w0) + _mm(h0, w1) + _mm(h1, w0)
    acc = acc + (_mm(h1, w1) + _mm(h0, w2) + _mm(h2, w0))
    o_ref[...] = acc + b_ref[...]


def _ada(c, w_ada, b_ada):
    B, D = c.shape
    n = w_ada.shape[1]
    tn = 1024
    return pl.pallas_call(
        _ada_kernel,
        grid=(n // tn,),
        in_specs=[
            pl.BlockSpec((B, D), lambda j: (0, 0)),
            pl.BlockSpec((D, tn), lambda j: (0, j)),
            pl.BlockSpec((1, tn), lambda j: (0, j)),
        ],
        out_specs=pl.BlockSpec((B, tn), lambda j: (0, j)),
        out_shape=jax.ShapeDtypeStruct((B, n), F32),
        compiler_params=_params(("arbitrary",)),
        name="ada",
    )(c, w_ada, b_ada.reshape(1, n))


def _ffn_kernel(x_ref, mod_ref, g_ref, wg_ref, wu_ref, wd_ref, gf_ref, o_ref, h_ref, acc_ref, *, k_mod, final, ck):
    x = x_ref[0]
    sh = mod_ref[0, k_mod:k_mod + 1, :]
    sc = mod_ref[0, k_mod + 1:k_mod + 2, :]
    ga = mod_ref[0, k_mod + 2:k_mod + 3, :]
    h_ref[...] = _rms_mod(x, g_ref[...], sh, sc).astype(BF16)
    acc_ref[...] = jnp.zeros_like(acc_ref)

    n_chunks = wg_ref.shape[1] // ck
    h = h_ref[...]

    def gate_up(j):
        return _mm(h, wg_ref[:, j * ck:(j + 1) * ck]), _mm(h, wu_ref[:, j * ck:(j + 1) * ck])

    gu = gate_up(0)
    for j in range(n_chunks):
        g, u = gu
        if j + 1 < n_chunks:
            gu = gate_up(j + 1)
        a = (g * jax.nn.sigmoid(g) * u).astype(BF16)
        acc_ref[...] += _mm(a, wd_ref[j * ck:(j + 1) * ck, :])
    out = x + 0.5 * ga * acc_ref[...]
    if final:
        ms = jnp.mean(out * out, axis=-1, keepdims=True)
        out = out * lax.rsqrt(ms + RMS_EPS) * gf_ref[...]
    o_ref[0] = out


def _ffn(x, mod3, g, wg, wu, wd, g_final, *, k_mod, final, tm, ck):
    B, T, D = x.shape
    F = wg.shape[1]
    assert F % ck == 0
    const2 = lambda b, t: (0, 0)
    return pl.pallas_call(
        functools.partial(_ffn_kernel, k_mod=k_mod, final=final, ck=ck),
        grid=(B, T // tm),
        in_specs=[
            pl.BlockSpec((1, tm, D), lambda b, t: (b, t, 0)),
            pl.BlockSpec((1, N_MOD, D), lambda b, t: (b, 0, 0)),
            pl.BlockSpec((1, D), lambda b, t: (0, 0)),
            pl.BlockSpec((D, F), const2, pipeline_mode=pl.Buffered(1)),
            pl.BlockSpec((D, F), const2, pipeline_mode=pl.Buffered(1)),
            pl.BlockSpec((F, D), const2, pipeline_mode=pl.Buffered(1)),
            pl.BlockSpec((1, D), lambda b, t: (0, 0)),
        ],
        out_specs=pl.BlockSpec((1, tm, D), lambda b, t: (b, t, 0)),
        out_shape=jax.ShapeDtypeStruct((B, T, D), F32),
        scratch_shapes=[pltpu.VMEM((tm, D), BF16), pltpu.VMEM((tm, D), F32)],
        compiler_params=_params(("arbitrary", "arbitrary")),
        name="ffn_final" if final else "ffn",
    )(x, mod3, g.reshape(1, D), wg, wu, wd, g_final.reshape(1, D))


def _inproj_kernel(x_ref, mod_ref, g_ref, pos_ref, invf_ref, bf_ref, sel_ref, oneq_ref, onek_ref, tri_ref, w_ref,
                   qn_ref, kcr_ref, vcr_ref, ks_ref, vsT_ref, kw_ref, vwT_ref, qf_ref, kf_ref, vfT_ref, gT_ref,
                   h_ref, carry_ref, *, tm):
    t_idx = pl.program_id(1)
    sh = mod_ref[0, 3:4, :]
    sc = mod_ref[0, 4:5, :]
    h_ref[...] = _rms_mod(x_ref[0], g_ref[...], sh, sc).astype(BF16)
    h = h_ref[...]

    lane = lax.broadcasted_iota(jnp.int32, (tm, LANES), 1)
    low = lane < HEAD_DIM
    first8 = (lane & (HEAD_DIM - 1)) < (ROPE_DIM // 2)
    ang = invf_ref[...] * pos_ref[0]
    cos8 = jnp.cos(ang)
    sin8 = jnp.sin(ang)
    rest = HEAD_DIM - ROPE_DIM
    cos_t = jnp.concatenate([cos8, cos8, jnp.ones((rest, tm), F32)] * 2, axis=0).T
    sin_t = jnp.concatenate([-sin8, sin8, jnp.zeros((rest, tm), F32)] * 2, axis=0).T

    def rope(xs):
        partner = jnp.where(first8, pltpu.roll(xs, LANES - ROPE_DIM // 2, 1), pltpu.roll(xs, ROPE_DIM // 2, 1))
        return xs * cos_t + partner * sin_t

    def split_heads(xs):
        return jnp.where(low, xs, 0.0), jnp.where(low, pltpu.roll(xs, HEAD_DIM, 1), 0.0)

    def with_ones(vt):
        extra = (lax.broadcasted_iota(jnp.int32, (V_ROWS - HEAD_DIM, vt.shape[1]), 0) == 0).astype(BF16)
        return jnp.concatenate([vt, extra], axis=0)

    sm = _mm(h, w_ref[:, _OFF_SMALL:_OFF_SMALL + LANES])
    gT_ref[0] = jax.nn.sigmoid(sm).T[:_GATE_COLS, :]
    xl = sm + bf_ref[...]
    logf = jnp.minimum(xl, 0.0) - jnp.log1p(jnp.exp(-jnp.abs(xl)))
    tri = tri_ref[...]
    l0, l1, l2 = _split3(logf)

    @pl.when(t_idx == 0)
    def _():
        carry_ref[...] = jnp.zeros_like(carry_ref)

    cf = _mm(tri, l0) + _mm(tri, l1) + _mm(tri, l2) + carry_ref[...]
    carry_ref[...] = cf[tm - 1:tm, :]
    pieces = _split3(cf * LOG2E)
    placed = _mm(pieces[0], sel_ref[0]) + _mm(pieces[1], sel_ref[1]) + _mm(pieces[2], sel_ref[2])
    q_tail = placed[:, :LANES]
    k_tail = placed[:, LANES:]

    qn = _mm(h, w_ref[:, _OFF_QN:_OFF_QN + D_NSA])
    scale = HEAD_DIM ** -0.5 * LOG2E
    for j in range(D_NSA // LANES):
        a, b = split_heads(rope(qn[:, j * LANES:(j + 1) * LANES]) * scale)
        qn_ref[0, 2 * j] = a.astype(BF16)
        qn_ref[0, 2 * j + 1] = b.astype(BF16)

    nsa_tk = vsT_ref.shape[-1]
    kv = _mm(h, w_ref[:, _OFF_KV:_OFF_KV + N_KV_STREAMS * D_KV])
    kcr_ref[0] = rope(kv[:, 0:LANES])
    vcr_ref[0] = kv[:, LANES:2 * LANES]
    blk = lax.shift_right_logical(t_idx * tm + lax.broadcasted_iota(jnp.int32, (tm, LANES), 0), SLC_BLOCK.bit_length() - 1)
    onehot = (lane == blk + HEAD_DIM).astype(F32)
    for k_out, v_out, off, tail in ((ks_ref, vsT_ref, 2 * LANES, onehot), (kw_ref, vwT_ref, 4 * LANES, None)):
        a, b = split_heads(rope(kv[:, off:off + LANES]))
        if tail is not None:
            a, b = a + tail, b + tail
        k_out[0, 0] = a.astype(BF16)
        k_out[0, 1] = b.astype(BF16)
        vT = kv[:, off + LANES:off + 2 * LANES].T.astype(BF16)
        for gi in range(N_KV_NSA):
            for c in range(tm // nsa_tk):
                v_out[0, gi, c] = with_ones(vT[gi * HEAD_DIM:(gi + 1) * HEAD_DIM, c * nsa_tk:(c + 1) * nsa_tk])

    qf = _mm(h, w_ref[:, _OFF_QF:_OFF_QF + D_FOX])
    kf = _mm(h, w_ref[:, _OFF_KF:_OFF_KF + D_FOX])
    for j in range(D_FOX // LANES):
        qa, qb = split_heads(qf[:, j * LANES:(j + 1) * LANES] * scale)
        ka, kb = split_heads(kf[:, j * LANES:(j + 1) * LANES])
        for hd, qh, kh in ((2 * j, qa, ka), (2 * j + 1, qb, kb)):
            qf_ref[0, hd] = (qh + q_tail + oneq_ref[hd:hd + 1, :]).astype(BF16)
            kf_ref[0, hd] = (kh + k_tail + onek_ref[hd:hd + 1, :]).astype(BF16)
    vf = _mm(h, w_ref[:, _OFF_VF:_OFF_VF + D_FOX])
    fox_tk = vfT_ref.shape[-1]
    for j in range(D_FOX // LANES):
        vT = vf[:, j * LANES:(j + 1) * LANES].T.astype(BF16)
        for hh in range(2):
            for c in range(tm // fox_tk):
                vfT_ref[0, 2 * j + hh, c] = with_ones(vT[hh * HEAD_DIM:(hh + 1) * HEAD_DIM, c * fox_tk:(c + 1) * fox_tk])


def _inproj(x, mod3, g, pos_row, invf, bf_row, sel, oneq, onek, tri, w, *, tm, nsa_tk, fox_tk):
    B, T, D = x.shape
    H, G = N_HEADS_NSA, N_KV_NSA
    bt = lambda b, t: (b, t, 0)
    hb = lambda b, t: (b, 0, t, 0)
    vb = lambda b, t: (b, 0, t, 0, 0)
    out_shape = [
        jax.ShapeDtypeStruct((B, H, T, LANES), BF16),
        jax.ShapeDtypeStruct((B, T, LANES), F32),
        jax.ShapeDtypeStruct((B, T, LANES), F32),
        jax.ShapeDtypeStruct((B, G, T, LANES), BF16),
        jax.ShapeDtypeStruct((B, G, T // nsa_tk, V_ROWS, nsa_tk), BF16),
        jax.ShapeDtypeStruct((B, G, T, LANES), BF16),
        jax.ShapeDtypeStruct((B, G, T // nsa_tk, V_ROWS, nsa_tk), BF16),
        jax.ShapeDtypeStruct((B, N_HEADS_FOX, T, LANES), BF16),
        jax.ShapeDtypeStruct((B, N_HEADS_FOX, T, LANES), BF16),
        jax.ShapeDtypeStruct((B, N_HEADS_FOX, T // fox_tk, V_ROWS, fox_tk), BF16),
        jax.ShapeDtypeStruct((B, _GATE_COLS, T), F32),
    ]
    out_specs = [
        pl.BlockSpec((1, H, tm, LANES), hb),
        pl.BlockSpec((1, tm, LANES), bt),
        pl.BlockSpec((1, tm, LANES), bt),
        pl.BlockSpec((1, G, tm, LANES), hb),
        pl.BlockSpec((1, G, tm // nsa_tk, V_ROWS, nsa_tk), vb),
        pl.BlockSpec((1, G, tm, LANES), hb),
        pl.BlockSpec((1, G, tm // nsa_tk, V_ROWS, nsa_tk), vb),
        pl.BlockSpec((1, N_HEADS_FOX, tm, LANES), hb),
        pl.BlockSpec((1, N_HEADS_FOX, tm, LANES), hb),
        pl.BlockSpec((1, N_HEADS_FOX, tm // fox_tk, V_ROWS, fox_tk), vb),
        pl.BlockSpec((1, _GATE_COLS, tm), lambda b, t: (b, 0, t)),
    ]
    row = lambda b, t: (0, 0)
    return pl.pallas_call(
        functools.partial(_inproj_kernel, tm=tm),
        grid=(B, T // tm),
        in_specs=[
            pl.BlockSpec((1, tm, D), bt),
            pl.BlockSpec((1, N_MOD, D), lambda b, t: (b, 0, 0)),
            pl.BlockSpec((1, D), row),
            pl.BlockSpec((1, 1, tm), lambda b, t: (b, 0, t)),
            pl.BlockSpec((ROPE_DIM // 2, tm), row),
            pl.BlockSpec((1, LANES), row),
            pl.BlockSpec((3, LANES, 2 * LANES), lambda b, t: (0, 0, 0)),
            pl.BlockSpec((N_HEADS_FOX, LANES), row),
            pl.BlockSpec((N_HEADS_FOX, LANES), row),
            pl.BlockSpec((tm, tm), row),
            pl.BlockSpec((D, _W_IN_COLS), row, pipeline_mode=pl.Buffered(1)),
        ],
        out_specs=out_specs,
        out_shape=out_shape,
        scratch_shapes=[pltpu.VMEM((tm, D), BF16), pltpu.VMEM((1, LANES), F32)],
        compiler_params=_params(("arbitrary", "arbitrary")),
        name="inproj",
    )(x, mod3, g.reshape(1, D), pos_row, invf, bf_row, sel, oneq, onek, tri, w)


def _gelu_tanh(x):
    c = np.float32(np.sqrt(2.0 / np.pi))
    return x * (0.5 * (1.0 + jnp.tanh(c * (x + 0.044715 * (x * x * x)))))


def _compress_kernel(zk_ref, zv_ref, pe_ref, wkt_ref, wkb_ref, wvt_ref, wvb_ref, w2k_ref, w2v_ref, ovT_ref, kc_ref, lhs_ref):
    nsub = zk_ref.shape[1] // CMP_STRIDE

    def mlp(z_ref, pe_top, pe_bot, wt_ref, wb_ref):
        a = b = None
        for j in range(CMP_STRIDE):
            xj = z_ref[0, pl.ds(j, nsub, stride=CMP_STRIDE), :]
            lanes = slice(j * LANES, (j + 1) * LANES)
            aj = _mm((xj + pe_ref[pe_top:pe_top + 1, lanes]).astype(BF16), wt_ref[lanes, :])
            bj = _mm((xj + pe_ref[pe_bot:pe_bot + 1, lanes]).astype(BF16), wb_ref[lanes, :])
            a = aj if a is None else a + aj
            b = bj if b is None else b + bj
        return _gelu_tanh(a + pltpu.roll(b, nsub - 1, 0))

    hk = mlp(zk_ref, 0, 1, wkt_ref, wkb_ref)
    hv = mlp(zv_ref, 2, 3, wvt_ref, wvb_ref)
    for gi in range(N_KV_NSA):
        kc_ref[0, gi] = _mm(hk[:, gi * CMP_HIDDEN:(gi + 1) * CMP_HIDDEN].astype(BF16), w2k_ref[...]).astype(BF16)
        vc = _mm(hv[:, gi * CMP_HIDDEN:(gi + 1) * CMP_HIDDEN].astype(BF16), w2v_ref[...])
        ones = (lax.broadcasted_iota(jnp.int32, (V_ROWS - HEAD_DIM, nsub), 0) == 0).astype(BF16)
        lhs_ref[0, gi] = jnp.concatenate([vc.T[:HEAD_DIM, :].astype(BF16), ones, ovT_ref[...]], axis=0)


def _compress(zk, zv, pe4, wkt, wkb, wvt, wvb, w2k, w2v, ovT):
    B, T, _ = zk.shape
    nsub = T // CMP_STRIDE
    zc = CMP_STRIDE * LANES
    G = N_KV_NSA
    c2 = lambda b: (0, 0)
    zspec = pl.BlockSpec((1, T, LANES), lambda b: (b, 0, 0))
    wspec = pl.BlockSpec((zc, G * CMP_HIDDEN), c2)
    w2spec = pl.BlockSpec((CMP_HIDDEN, LANES), c2)
    return pl.pallas_call(
        _compress_kernel,
        grid=(B,),
        in_specs=[zspec, zspec, pl.BlockSpec((4, zc), c2), wspec, wspec, wspec, wspec, w2spec, w2spec,
                  pl.BlockSpec(ovT.shape, c2)],
        out_specs=[
            pl.BlockSpec((1, G, nsub, LANES), lambda b: (b, 0, 0, 0)),
            pl.BlockSpec((1, G, V_ROWS + ovT.shape[0], nsub), lambda b: (b, 0, 0, 0)),
        ],
        out_shape=[
            jax.ShapeDtypeStruct((B, G, nsub, LANES), BF16),
            jax.ShapeDtypeStruct((B, G, V_ROWS + ovT.shape[0], nsub), BF16),
        ],
        compiler_params=_params(("arbitrary",)),
        name="compress",
    )(zk, zv, pe4, wkt, wkb, wvt, wvb, w2k, w2v, ovT)


def _flash_init(m_ref, acc_ref):
    m_ref[...] = jnp.full_like(m_ref, NEG_INF)
    acc_ref[...] = jnp.zeros_like(acc_ref)


def _chain_softmax(s_list, m_ref, tile_max=None):
    m_old = m_ref[...]
    m_new = m_old
    if tile_max is not None:
        m_new = jnp.maximum(m_new, tile_max)
    else:
        for s in s_list:
            m_new = jnp.maximum(m_new, jnp.max(s, axis=0, keepdims=True))
    m_ref[...] = m_new
    return jnp.exp2(m_old - m_new), [jnp.exp2(s - m_new).astype(BF16) for s in s_list]


def _chain_values(staged, v_list, acc_ref):
    alpha, ps = staged
    pv = _mm(v_list[0], ps[0])
    for v, p in zip(v_list[1:], ps[1:]):
        pv = pv + _mm(v, p)
    acc_ref[...] = alpha * acc_ref[...] + pv


def _normalized(acc):
    return acc[:HEAD_DIM] / acc[HEAD_DIM:HEAD_DIM + 1]


def _pipelined_sweep(n, n_chains, store, softmax, values):
    chains = range(n_chains)

    def by_parity(i, fn):
        @pl.when((i & 1) == 0)
        def _():
            fn(0)

        @pl.when((i & 1) == 1)
        def _():
            fn(1)

    def step(nxt, masked, cur, par):
        staged = None
        for c in chains:
            if nxt is not None:
                store(nxt, 1 - par, masked, c)
            new = softmax(cur, par, c)
            if staged is not None:
                values(cur, staged, c - 1)
            staged = new
        values(cur, staged, n_chains - 1)

    def first_scores(masked):
        for c in chains:
            store(0, 0, masked, c)

    @pl.when(n == 0)
    def _():
        first_scores(True)
        step(None, False, 0, 0)

    @pl.when(n == 1)
    def _():
        first_scores(False)
        step(1, True, 0, 0)
        step(None, False, 1, 1)

    @pl.when(n >= 2)
    def _():
        first_scores(False)
        step(1, False, 0, 0)
        n_mid = n - 2

        def body(j, carry):
            step(2 * j + 2, False, 2 * j + 1, 1)
            step(2 * j + 3, False, 2 * j + 2, 0)
            return carry

        lax.fori_loop(0, n_mid // 2, body, 0)

        @pl.when((n_mid & 1) == 1)
        def _():
            step(n_mid + 1, False, n_mid, 1)

        def last_two(par):
            step(n, True, n - 1, par)
            step(None, False, n, 1 - par)

        by_parity(n - 1, last_two)


_FOX_HEADS_PER_STEP = 8


def _fox_kernel(q_ref, k_ref, vT_ref, o_ref, m_ref, acc_ref, s0_ref, s1_ref, x0_ref, x1_ref, *, tq, tk):
    qi = pl.program_id(2)
    nkc = tq // tk
    heads = range(_FOX_HEADS_PER_STEP)
    s_bufs = (s0_ref, s1_ref)
    x_bufs = (x0_ref, x1_ref)
    row = lax.broadcasted_iota(jnp.int32, (tk, tq), 0)
    lane = lax.broadcasted_iota(jnp.int32, (tk, tq), 1)
    for hh in heads:
        _flash_init(m_ref.at[hh], acc_ref.at[hh])

    def store(kt, buf, masked, hh):
        tile_max = None
        for c in range(nkc):
            k0 = pl.multiple_of(kt * tq + c * tk, tk)
            s = _nt(k_ref[0, hh, pl.ds(k0, tk), :], q_ref[0, hh])
            if masked:
                s = jnp.where(row + c * tk <= lane, s, NEG_INF)
            s_bufs[buf][hh, c * tk:(c + 1) * tk, :] = s
            cm = jnp.max(s, axis=0, keepdims=True)
            tile_max = cm if tile_max is None else jnp.maximum(tile_max, cm)
        x_bufs[buf][hh] = tile_max

    def softmax(kt, buf, hh):
        return _chain_softmax([s_bufs[buf][hh, c * tk:(c + 1) * tk, :] for c in range(nkc)], m_ref.at[hh], x_bufs[buf][hh])

    def values(kt, staged, hh):
        _chain_values(staged, [vT_ref[0, hh, kt * nkc + c] for c in range(nkc)], acc_ref.at[hh])

    _pipelined_sweep(qi, len(heads), store, softmax, values)
    for pr in range(_FOX_HEADS_PER_STEP // 2):
        st = jnp.concatenate([_normalized(acc_ref[2 * pr]), _normalized(acc_ref[2 * pr + 1])], axis=0)
        o_ref[0, :, pr * LANES:(pr + 1) * LANES] = st.T.astype(BF16)


def _fox(qf, kf, vfT, *, tq):
    B, H, T, _ = qf.shape
    tk = vfT.shape[-1]
    nh = _FOX_HEADS_PER_STEP
    return pl.pallas_call(
        functools.partial(_fox_kernel, tq=tq, tk=tk),
        grid=(B, H // nh, T // tq),
        in_specs=[
            pl.BlockSpec((1, nh, tq, LANES), lambda b, p, i: (b, p, i, 0)),
            pl.BlockSpec((1, nh, T, LANES), lambda b, p, i: (b, p, 0, 0)),
            pl.BlockSpec((1, nh, T // tk, V_ROWS, tk), lambda b, p, i: (b, p, 0, 0, 0)),
        ],
        out_specs=pl.BlockSpec((1, tq, nh * HEAD_DIM), lambda b, p, i: (b, i, p)),
        out_shape=jax.ShapeDtypeStruct((B, T, H * HEAD_DIM), BF16),
        scratch_shapes=[
            pltpu.VMEM((nh, 1, tq), F32),
            pltpu.VMEM((nh, V_ROWS, tq), F32),
            pltpu.VMEM((nh, tq, tq), F32),
            pltpu.VMEM((nh, tq, tq), F32),
            pltpu.VMEM((nh, 1, tq), F32),
            pltpu.VMEM((nh, 1, tq), F32),
        ],
        compiler_params=_params(("arbitrary", "arbitrary", "arbitrary")),
        name="fox",
    )(qf, kf, vfT)


def _softmax_pv(s_lists, v_lists):
    staged = []
    for s_list in s_lists:
        m = jnp.max(s_list[0], axis=0, keepdims=True)
        for s in s_list[1:]:
            m = jnp.maximum(m, jnp.max(s, axis=0, keepdims=True))
        staged.append([jnp.exp2(s - m).astype(BF16) for s in s_list])
    outs = []
    for ps, v_list in zip(staged, v_lists):
        pv = _mm(v_list[0], ps[0])
        for v, p in zip(v_list[1:], ps[1:]):
            pv = pv + _mm(v, p)
        outs.append(_normalized(pv))
    return outs


_NSA_LANE_SPLITS = GROUP


def _nsa_kernel(q_ref, kc_ref, cmp_lhs_ref, ks_ref, vsT_ref, kw_ref, vwT_ref, g_ref, o_ref,
                m_ref, acc_ref, imp_ref, cnt_ref, qa_ref, oc_ref, ow_ref, s0_ref, s1_ref, x0_ref, x1_ref, *, tq, tk):
    qt = pl.program_id(1)
    G = N_KV_NSA
    nq = GROUP * tq
    hw = nq // _NSA_LANE_SPLITS
    heads_per_split = GROUP // _NSA_LANE_SPLITS
    ncp = kc_ref.shape[2]
    ns = cmp_lhs_ref.shape[2] - V_ROWS
    blk_shift = SLC_BLOCK.bit_length() - 1
    t_row = qt * tq + (lax.broadcasted_iota(jnp.int32, (1, nq), 1) & (tq - 1))
    ql = lax.broadcasted_iota(jnp.int32, (1, hw), 1) & (tq - 1)
    row_k = lax.broadcasted_iota(jnp.int32, (tk, hw), 0)
    causal = row_k <= ql
    n_io = lax.broadcasted_iota(jnp.int32, (ns, tq), 0)
    cur = lax.shift_right_logical(qt * tq + lax.broadcasted_iota(jnp.int32, (ns, tq), 1), blk_shift)
    forced = (n_io == 0) | (n_io == cur) | (n_io == cur - 1)
    visible = n_io <= cur
    c_io = lax.broadcasted_iota(jnp.int32, (ncp, nq), 0)
    cmp_valid = (c_io * CMP_STRIDE + (CMP_BLOCK - 1)) <= t_row
    chains = [(g, h) for g in range(G) for h in range(_NSA_LANE_SPLITS)]

    def q_of(g):
        return q_ref[0, g * GROUP:(g + 1) * GROUP].reshape(nq, LANES)

    def q_split(g, h):
        h0 = g * GROUP + h * heads_per_split
        return q_ref[0, h0:h0 + heads_per_split].reshape(hw, LANES)

    def lanes_of(h):
        return pl.ds(h * hw, hw)

    cmp_ok = cmp_valid[:, :hw]
    s_cmp = [_nt(kc_ref[0, g], q_split(g, h)) for g, h in chains]
    p_cmp = []
    for sc in s_cmp:
        smk = jnp.where(cmp_ok, sc, NEG_INF)
        p_cmp.append(jnp.exp2(smk - jnp.max(smk, axis=0, keepdims=True)).astype(BF16))
    any_valid = t_row[:, :hw] >= (CMP_BLOCK - 1)
    imp_sum = [None] * G
    for (g, h), pc in zip(chains, p_cmp):
        r_all = _mm(cmp_lhs_ref[0, g], pc)
        inv = jnp.where(any_valid, 1.0 / r_all[HEAD_DIM:HEAD_DIM + 1], 0.0)
        oc_ref[g, :, lanes_of(h)] = r_all[:HEAD_DIM] * inv
        imp_h = r_all[V_ROWS:] * inv
        imp_sum[g] = imp_h if imp_sum[g] is None else imp_sum[g] + imp_h
    for g in range(G):
        imp_ref[g] = jnp.where(forced, FORCE_SCORE, jnp.where(visible, imp_sum[g], -1.0))

    n_back = WINDOW // tk
    hq = tq // 2
    row_h = lax.broadcasted_iota(jnp.int32, (hq, hq), 0)
    lane_h = lax.broadcasted_iota(jnp.int32, (hq, hq), 1)

    @pl.when(qt >= n_back)
    def _():
        base = pl.multiple_of((qt - 1) * tk, tk)
        s_lists, v_lists, where_to = [], [], []
        for g, h in chains:
            for sub in range(2):
                q = q_ref[0, g * GROUP + h, sub * hq:(sub + 1) * hq, :]
                s3 = _nt(kw_ref[0, g, pl.ds(base + sub * hq, 3 * hq), :], q)
                s_lists.append([jnp.where(row_h > lane_h, s3[:hq], NEG_INF),
                                s3[hq:2 * hq],
                                jnp.where(row_h <= lane_h, s3[2 * hq:], NEG_INF)])
                halves = [vwT_ref[0, g, qt - 1, :, :hq], vwT_ref[0, g, qt - 1, :, hq:],
                          vwT_ref[0, g, qt, :, :hq], vwT_ref[0, g, qt, :, hq:]]
                v_lists.append(halves[sub:sub + 3])
                where_to.append((g, pl.ds(h * tq + sub * hq, hq)))
        for (g, lanes), o in zip(where_to, _softmax_pv(s_lists, v_lists)):
            ow_ref[g, :, lanes] = o

    @pl.when(qt < n_back)
    def _():
        s_lists = []
        for g, h in chains:
            q = q_split(g, h)
            s_lists.append([jnp.where((row_k + j * tk) <= (qt * tq + ql), _nt(kw_ref[0, g, j * tk:(j + 1) * tk, :], q), NEG_INF)
                            for j in range(n_back)])
        v_lists = [[vwT_ref[0, g, j] for j in range(n_back)] for g, h in chains]
        for (g, h), o in zip(chains, _softmax_pv(s_lists, v_lists)):
            ow_ref[g, :, lanes_of(h)] = o

    n_vis = jnp.minimum((qt + 1) * (tq // SLC_BLOCK), ns)
    sub = 8
    sub_io = lax.broadcasted_iota(jnp.int32, (sub, tq), 0)
    cnt_ref[...] = jnp.zeros_like(cnt_ref)
    for mb in range(ns // sub):
        @pl.when(mb * sub < n_vis)
        def _():
            for g in range(G):
                groups = [imp_ref[g, j * sub:(j + 1) * sub, :] for j in range(ns // sub)]
                counts = [cnt_ref[g, j * sub:(j + 1) * sub, :] for j in range(ns // sub)]
                for mi in range(sub):
                    row = jnp.broadcast_to(groups[mb][mi:mi + 1, :], (sub, tq))
                    for j in range(ns // sub):
                        if j < mb:
                            beats = row > groups[j]
                        elif j > mb:
                            beats = row >= groups[j]
                        else:
                            beats = (row > groups[j]) | ((row >= groups[j]) & (sub_io > mi))
                        counts[j] = counts[j] + beats.astype(F32)
                for j in range(ns // sub):
                    cnt_ref[g, j * sub:(j + 1) * sub, :] = counts[j]

    for g in range(G):
        bias = jnp.where(cnt_ref[g] < float(TOP_N), 0.0, NEG_INF)
        parts = [jnp.zeros((HEAD_DIM, tq), F32), bias]
        if ns < LANES - HEAD_DIM:
            parts.append(jnp.zeros((LANES - HEAD_DIM - ns, tq), F32))
        bias_t = jnp.concatenate(parts, axis=0).T.astype(BF16)
        qa_ref[g] = q_of(g) + jnp.concatenate([bias_t] * GROUP, axis=0)
        _flash_init(m_ref.at[g], acc_ref.at[g])

    slc_refs = [(m_ref.at[g, :, lanes_of(h)], acc_ref.at[g, :, lanes_of(h)]) for g, h in chains]

    s_bufs = (s0_ref, s1_ref)
    x_bufs = (x0_ref, x1_ref)

    def slc_store(kj, buf, masked, c):
        g, h = chains[c]
        k0 = pl.multiple_of(kj * tk, tk)
        s = _nt(ks_ref[0, g, pl.ds(k0, tk), :], qa_ref[g, lanes_of(h), :])
        if masked:
            s = jnp.where(causal, s, NEG_INF)
        s_bufs[buf][g, :, lanes_of(h)] = s
        x_bufs[buf][g, :, lanes_of(h)] = jnp.max(s, axis=0, keepdims=True)

    def slc_softmax(kj, buf, c):
        g, h = chains[c]
        return _chain_softmax([s_bufs[buf][g, :, lanes_of(h)]], slc_refs[c][0], x_bufs[buf][g, :, lanes_of(h)])

    def slc_values(kj, staged, c):
        _chain_values(staged, [vsT_ref[0, chains[c][0], kj]], slc_refs[c][1])

    _pipelined_sweep(qt, len(chains), slc_store, slc_softmax, slc_values)

    for g in range(G):
        def gate_row(j):
            return jnp.concatenate([g_ref[0, g, 3 * r + j:3 * r + j + 1, :] for r in range(GROUP)], axis=1)

        oT = gate_row(0) * oc_ref[g] + gate_row(1) * _normalized(acc_ref[g]) + gate_row(2) * ow_ref[g]
        for pr in range(GROUP // 2):
            st = jnp.concatenate([oT[:, (2 * pr) * tq:(2 * pr + 1) * tq], oT[:, (2 * pr + 1) * tq:(2 * pr + 2) * tq]], axis=0)
            col = (g * GROUP // 2 + pr) * LANES
            o_ref[0, :, col:col + LANES] = st.T.astype(BF16)


def _nsa(qn, kc, cmp_lhs, ks, vsT, kw, vwT, gT4, *, tq):
    B, H, T, _ = qn.shape
    G = N_KV_NSA
    tk = vsT.shape[-1]
    ncp = kc.shape[2]
    ns = cmp_lhs.shape[2] - V_ROWS
    assert tq == tk and WINDOW == tk and _NSA_LANE_SPLITS == GROUP and T >= WINDOW and ns <= LANES - HEAD_DIM and ns % 8 == 0
    nq = GROUP * tq
    kvspec = pl.BlockSpec((1, G, T, LANES), lambda b, i: (b, 0, 0, 0))
    vtspec = pl.BlockSpec((1, G, T // tk, V_ROWS, tk), lambda b, i: (b, 0, 0, 0, 0))
    return pl.pallas_call(
        functools.partial(_nsa_kernel, tq=tq, tk=tk),
        grid=(B, T // tq),
        in_specs=[
            pl.BlockSpec((1, H, tq, LANES), lambda b, i: (b, 0, i, 0)),
            pl.BlockSpec((1, G, ncp, LANES), lambda b, i: (b, 0, 0, 0)),
            pl.BlockSpec((1, G, V_ROWS + ns, ncp), lambda b, i: (b, 0, 0, 0)),
            kvspec, vtspec, kvspec, vtspec,
            pl.BlockSpec((1, G, 3 * GROUP, tq), lambda b, i: (b, 0, 0, i)),
        ],
        out_specs=pl.BlockSpec((1, tq, H * HEAD_DIM), lambda b, i: (b, i, 0)),
        out_shape=jax.ShapeDtypeStruct((B, T, H * HEAD_DIM), BF16),
        scratch_shapes=[
            pltpu.VMEM((G, 1, nq), F32),
            pltpu.VMEM((G, V_ROWS, nq), F32),
            pltpu.VMEM((G, ns, tq), F32),
            pltpu.VMEM((G, ns, tq), F32),
            pltpu.VMEM((G, nq, LANES), BF16),
            pltpu.VMEM((G, HEAD_DIM, nq), F32),
            pltpu.VMEM((G, HEAD_DIM, nq), F32),
            pltpu.VMEM((G, tk, nq), F32),
            pltpu.VMEM((G, tk, nq), F32),
            pltpu.VMEM((G, 1, nq), F32),
            pltpu.VMEM((G, 1, nq), F32),
        ],
        compiler_params=_params(("arbitrary", "arbitrary")),
        name="nsa",
    )(qn, kc, cmp_lhs, ks, vsT, kw, vwT, gT4)


def _mixout_kernel(x_ref, mod_ref, g_ref, on_ref, of_ref, wgm_ref, wun_ref, wuf_ref, wo_ref, o_ref):
    x = x_ref[0]
    D = x.shape[-1]
    sh = mod_ref[0, 3:4, :]
    sc = mod_ref[0, 4:5, :]
    ga = mod_ref[0, 5:6, :]
    h = _rms_mod(x, g_ref[...], sh, sc).astype(BF16)
    gm = _mm(h, wgm_ref[...])
    un = _mm(on_ref[0], wun_ref[...])
    uf = _mm(of_ref[0], wuf_ref[...])
    merged = jax.nn.sigmoid(gm[:, :D]) * un + jax.nn.sigmoid(gm[:, D:]) * uf
    y = _mm(merged.astype(BF16), wo_ref[...])
    o_ref[0] = x + ga * y


def _mixout(x, mod3, g, o_nsa, o_fox, wgm, wun, wuf, wo, *, tm):
    B, T, D = x.shape
    bt = lambda b, t: (b, t, 0)
    c2 = lambda b, t: (0, 0)
    dn = o_nsa.shape[-1]
    return pl.pallas_call(
        _mixout_kernel,
        grid=(B, T // tm),
        in_specs=[
            pl.BlockSpec((1, tm, D), bt),
            pl.BlockSpec((1, N_MOD, D), lambda b, t: (b, 0, 0)),
            pl.BlockSpec((1, D), c2),
            pl.BlockSpec((1, tm, dn), bt),
            pl.BlockSpec((1, tm, dn), bt),
            pl.BlockSpec((D, 2 * D), c2, pipeline_mode=pl.Buffered(1)),
            pl.BlockSpec((dn, D), c2, pipeline_mode=pl.Buffered(1)),
            pl.BlockSpec((dn, D), c2, pipeline_mode=pl.Buffered(1)),
            pl.BlockSpec((D, D), c2, pipeline_mode=pl.Buffered(1)),
        ],
        out_specs=pl.BlockSpec((1, tm, D), bt),
        out_shape=jax.ShapeDtypeStruct((B, T, D), F32),
        compiler_params=_params(("arbitrary", "arbitrary")),
        name="mixout",
    )(x, mod3, g.reshape(1, D), o_nsa, o_fox, wgm, wun, wuf, wo)


def _compress_weights(pe, w1, w2):
    half = CMP_BLOCK // 2
    eye = jnp.eye(N_KV_NSA, dtype=F32)

    def expand(w_half):
        w3 = w_half.reshape(half, HEAD_DIM, CMP_HIDDEN)
        return jnp.einsum('jdn,gh->jgdhn', w3, eye).reshape(half * N_KV_NSA * HEAD_DIM, N_KV_NSA * CMP_HIDDEN).astype(BF16)

    def pe_row(pe_half):
        return jnp.broadcast_to(pe_half[:, None, :], (half, N_KV_NSA, HEAD_DIM)).reshape(1, -1)

    w2p = jnp.pad(w2, ((0, 0), (0, LANES - HEAD_DIM))).astype(BF16)
    return (pe_row(pe[:half]), pe_row(pe[half:]), expand(w1[:half * HEAD_DIM]), expand(w1[half * HEAD_DIM:]), w2p)


def kernel(x, c, positions, w_ada, b_ada, g_ffn1, w_gate1, w_up1, w_down1, g_mix, w_in, b_forget, pe_ck, w1_ck, w2_ck, pe_cv, w1_cv, w2_cv, w_up_nsa, w_up_fox, w_o, g_ffn2, w_gate2, w_up2, w_down2, g_final):
    B, T, D = x.shape
    depth = w_ada.shape[0]
    tm = 512
    tm_dense = 1024 if T % 1024 == 0 else tm
    ffn_ck = 256
    nsa_tq, nsa_tk = 512, 512
    fox_tq, fox_tk = 512, 512
    n_slc = T // SLC_BLOCK
    n_sub = T // CMP_STRIDE

    half = ROPE_DIM // 2
    inv_freq = ROPE_THETA ** (-jnp.arange(half, dtype=F32) / half)
    invf = jnp.broadcast_to(inv_freq[:, None], (half, tm))
    cmp_start = np.arange(n_sub) * CMP_STRIDE
    slc_start = np.arange(n_slc) * SLC_BLOCK
    ov = ((cmp_start[:, None] < slc_start[None, :] + SLC_BLOCK) & (slc_start[None, :] < cmp_start[:, None] + CMP_BLOCK))
    ov[n_sub - CMP_BLOCK // CMP_STRIDE + 1:, :] = False
    ovT = jnp.asarray(ov.T, dtype=BF16)

    pos_row = positions.astype(F32)[:, None, :]
    sel_np = np.zeros((3, LANES, 2 * LANES), np.float32)
    oneq_np = np.zeros((N_HEADS_FOX, LANES), np.float32)
    onek_np = np.zeros((N_HEADS_FOX, LANES), np.float32)
    for hd in range(N_HEADS_FOX):
        for j in range(3):
            sel_np[j, _GATE_COLS + hd, _TAIL + 3 * hd + j] = 1.0
            sel_np[j, _GATE_COLS + hd, LANES + _TAIL2 + 3 * hd + j] = -1.0
            oneq_np[hd, _TAIL2 + 3 * hd + j] = 1.0
            onek_np[hd, _TAIL + 3 * hd + j] = 1.0
    sel, oneq, onek = jnp.asarray(sel_np, BF16), jnp.asarray(oneq_np), jnp.asarray(onek_np)
    tri = jnp.asarray(np.tril(np.ones((tm, tm), np.float32)), BF16)
    c_in = c
    for l in range(depth):
        mod3 = _ada(c_in, w_ada[l], b_ada[l]).reshape(B, N_MOD, D)
        wg1, wu1, wd1 = w_gate1[l].astype(BF16), w_up1[l].astype(BF16), w_down1[l].astype(BF16)
        wg2, wu2, wd2 = w_gate2[l].astype(BF16), w_up2[l].astype(BF16), w_down2[l].astype(BF16)
        wl = w_in[l]
        c0 = D_NSA + N_KV_STREAMS * D_KV
        c1 = c0 + _GATE_COLS
        c2 = c1 + 3 * D_FOX
        c3 = c2 + N_HEADS_FOX
        small = jnp.concatenate([wl[:, c0:c1], wl[:, c2:c3], jnp.zeros((D, LANES - _GATE_COLS - N_HEADS_FOX), F32)], axis=1)
        w_proj = jnp.concatenate([wl[:, :c0], wl[:, c1:c2], small], axis=1).astype(BF16)
        w_gm = wl[:, c3:].astype(BF16)
        bf_row = jnp.zeros((1, LANES), F32).at[0, _GATE_COLS:_GATE_COLS + N_HEADS_FOX].set(b_forget[l])

        x = _ffn(x, mod3, g_ffn1[l], wg1, wu1, wd1, g_final, k_mod=0, final=False, tm=tm_dense, ck=ffn_ck)

        (qn, kcr, vcr, ks, vsT, kw, vwT, qf, kf, vfT, gT) = _inproj(
            x, mod3, g_mix[l], pos_row, invf, bf_row, sel, oneq, onek, tri, w_proj, tm=tm, nsa_tk=nsa_tk, fox_tk=fox_tk)

        pk_t, pk_b, wk_t, wk_b, w2k = _compress_weights(pe_ck[l], w1_ck[l], w2_ck[l])
        pv_t, pv_b, wv_t, wv_b, w2v = _compress_weights(pe_cv[l], w1_cv[l], w2_cv[l])
        pe4 = jnp.concatenate([pk_t, pk_b, pv_t, pv_b], axis=0)
        kc, cmp_lhs = _compress(kcr, vcr, pe4, wk_t, wk_b, wv_t, wv_b, w2k, w2v, ovT)

        o_fox = _fox(qf, kf, vfT, tq=fox_tq)
        gT4 = gT.reshape(B, N_KV_NSA, 3 * GROUP, T)
        o_nsa = _nsa(qn, kc, cmp_lhs, ks, vsT, kw, vwT, gT4, tq=nsa_tq)

        x = _mixout(x, mod3, g_mix[l], o_nsa, o_fox, w_gm, w_up_nsa[l].astype(BF16), w_up_fox[l].astype(BF16),
                    w_o[l].astype(BF16), tm=tm_dense)
        last = l == depth - 1
        x = _ffn(x, mod3, g_ffn2[l], wg2, wu2, wd2, g_final, k_mod=6, final=last, tm=tm_dense, ck=ffn_ck)
    return x
```

```python
import functools

import numpy as np
import jax
import jax.numpy as jnp
from jax import lax
from jax.experimental import pallas as pl
from jax.experimental.pallas import tpu as pltpu

HEAD_DIM = 64
N_HEADS_NSA = 8
N_KV_NSA = 2
GROUP = N_HEADS_NSA // N_KV_NSA
N_HEADS_FOX = 8
CMP_BLOCK = 32
CMP_STRIDE = 16
CMP_HIDDEN = 128
SLC_BLOCK = 64
TOP_N = 16
WINDOW = 512
ROPE_THETA = 500000.0
ROPE_DIM = HEAD_DIM // 4
N_MOD = 9
RMS_EPS = 1e-6
NEG_INF = -1e30
FORCE_SCORE = 1e4

LANES = 128
LOG2E = 1.4426950408889634
V_ROWS = HEAD_DIM + 16
F32 = jnp.float32
BF16 = jnp.bfloat16
VMEM_LIMIT = 56 * 1024 * 1024

D_NSA = N_HEADS_NSA * HEAD_DIM
D_KV = N_KV_NSA * HEAD_DIM
D_FOX = N_HEADS_FOX * HEAD_DIM
N_KV_STREAMS = 6
_GATE_COLS = 3 * N_HEADS_NSA
_OFF_QN = 0
_OFF_KV = _OFF_QN + D_NSA
_OFF_QF = _OFF_KV + N_KV_STREAMS * D_KV
_OFF_KF = _OFF_QF + D_FOX
_OFF_VF = _OFF_KF + D_FOX
_OFF_SMALL = _OFF_VF + D_FOX
_W_IN_COLS = _OFF_SMALL + LANES
_TAIL = HEAD_DIM
_TAIL2 = HEAD_DIM + 36


def _params(sem):
    return pltpu.CompilerParams(dimension_semantics=sem, vmem_limit_bytes=VMEM_LIMIT)


def _nt(a, b):
    return lax.dot_general(a, b, (((1,), (1,)), ((), ())), preferred_element_type=F32)


def _mm(a, b):
    return jnp.dot(a, b, preferred_element_type=F32)


def _split3(x):
    hi = x.astype(BF16)
    r = x - hi.astype(F32)
    mid = r.astype(BF16)
    lo = (r - mid.astype(F32)).astype(BF16)
    return hi, mid, lo


def _rms_mod(x, g, shift, scale):
    ms = jnp.mean(x * x, axis=-1, keepdims=True)
    y = x * lax.rsqrt(ms + RMS_EPS) * g
    return y * (1.0 + scale) + shift


def _ada_kernel(c_ref, w_ref, b_ref, o_ref):
    c = c_ref[...]
    ca = c * jax.nn.sigmoid(c)
    h0, h1, h2 = _split3(ca)
    w0, w1, w2 = _split3(w_ref[...])
    acc = _mm(h0, w0) + _mm(h0, w1) + _mm(h1, w0)
    acc = acc + (_mm(h1, w1) + _mm(h0, w2) + _mm(h2, w0))
    o_ref[...] = acc + b_ref[...]


def _ada(c, w_ada, b_ada):
    B, D = c.shape
    n = w_ada.shape[1]
    tn = 1024
    return pl.pallas_call(
        _ada_kernel,
        grid=(n // tn,),
        in_specs=[
            pl.BlockSpec((B, D), lambda j: (0, 0)),
            pl.BlockSpec((D, tn), lambda j: (0, j)),
            pl.BlockSpec((1, tn), lambda j: (0, j)),
        ],
        out_specs=pl.BlockSpec((B, tn), lambda j: (0, j)),
        out_shape=jax.ShapeDtypeStruct((B, n), F32),
        compiler_params=_params(("arbitrary",)),
        name="ada",
    )(c, w_ada, b_ada.reshape(1, n))


def _ffn_kernel(x_ref, mod_ref, g_ref, wg_ref, wu_ref, wd_ref, gf_ref, o_ref, h_ref, acc_ref, *, k_mod, final, ck):
    x = x_ref[0]
    sh = mod_ref[0, k_mod:k_mod + 1, :]
    sc = mod_ref[0, k_mod + 1:k_mod + 2, :]
    ga = mod_ref[0, k_mod + 2:k_mod + 3, :]
    h_ref[...] = _rms_mod(x, g_ref[...], sh, sc).astype(BF16)
    acc_ref[...] = jnp.zeros_like(acc_ref)

    n_chunks = wg_ref.shape[1] // ck
    h = h_ref[...]

    def gate_up(j):
        return _mm(h, wg_ref[:, j * ck:(j + 1) * ck]), _mm(h, wu_ref[:, j * ck:(j + 1) * ck])

    gu = gate_up(0)
    for j in range(n_chunks):
        g, u = gu
        if j + 1 < n_chunks:
            gu = gate_up(j + 1)
        a = (g * jax.nn.sigmoid(g) * u).astype(BF16)
        acc_ref[...] += _mm(a, wd_ref[j * ck:(j + 1) * ck, :])
    out = x + 0.5 * ga * acc_ref[...]
    if final:
        ms = jnp.mean(out * out, axis=-1, keepdims=True)
        out = out * lax.rsqrt(ms + RMS_EPS) * gf_ref[...]
    o_ref[0] = out


def _ffn(x, mod3, g, wg, wu, wd, g_final, *, k_mod, final, tm, ck):
    B, T, D = x.shape
    F = wg.shape[1]
    assert F % ck == 0
    const2 = lambda b, t: (0, 0)
    return pl.pallas_call(
        functools.partial(_ffn_kernel, k_mod=k_mod, final=final, ck=ck),
        grid=(B, T // tm),
        in_specs=[
            pl.BlockSpec((1, tm, D), lambda b, t: (b, t, 0)),
            pl.BlockSpec((1, N_MOD, D), lambda b, t: (b, 0, 0)),
            pl.BlockSpec((1, D), lambda b, t: (0, 0)),
            pl.BlockSpec((D, F), const2, pipeline_mode=pl.Buffered(1)),
            pl.BlockSpec((D, F), const2, pipeline_mode=pl.Buffered(1)),
            pl.BlockSpec((F, D), const2, pipeline_mode=pl.Buffered(1)),
            pl.BlockSpec((1, D), lambda b, t: (0, 0)),
        ],
        out_specs=pl.BlockSpec((1, tm, D), lambda b, t: (b, t, 0)),
        out_shape=jax.ShapeDtypeStruct((B, T, D), F32),
        scratch_shapes=[pltpu.VMEM((tm, D), BF16), pltpu.VMEM((tm, D), F32)],
        compiler_params=_params(("arbitrary", "arbitrary")),
        name="ffn_final" if final else "ffn",
    )(x, mod3, g.reshape(1, D), wg, wu, wd, g_final.reshape(1, D))


def _inproj_kernel(x_ref, mod_ref, g_ref, pos_ref, invf_ref, bf_ref, sel_ref, oneq_ref, onek_ref, tri_ref, w_ref,
                   qn_ref, kcr_ref, vcr_ref, ks_ref, vsT_ref, kw_ref, vwT_ref, qf_ref, kf_ref, vfT_ref, gT_ref,
                   h_ref, carry_ref, *, tm):
    t_idx = pl.program_id(1)
    sh = mod_ref[0, 3:4, :]
    sc = mod_ref[0, 4:5, :]
    h_ref[...] = _rms_mod(x_ref[0], g_ref[...], sh, sc).astype(BF16)
    h = h_ref[...]

    lane = lax.broadcasted_iota(jnp.int32, (tm, LANES), 1)
    low = lane < HEAD_DIM
    first8 = (lane & (HEAD_DIM - 1)) < (ROPE_DIM // 2)
    ang = invf_ref[...] * pos_ref[0]
    cos8 = jnp.cos(ang)
    sin8 = jnp.sin(ang)
    rest = HEAD_DIM - ROPE_DIM
    cos_t = jnp.concatenate([cos8, cos8, jnp.ones((rest, tm), F32)] * 2, axis=0).T
    sin_t = jnp.concatenate([-sin8, sin8, jnp.zeros((rest, tm), F32)] * 2, axis=0).T

    def rope(xs):
        partner = jnp.where(first8, pltpu.roll(xs, LANES - ROPE_DIM // 2, 1), pltpu.roll(xs, ROPE_DIM // 2, 1))
        return xs * cos_t + partner * sin_t

    def split_heads(xs):
        return jnp.where(low, xs, 0.0), jnp.where(low, pltpu.roll(xs, HEAD_DIM, 1), 0.0)

    def with_ones(vt):
        extra = (lax.broadcasted_iota(jnp.int32, (V_ROWS - HEAD_DIM, vt.shape[1]), 0) == 0).astype(BF16)
        return jnp.concatenate([vt, extra], axis=0)

    sm = _mm(h, w_ref[:, _OFF_SMALL:_OFF_SMALL + LANES])
    gT_ref[0] = jax.nn.sigmoid(sm).T[:_GATE_COLS, :]
    xl = sm + bf_ref[...]
    logf = jnp.minimum(xl, 0.0) - jnp.log1p(jnp.exp(-jnp.abs(xl)))
    tri = tri_ref[...]
    l0, l1, l2 = _split3(logf)

    @pl.when(t_idx == 0)
    def _():
        carry_ref[...] = jnp.zeros_like(carry_ref)

    cf = _mm(tri, l0) + _mm(tri, l1) + _mm(tri, l2) + carry_ref[...]
    carry_ref[...] = cf[tm - 1:tm, :]
    pieces = _split3(cf * LOG2E)
    placed = _mm(pieces[0], sel_ref[0]) + _mm(pieces[1], sel_ref[1]) + _mm(pieces[2], sel_ref[2])
    q_tail = placed[:, :LANES]
    k_tail = placed[:, LANES:]

    qn = _mm(h, w_ref[:, _OFF_QN:_OFF_QN + D_NSA])
    scale = HEAD_DIM ** -0.5 * LOG2E
    for j in range(D_NSA // LANES):
        a, b = split_heads(rope(qn[:, j * LANES:(j + 1) * LANES]) * scale)
        qn_ref[0, 2 * j] = a.astype(BF16)
        qn_ref[0, 2 * j + 1] = b.astype(BF16)

    nsa_tk = vsT_ref.shape[-1]
    kv = _mm(h, w_ref[:, _OFF_KV:_OFF_KV + N_KV_STREAMS * D_KV])
    kcr_ref[0] = rope(kv[:, 0:LANES])
    vcr_ref[0] = kv[:, LANES:2 * LANES]
    blk = lax.shift_right_logical(t_idx * tm + lax.broadcasted_iota(jnp.int32, (tm, LANES), 0), SLC_BLOCK.bit_length() - 1)
    onehot = (lane == blk + HEAD_DIM).astype(F32)
    for k_out, v_out, off, tail in ((ks_ref, vsT_ref, 2 * LANES, onehot), (kw_ref, vwT_ref, 4 * LANES, None)):
        a, b = split_heads(rope(kv[:, off:off + LANES]))
        if tail is not None:
            a, b = a + tail, b + tail
        k_out[0, 0] = a.astype(BF16)
        k_out[0, 1] = b.astype(BF16)
        vT = kv[:, off + LANES:off + 2 * LANES].T.astype(BF16)
        for gi in range(N_KV_NSA):
            for c in range(tm // nsa_tk):
                v_out[0, gi, c] = with_ones(vT[gi * HEAD_DIM:(gi + 1) * HEAD_DIM, c * nsa_tk:(c + 1) * nsa_tk])

    qf = _mm(h, w_ref[:, _OFF_QF:_OFF_QF + D_FOX])
    kf = _mm(h, w_ref[:, _OFF_KF:_OFF_KF + D_FOX])
    for j in range(D_FOX // LANES):
        qa, qb = split_heads(qf[:, j * LANES:(j + 1) * LANES] * scale)
        ka, kb = split_heads(kf[:, j * LANES:(j + 1) * LANES])
        for hd, qh, kh in ((2 * j, qa, ka), (2 * j + 1, qb, kb)):
            qf_ref[0, hd] = (qh + q_tail + oneq_ref[hd:hd + 1, :]).astype(BF16)
            kf_ref[0, hd] = (kh + k_tail + onek_ref[hd:hd + 1, :]).astype(BF16)
    vf = _mm(h, w_ref[:, _OFF_VF:_OFF_VF + D_FOX])
    fox_tk = vfT_ref.shape[-1]
    for j in range(D_FOX // LANES):
        vT = vf[:, j * LANES:(j + 1) * LANES].T.astype(BF16)
        for hh in range(2):
            for c in range(tm // fox_tk):
                vfT_ref[0, 2 * j + hh, c] = with_ones(vT[hh * HEAD_DIM:(hh + 1) * HEAD_DIM, c * fox_tk:(c + 1) * fox_tk])


def _inproj(x, mod3, g, pos_row, invf, bf_row, sel, oneq, onek, tri, w, *, tm, nsa_tk, fox_tk):
    B, T, D = x.shape
    H, G = N_HEADS_NSA, N_KV_NSA
    bt = lambda b, t: (b, t, 0)
    hb = lambda b, t: (b, 0, t, 0)
    vb = lambda b, t: (b, 0, t, 0, 0)
    out_shape = [
        jax.ShapeDtypeStruct((B, H, T, LANES), BF16),
        jax.ShapeDtypeStruct((B, T, LANES), F32),
        jax.ShapeDtypeStruct((B, T, LANES), F32),
        jax.ShapeDtypeStruct((B, G, T, LANES), BF16),
        jax.ShapeDtypeStruct((B, G, T // nsa_tk, V_ROWS, nsa_tk), BF16),
        jax.ShapeDtypeStruct((B, G, T, LANES), BF16),
        jax.ShapeDtypeStruct((B, G, T // nsa_tk, V_ROWS, nsa_tk), BF16),
        jax.ShapeDtypeStruct((B, N_HEADS_FOX, T, LANES), BF16),
        jax.ShapeDtypeStruct((B, N_HEADS_FOX, T, LANES), BF16),
        jax.ShapeDtypeStruct((B, N_HEADS_FOX, T // fox_tk, V_ROWS, fox_tk), BF16),
        jax.ShapeDtypeStruct((B, _GATE_COLS, T), F32),
    ]
    out_specs = [
        pl.BlockSpec((1, H, tm, LANES), hb),
        pl.BlockSpec((1, tm, LANES), bt),
        pl.BlockSpec((1, tm, LANES), bt),
        pl.BlockSpec((1, G, tm, LANES), hb),
        pl.BlockSpec((1, G, tm // nsa_tk, V_ROWS, nsa_tk), vb),
        pl.BlockSpec((1, G, tm, LANES), hb),
        pl.BlockSpec((1, G, tm // nsa_tk, V_ROWS, nsa_tk), vb),
        pl.BlockSpec((1, N_HEADS_FOX, tm, LANES), hb),
        pl.BlockSpec((1, N_HEADS_FOX, tm, LANES), hb),
        pl.BlockSpec((1, N_HEADS_FOX, tm // fox_tk, V_ROWS, fox_tk), vb),
        pl.BlockSpec((1, _GATE_COLS, tm), lambda b, t: (b, 0, t)),
    ]
    row = lambda b, t: (0, 0)
    return pl.pallas_call(
        functools.partial(_inproj_kernel, tm=tm),
        grid=(B, T // tm),
        in_specs=[
            pl.BlockSpec((1, tm, D), bt),
            pl.BlockSpec((1, N_MOD, D), lambda b, t: (b, 0, 0)),
            pl.BlockSpec((1, D), row),
            pl.BlockSpec((1, 1, tm), lambda b, t: (b, 0, t)),
            pl.BlockSpec((ROPE_DIM // 2, tm), row),
            pl.BlockSpec((1, LANES), row),
            pl.BlockSpec((3, LANES, 2 * LANES), lambda b, t: (0, 0, 0)),
            pl.BlockSpec((N_HEADS_FOX, LANES), row),
            pl.BlockSpec((N_HEADS_FOX, LANES), row),
            pl.BlockSpec((tm, tm), row),
            pl.BlockSpec((D, _W_IN_COLS), row, pipeline_mode=pl.Buffered(1)),
        ],
        out_specs=out_specs,
        out_shape=out_shape,
        scratch_shapes=[pltpu.VMEM((tm, D), BF16), pltpu.VMEM((1, LANES), F32)],
        compiler_params=_params(("arbitrary", "arbitrary")),
        name="inproj",
    )(x, mod3, g.reshape(1, D), pos_row, invf, bf_row, sel, oneq, onek, tri, w)


def _gelu_tanh(x):
    c = np.float32(np.sqrt(2.0 / np.pi))
    return x * (0.5 * (1.0 + jnp.tanh(c * (x + 0.044715 * (x * x * x)))))


def _compress_kernel(zk_ref, zv_ref, pe_ref, wkt_ref, wkb_ref, wvt_ref, wvb_ref, w2k_ref, w2v_ref, ovT_ref, kc_ref, lhs_ref):
    nsub = zk_ref.shape[1] // CMP_STRIDE

    def mlp(z_ref, pe_top, pe_bot, wt_ref, wb_ref):
        a = b = None
        for j in range(CMP_STRIDE):
            xj = z_ref[0, pl.ds(j, nsub, stride=CMP_STRIDE), :]
            lanes = slice(j * LANES, (j + 1) * LANES)
            aj = _mm((xj + pe_ref[pe_top:pe_top + 1, lanes]).astype(BF16), wt_ref[lanes, :])
            bj = _mm((xj + pe_ref[pe_bot:pe_bot + 1, lanes]).astype(BF16), wb_ref[lanes, :])
            a = aj if a is None else a + aj
            b = bj if b is None else b + bj
        return _gelu_tanh(a + pltpu.roll(b, nsub - 1, 0))

    hk = mlp(zk_ref, 0, 1, wkt_ref, wkb_ref)
    hv = mlp(zv_ref, 2, 3, wvt_ref, wvb_ref)
    for gi in range(N_KV_NSA):
        kc_ref[0, gi] = _mm(hk[:, gi * CMP_HIDDEN:(gi + 1) * CMP_HIDDEN].astype(BF16), w2k_ref[...]).astype(BF16)
        vc = _mm(hv[:, gi * CMP_HIDDEN:(gi + 1) * CMP_HIDDEN].astype(BF16), w2v_ref[...])
        ones = (lax.broadcasted_iota(jnp.int32, (V_ROWS - HEAD_DIM, nsub), 0) == 0).astype(BF16)
        lhs_ref[0, gi] = jnp.concatenate([vc.T[:HEAD_DIM, :].astype(BF16), ones, ovT_ref[...]], axis=0)


def _compress(zk, zv, pe4, wkt, wkb, wvt, wvb, w2k, w2v, ovT):
    B, T, _ = zk.shape
    nsub = T // CMP_STRIDE
    zc = CMP_STRIDE * LANES
    G = N_KV_NSA
    c2 = lambda b: (0, 0)
    zspec = pl.BlockSpec((1, T, LANES), lambda b: (b, 0, 0))
    wspec = pl.BlockSpec((zc, G * CMP_HIDDEN), c2)
    w2spec = pl.BlockSpec((CMP_HIDDEN, LANES), c2)
    return pl.pallas_call(
        _compress_kernel,
        grid=(B,),
        in_specs=[zspec, zspec, pl.BlockSpec((4, zc), c2), wspec, wspec, wspec, wspec, w2spec, w2spec,
                  pl.BlockSpec(ovT.shape, c2)],
        out_specs=[
            pl.BlockSpec((1, G, nsub, LANES), lambda b: (b, 0, 0, 0)),
            pl.BlockSpec((1, G, V_ROWS + ovT.shape[0], nsub), lambda b: (b, 0, 0, 0)),
        ],
        out_shape=[
            jax.ShapeDtypeStruct((B, G, nsub, LANES), BF16),
            jax.ShapeDtypeStruct((B, G, V_ROWS + ovT.shape[0], nsub), BF16),
        ],
        compiler_params=_params(("arbitrary",)),
        name="compress",
    )(zk, zv, pe4, wkt, wkb, wvt, wvb, w2k, w2v, ovT)


_SCORE_PAD = LANES


def _flash_init(m_ref, acc_ref):
    m_ref[...] = jnp.full_like(m_ref, NEG_INF)
    acc_ref[...] = jnp.zeros_like(acc_ref)


def _chain_softmax(s_list, m_ref, tile_max=None):
    m_old = m_ref[...]
    m_new = m_old
    if tile_max is not None:
        m_new = jnp.maximum(m_new, tile_max)
    else:
        for s in s_list:
            m_new = jnp.maximum(m_new, jnp.max(s, axis=0, keepdims=True))
    m_ref[...] = m_new
    return jnp.exp2(m_old - m_new), [jnp.exp2(s - m_new).astype(BF16) for s in s_list]


def _chain_values(staged, v_list, acc_ref):
    alpha, ps = staged
    pv = _mm(v_list[0], ps[0])
    for v, p in zip(v_list[1:], ps[1:]):
        pv = pv + _mm(v, p)
    acc_ref[...] = alpha * acc_ref[...] + pv


def _normalized(acc):
    return acc[:HEAD_DIM] / acc[HEAD_DIM:HEAD_DIM + 1]


def _pipelined_sweep(n, n_chains, store, softmax, values):
    chains = range(n_chains)

    def by_parity(i, fn):
        @pl.when((i & 1) == 0)
        def _():
            fn(0)

        @pl.when((i & 1) == 1)
        def _():
            fn(1)

    def step(nxt, masked, cur, par):
        staged = None
        for c in chains:
            if nxt is not None:
                store(nxt, 1 - par, masked, c)
            new = softmax(cur, par, c)
            if staged is not None:
                values(cur, staged, c - 1)
            staged = new
        values(cur, staged, n_chains - 1)

    def first_scores(masked):
        for c in chains:
            store(0, 0, masked, c)

    @pl.when(n == 0)
    def _():
        first_scores(True)
        step(None, False, 0, 0)

    @pl.when(n == 1)
    def _():
        first_scores(False)
        step(1, True, 0, 0)
        step(None, False, 1, 1)

    @pl.when(n >= 2)
    def _():
        first_scores(False)
        step(1, False, 0, 0)
        n_mid = n - 2

        def body(j, carry):
            step(2 * j + 2, False, 2 * j + 1, 1)
            step(2 * j + 3, False, 2 * j + 2, 0)
            return carry

        lax.fori_loop(0, n_mid // 2, body, 0)

        @pl.when((n_mid & 1) == 1)
        def _():
            step(n_mid + 1, False, n_mid, 1)

        def last_two(par):
            step(n, True, n - 1, par)
            step(None, False, n, 1 - par)

        by_parity(n - 1, last_two)


_FOX_HEADS_PER_STEP = 8


def _fox_kernel(q_ref, k_ref, vT_ref, o_ref, m_ref, acc_ref, s0_ref, s1_ref, x0_ref, x1_ref, *, tq, tk):
    qi = pl.program_id(2)
    nkc = tq // tk
    heads = range(_FOX_HEADS_PER_STEP)
    s_bufs = (s0_ref, s1_ref)
    x_bufs = (x0_ref, x1_ref)
    row = lax.broadcasted_iota(jnp.int32, (tk, tq), 0)
    lane = lax.broadcasted_iota(jnp.int32, (tk, tq), 1)
    for hh in heads:
        _flash_init(m_ref.at[hh], acc_ref.at[hh])

    def store(kt, buf, masked, hh):
        tile_max = None
        for c in range(nkc):
            k0 = pl.multiple_of(kt * tq + c * tk, tk)
            s = _nt(k_ref[0, hh, pl.ds(k0, tk), :], q_ref[0, hh])
            if masked:
                s = jnp.where(row + c * tk <= lane, s, NEG_INF)
            s_bufs[buf][hh, c * tk:(c + 1) * tk, :tq] = s
            cm = jnp.max(s, axis=0, keepdims=True)
            tile_max = cm if tile_max is None else jnp.maximum(tile_max, cm)
        x_bufs[buf][hh] = tile_max

    def softmax(kt, buf, hh):
        return _chain_softmax([s_bufs[buf][hh, c * tk:(c + 1) * tk, :tq] for c in range(nkc)], m_ref.at[hh], x_bufs[buf][hh])

    def values(kt, staged, hh):
        _chain_values(staged, [vT_ref[0, hh, kt * nkc + c] for c in range(nkc)], acc_ref.at[hh])

    _pipelined_sweep(qi, len(heads), store, softmax, values)
    for pr in range(_FOX_HEADS_PER_STEP // 2):
        st = jnp.concatenate([_normalized(acc_ref[2 * pr]), _normalized(acc_ref[2 * pr + 1])], axis=0)
        o_ref[0, :, pr * LANES:(pr + 1) * LANES] = st.T.astype(BF16)


def _fox(qf, kf, vfT, *, tq):
    B, H, T, _ = qf.shape
    tk = vfT.shape[-1]
    nh = _FOX_HEADS_PER_STEP
    return pl.pallas_call(
        functools.partial(_fox_kernel, tq=tq, tk=tk),
        grid=(B, H // nh, T // tq),
        in_specs=[
            pl.BlockSpec((1, nh, tq, LANES), lambda b, p, i: (b, p, i, 0)),
            pl.BlockSpec((1, nh, T, LANES), lambda b, p, i: (b, p, 0, 0)),
            pl.BlockSpec((1, nh, T // tk, V_ROWS, tk), lambda b, p, i: (b, p, 0, 0, 0)),
        ],
        out_specs=pl.BlockSpec((1, tq, nh * HEAD_DIM), lambda b, p, i: (b, i, p)),
        out_shape=jax.ShapeDtypeStruct((B, T, H * HEAD_DIM), BF16),
        scratch_shapes=[
            pltpu.VMEM((nh, 1, tq), F32),
            pltpu.VMEM((nh, V_ROWS, tq), F32),
            pltpu.VMEM((nh, tq, tq + _SCORE_PAD), F32),
            pltpu.VMEM((nh, tq, tq + _SCORE_PAD), F32),
            pltpu.VMEM((nh, 1, tq), F32),
            pltpu.VMEM((nh, 1, tq), F32),
        ],
        compiler_params=_params(("arbitrary", "arbitrary", "arbitrary")),
        name="fox",
    )(qf, kf, vfT)


def _softmax_pv(s_lists, v_lists):
    staged = []
    for s_list in s_lists:
        m = jnp.max(s_list[0], axis=0, keepdims=True)
        for s in s_list[1:]:
            m = jnp.maximum(m, jnp.max(s, axis=0, keepdims=True))
        staged.append([jnp.exp2(s - m).astype(BF16) for s in s_list])
    outs = []
    for ps, v_list in zip(staged, v_lists):
        pv = _mm(v_list[0], ps[0])
        for v, p in zip(v_list[1:], ps[1:]):
            pv = pv + _mm(v, p)
        outs.append(_normalized(pv))
    return outs


_NSA_LANE_SPLITS = GROUP


def _nsa_kernel(q_ref, kc_ref, cmp_lhs_ref, ks_ref, vsT_ref, kw_ref, vwT_ref, g_ref, o_ref,
                m_ref, acc_ref, imp_ref, cnt_ref, qa_ref, oc_ref, ow_ref, s0_ref, s1_ref, x0_ref, x1_ref, *, tq, tk):
    qt = pl.program_id(1)
    G = N_KV_NSA
    nq = GROUP * tq
    hw = nq // _NSA_LANE_SPLITS
    heads_per_split = GROUP // _NSA_LANE_SPLITS
    ncp = kc_ref.shape[2]
    ns = cmp_lhs_ref.shape[2] - V_ROWS
    blk_shift = SLC_BLOCK.bit_length() - 1
    t_row = qt * tq + (lax.broadcasted_iota(jnp.int32, (1, nq), 1) & (tq - 1))
    ql = lax.broadcasted_iota(jnp.int32, (1, hw), 1) & (tq - 1)
    row_k = lax.broadcasted_iota(jnp.int32, (tk, hw), 0)
    causal = row_k <= ql
    n_io = lax.broadcasted_iota(jnp.int32, (ns, tq), 0)
    cur = lax.shift_right_logical(qt * tq + lax.broadcasted_iota(jnp.int32, (ns, tq), 1), blk_shift)
    forced = (n_io == 0) | (n_io == cur) | (n_io == cur - 1)
    visible = n_io <= cur
    c_io = lax.broadcasted_iota(jnp.int32, (ncp, nq), 0)
    cmp_valid = (c_io * CMP_STRIDE + (CMP_BLOCK - 1)) <= t_row
    chains = [(g, h) for g in range(G) for h in range(_NSA_LANE_SPLITS)]

    def q_of(g):
        return q_ref[0, g * GROUP:(g + 1) * GROUP].reshape(nq, LANES)

    def q_split(g, h):
        h0 = g * GROUP + h * heads_per_split
        return q_ref[0, h0:h0 + heads_per_split].reshape(hw, LANES)

    def lanes_of(h):
        return pl.ds(h * hw, hw)

    cmp_ok = cmp_valid[:, :hw]
    s_cmp = [_nt(kc_ref[0, g], q_split(g, h)) for g, h in chains]
    p_cmp = []
    for sc in s_cmp:
        smk = jnp.where(cmp_ok, sc, NEG_INF)
        p_cmp.append(jnp.exp2(smk - jnp.max(smk, axis=0, keepdims=True)).astype(BF16))
    any_valid = t_row[:, :hw] >= (CMP_BLOCK - 1)
    imp_sum = [None] * G
    for (g, h), pc in zip(chains, p_cmp):
        r_all = _mm(cmp_lhs_ref[0, g], pc)
        inv = jnp.where(any_valid, 1.0 / r_all[HEAD_DIM:HEAD_DIM + 1], 0.0)
        oc_ref[g, :, lanes_of(h)] = r_all[:HEAD_DIM] * inv
        imp_h = r_all[V_ROWS:] * inv
        imp_sum[g] = imp_h if imp_sum[g] is None else imp_sum[g] + imp_h
    for g in range(G):
        imp_ref[g] = jnp.where(forced, FORCE_SCORE, jnp.where(visible, imp_sum[g], -1.0))

    n_back = WINDOW // tk
    hq = tq // 2
    row_h = lax.broadcasted_iota(jnp.int32, (hq, hq), 0)
    lane_h = lax.broadcasted_iota(jnp.int32, (hq, hq), 1)

    @pl.when(qt >= n_back)
    def _():
        base = pl.multiple_of((qt - 1) * tk, tk)
        s_lists, v_lists, where_to = [], [], []
        for g, h in chains:
            for sub in range(2):
                q = q_ref[0, g * GROUP + h, sub * hq:(sub + 1) * hq, :]
                s3 = _nt(kw_ref[0, g, pl.ds(base + sub * hq, 3 * hq), :], q)
                s_lists.append([jnp.where(row_h > lane_h, s3[:hq], NEG_INF),
                                s3[hq:2 * hq],
                                jnp.where(row_h <= lane_h, s3[2 * hq:], NEG_INF)])
                halves = [vwT_ref[0, g, qt - 1, :, :hq], vwT_ref[0, g, qt - 1, :, hq:],
                          vwT_ref[0, g, qt, :, :hq], vwT_ref[0, g, qt, :, hq:]]
                v_lists.append(halves[sub:sub + 3])
                where_to.append((g, pl.ds(h * tq + sub * hq, hq)))
        for (g, lanes), o in zip(where_to, _softmax_pv(s_lists, v_lists)):
            ow_ref[g, :, lanes] = o

    @pl.when(qt < n_back)
    def _():
        s_lists = []
        for g, h in chains:
            q = q_split(g, h)
            s_lists.append([jnp.where((row_k + j * tk) <= (qt * tq + ql), _nt(kw_ref[0, g, j * tk:(j + 1) * tk, :], q), NEG_INF)
                            for j in range(n_back)])
        v_lists = [[vwT_ref[0, g, j] for j in range(n_back)] for g, h in chains]
        for (g, h), o in zip(chains, _softmax_pv(s_lists, v_lists)):
            ow_ref[g, :, lanes_of(h)] = o

    n_vis = jnp.minimum((qt + 1) * (tq // SLC_BLOCK), ns)
    sub = 8
    sub_io = lax.broadcasted_iota(jnp.int32, (sub, tq), 0)
    cnt_ref[...] = jnp.zeros_like(cnt_ref)
    for mb in range(ns // sub):
        @pl.when(mb * sub < n_vis)
        def _():
            for g in range(G):
                groups = [imp_ref[g, j * sub:(j + 1) * sub, :] for j in range(ns // sub)]
                counts = [cnt_ref[g, j * sub:(j + 1) * sub, :] for j in range(ns // sub)]
                for mi in range(sub):
                    row = jnp.broadcast_to(groups[mb][mi:mi + 1, :], (sub, tq))
                    for j in range(ns // sub):
                        if j < mb:
                            beats = row > groups[j]
                        elif j > mb:
                            beats = row >= groups[j]
                        else:
                            beats = (row > groups[j]) | ((row >= groups[j]) & (sub_io > mi))
                        counts[j] = counts[j] + beats.astype(F32)
                for j in range(ns // sub):
                    cnt_ref[g, j * sub:(j + 1) * sub, :] = counts[j]

    for g in range(G):
        bias = jnp.where(cnt_ref[g] < float(TOP_N), 0.0, NEG_INF)
        parts = [jnp.zeros((HEAD_DIM, tq), F32), bias]
        if ns < LANES - HEAD_DIM:
            parts.append(jnp.zeros((LANES - HEAD_DIM - ns, tq), F32))
        bias_t = jnp.concatenate(parts, axis=0).T.astype(BF16)
        qa_ref[g] = q_of(g) + jnp.concatenate([bias_t] * GROUP, axis=0)
        _flash_init(m_ref.at[g], acc_ref.at[g])

    slc_refs = [(m_ref.at[g, :, lanes_of(h)], acc_ref.at[g, :, lanes_of(h)]) for g, h in chains]

    s_bufs = (s0_ref, s1_ref)
    x_bufs = (x0_ref, x1_ref)

    def slc_store(kj, buf, masked, c):
        g, h = chains[c]
        k0 = pl.multiple_of(kj * tk, tk)
        s = _nt(ks_ref[0, g, pl.ds(k0, tk), :], qa_ref[g, lanes_of(h), :])
        if masked:
            s = jnp.where(causal, s, NEG_INF)
        s_bufs[buf][c, :, :hw] = s
        x_bufs[buf][g, :, lanes_of(h)] = jnp.max(s, axis=0, keepdims=True)

    def slc_softmax(kj, buf, c):
        g, h = chains[c]
        return _chain_softmax([s_bufs[buf][c, :, :hw]], slc_refs[c][0], x_bufs[buf][g, :, lanes_of(h)])

    def slc_values(kj, staged, c):
        _chain_values(staged, [vsT_ref[0, chains[c][0], kj]], slc_refs[c][1])

    _pipelined_sweep(qt, len(chains), slc_store, slc_softmax, slc_values)

    for g in range(G):
        def gate_row(j):
            return jnp.concatenate([g_ref[0, g, 3 * r + j:3 * r + j + 1, :] for r in range(GROUP)], axis=1)

        oT = gate_row(0) * oc_ref[g] + gate_row(1) * _normalized(acc_ref[g]) + gate_row(2) * ow_ref[g]
        for pr in range(GROUP // 2):
            st = jnp.concatenate([oT[:, (2 * pr) * tq:(2 * pr + 1) * tq], oT[:, (2 * pr + 1) * tq:(2 * pr + 2) * tq]], axis=0)
            col = (g * GROUP // 2 + pr) * LANES
            o_ref[0, :, col:col + LANES] = st.T.astype(BF16)


def _nsa(qn, kc, cmp_lhs, ks, vsT, kw, vwT, gT4, *, tq):
    B, H, T, _ = qn.shape
    G = N_KV_NSA
    tk = vsT.shape[-1]
    ncp = kc.shape[2]
    ns = cmp_lhs.shape[2] - V_ROWS
    assert tq == tk and WINDOW == tk and _NSA_LANE_SPLITS == GROUP and T >= WINDOW and ns <= LANES - HEAD_DIM and ns % 8 == 0
    nq = GROUP * tq
    kvspec = pl.BlockSpec((1, G, T, LANES), lambda b, i: (b, 0, 0, 0))
    vtspec = pl.BlockSpec((1, G, T // tk, V_ROWS, tk), lambda b, i: (b, 0, 0, 0, 0))
    return pl.pallas_call(
        functools.partial(_nsa_kernel, tq=tq, tk=tk),
        grid=(B, T // tq),
        in_specs=[
            pl.BlockSpec((1, H, tq, LANES), lambda b, i: (b, 0, i, 0)),
            pl.BlockSpec((1, G, ncp, LANES), lambda b, i: (b, 0, 0, 0)),
            pl.BlockSpec((1, G, V_ROWS + ns, ncp), lambda b, i: (b, 0, 0, 0)),
            kvspec, vtspec, kvspec, vtspec,
            pl.BlockSpec((1, G, 3 * GROUP, tq), lambda b, i: (b, 0, 0, i)),
        ],
        out_specs=pl.BlockSpec((1, tq, H * HEAD_DIM), lambda b, i: (b, i, 0)),
        out_shape=jax.ShapeDtypeStruct((B, T, H * HEAD_DIM), BF16),
        scratch_shapes=[
            pltpu.VMEM((G, 1, nq), F32),
            pltpu.VMEM((G, V_ROWS, nq), F32),
            pltpu.VMEM((G, ns, tq), F32),
            pltpu.VMEM((G, ns, tq), F32),
            pltpu.VMEM((G, nq, LANES), BF16),
            pltpu.VMEM((G, HEAD_DIM, nq), F32),
            pltpu.VMEM((G, HEAD_DIM, nq), F32),
            pltpu.VMEM((G * _NSA_LANE_SPLITS, tk, tq + _SCORE_PAD), F32),
            pltpu.VMEM((G * _NSA_LANE_SPLITS, tk, tq + _SCORE_PAD), F32),
            pltpu.VMEM((G, 1, nq), F32),
            pltpu.VMEM((G, 1, nq), F32),
        ],
        compiler_params=_params(("arbitrary", "arbitrary")),
        name="nsa",
    )(qn, kc, cmp_lhs, ks, vsT, kw, vwT, gT4)


def _mixout_kernel(x_ref, mod_ref, g_ref, on_ref, of_ref, wgm_ref, wun_ref, wuf_ref, wo_ref, o_ref):
    x = x_ref[0]
    D = x.shape[-1]
    sh = mod_ref[0, 3:4, :]
    sc = mod_ref[0, 4:5, :]
    ga = mod_ref[0, 5:6, :]
    h = _rms_mod(x, g_ref[...], sh, sc).astype(BF16)
    gm = _mm(h, wgm_ref[...])
    un = _mm(on_ref[0], wun_ref[...])
    uf = _mm(of_ref[0], wuf_ref[...])
    merged = jax.nn.sigmoid(gm[:, :D]) * un + jax.nn.sigmoid(gm[:, D:]) * uf
    y = _mm(merged.astype(BF16), wo_ref[...])
    o_ref[0] = x + ga * y


def _mixout(x, mod3, g, o_nsa, o_fox, wgm, wun, wuf, wo, *, tm):
    B, T, D = x.shape
    bt = lambda b, t: (b, t, 0)
    c2 = lambda b, t: (0, 0)
    dn = o_nsa.shape[-1]
    return pl.pallas_call(
        _mixout_kernel,
        grid=(B, T // tm),
        in_specs=[
            pl.BlockSpec((1, tm, D), bt),
            pl.BlockSpec((1, N_MOD, D), lambda b, t: (b, 0, 0)),
            pl.BlockSpec((1, D), c2),
            pl.BlockSpec((1, tm, dn), bt),
            pl.BlockSpec((1, tm, dn), bt),
            pl.BlockSpec((D, 2 * D), c2, pipeline_mode=pl.Buffered(1)),
            pl.BlockSpec((dn, D), c2, pipeline_mode=pl.Buffered(1)),
            pl.BlockSpec((dn, D), c2, pipeline_mode=pl.Buffered(1)),
            pl.BlockSpec((D, D), c2, pipeline_mode=pl.Buffered(1)),
        ],
        out_specs=pl.BlockSpec((1, tm, D), bt),
        out_shape=jax.ShapeDtypeStruct((B, T, D), F32),
        compiler_params=_params(("arbitrary", "arbitrary")),
        name="mixout",
    )(x, mod3, g.reshape(1, D), o_nsa, o_fox, wgm, wun, wuf, wo)


def _compress_weights(pe, w1, w2):
    half = CMP_BLOCK // 2
    eye = jnp.eye(N_KV_NSA, dtype=F32)

    def expand(w_half):
        w3 = w_half.reshape(half, HEAD_DIM, CMP_HIDDEN)
        return jnp.einsum('jdn,gh->jgdhn', w3, eye).reshape(half * N_KV_NSA * HEAD_DIM, N_KV_NSA * CMP_HIDDEN).astype(BF16)

    def pe_row(pe_half):
        return jnp.broadcast_to(pe_half[:, None, :], (half, N_KV_NSA, HEAD_DIM)).reshape(1, -1)

    w2p = jnp.pad(w2, ((0, 0), (0, LANES - HEAD_DIM))).astype(BF16)
    return (pe_row(pe[:half]), pe_row(pe[half:]), expand(w1[:half * HEAD_DIM]), expand(w1[half * HEAD_DIM:]), w2p)


def kernel(x, c, positions, w_ada, b_ada, g_ffn1, w_gate1, w_up1, w_down1, g_mix, w_in, b_forget, pe_ck, w1_ck, w2_ck, pe_cv, w1_cv, w2_cv, w_up_nsa, w_up_fox, w_o, g_ffn2, w_gate2, w_up2, w_down2, g_final):
    B, T, D = x.shape
    depth = w_ada.shape[0]
    tm = 512
    tm_dense = 1024 if T % 1024 == 0 else tm
    ffn_ck = 256
    nsa_tq, nsa_tk = 512, 512
    fox_tq, fox_tk = 512, 512
    n_slc = T // SLC_BLOCK
    n_sub = T // CMP_STRIDE

    half = ROPE_DIM // 2
    inv_freq = ROPE_THETA ** (-jnp.arange(half, dtype=F32) / half)
    invf = jnp.broadcast_to(inv_freq[:, None], (half, tm))
    cmp_start = np.arange(n_sub) * CMP_STRIDE
    slc_start = np.arange(n_slc) * SLC_BLOCK
    ov = ((cmp_start[:, None] < slc_start[None, :] + SLC_BLOCK) & (slc_start[None, :] < cmp_start[:, None] + CMP_BLOCK))
    ov[n_sub - CMP_BLOCK // CMP_STRIDE + 1:, :] = False
    ovT = jnp.asarray(ov.T, dtype=BF16)

    pos_row = positions.astype(F32)[:, None, :]
    sel_np = np.zeros((3, LANES, 2 * LANES), np.float32)
    oneq_np = np.zeros((N_HEADS_FOX, LANES), np.float32)
    onek_np = np.zeros((N_HEADS_FOX, LANES), np.float32)
    for hd in range(N_HEADS_FOX):
        for j in range(3):
            sel_np[j, _GATE_COLS + hd, _TAIL + 3 * hd + j] = 1.0
            sel_np[j, _GATE_COLS + hd, LANES + _TAIL2 + 3 * hd + j] = -1.0
            oneq_np[hd, _TAIL2 + 3 * hd + j] = 1.0
            onek_np[hd, _TAIL + 3 * hd + j] = 1.0
    sel, oneq, onek = jnp.asarray(sel_np, BF16), jnp.asarray(oneq_np), jnp.asarray(onek_np)
    tri = jnp.asarray(np.tril(np.ones((tm, tm), np.float32)), BF16)
    c_in = c
    for l in range(depth):
        mod3 = _ada(c_in, w_ada[l], b_ada[l]).reshape(B, N_MOD, D)
        wg1, wu1, wd1 = w_gate1[l].astype(BF16), w_up1[l].astype(BF16), w_down1[l].astype(BF16)
        wg2, wu2, wd2 = w_gate2[l].astype(BF16), w_up2[l].astype(BF16), w_down2[l].astype(BF16)
        wl = w_in[l]
        c0 = D_NSA + N_KV_STREAMS * D_KV
        c1 = c0 + _GATE_COLS
        c2 = c1 + 3 * D_FOX
        c3 = c2 + N_HEADS_FOX
        small = jnp.concatenate([wl[:, c0:c1], wl[:, c2:c3], jnp.zeros((D, LANES - _GATE_COLS - N_HEADS_FOX), F32)], axis=1)
        w_proj = jnp.concatenate([wl[:, :c0], wl[:, c1:c2], small], axis=1).astype(BF16)
        w_gm = wl[:, c3:].astype(BF16)
        bf_row = jnp.zeros((1, LANES), F32).at[0, _GATE_COLS:_GATE_COLS + N_HEADS_FOX].set(b_forget[l])

        x = _ffn(x, mod3, g_ffn1[l], wg1, wu1, wd1, g_final, k_mod=0, final=False, tm=tm_dense, ck=ffn_ck)

        (qn, kcr, vcr, ks, vsT, kw, vwT, qf, kf, vfT, gT) = _inproj(
            x, mod3, g_mix[l], pos_row, invf, bf_row, sel, oneq, onek, tri, w_proj, tm=tm, nsa_tk=nsa_tk, fox_tk=fox_tk)

        pk_t, pk_b, wk_t, wk_b, w2k = _compress_weights(pe_ck[l], w1_ck[l], w2_ck[l])
        pv_t, pv_b, wv_t, wv_b, w2v = _compress_weights(pe_cv[l], w1_cv[l], w2_cv[l])
        pe4 = jnp.concatenate([pk_t, pk_b, pv_t, pv_b], axis=0)
        kc, cmp_lhs = _compress(kcr, vcr, pe4, wk_t, wk_b, wv_t, wv_b, w2k, w2v, ovT)

        o_fox = _fox(qf, kf, vfT, tq=fox_tq)
        gT4 = gT.reshape(B, N_KV_NSA, 3 * GROUP, T)
        o_nsa = _nsa(qn, kc, cmp_lhs, ks, vsT, kw, vwT, gT4, tq=nsa_tq)

        x = _mixout(x, mod3, g_mix[l], o_nsa, o_fox, w_gm, w_up_nsa[l].astype(BF16), w_up_fox[l].astype(BF16),
                    w_o[l].astype(BF16), tm=tm_dense)
        last = l == depth - 1
        x = _ffn(x, mod3, g_ffn2[l], wg2, wu2, wd2, g_final, k_mod=6, final=last, tm=tm_dense, ck=ffn_ck)
    return x
```

```python
import functools

import numpy as np
import jax
import jax.numpy as jnp
from jax import lax
from jax.experimental import pallas as pl
from jax.experimental.pallas import tpu as pltpu

HEAD_DIM = 64
N_HEADS_NSA = 8
N_KV_NSA = 2
GROUP = N_HEADS_NSA // N_KV_NSA
N_HEADS_FOX = 8
CMP_BLOCK = 32
CMP_STRIDE = 16
CMP_HIDDEN = 128
SLC_BLOCK = 64
TOP_N = 16
WINDOW = 512
ROPE_THETA = 500000.0
ROPE_DIM = HEAD_DIM // 4
N_MOD = 9
RMS_EPS = 1e-6
NEG_INF = -1e30
FORCE_SCORE = 1e4

LANES = 128
LOG2E = 1.4426950408889634
V_ROWS = HEAD_DIM + 16
F32 = jnp.float32
BF16 = jnp.bfloat16
VMEM_LIMIT = 56 * 1024 * 1024

D_NSA = N_HEADS_NSA * HEAD_DIM
D_KV = N_KV_NSA * HEAD_DIM
D_FOX = N_HEADS_FOX * HEAD_DIM
N_KV_STREAMS = 6
_GATE_COLS = 3 * N_HEADS_NSA
_OFF_QN = 0
_OFF_KV = _OFF_QN + D_NSA
_OFF_QF = _OFF_KV + N_KV_STREAMS * D_KV
_OFF_KF = _OFF_QF + D_FOX
_OFF_VF = _OFF_KF + D_FOX
_OFF_SMALL = _OFF_VF + D_FOX
_W_IN_COLS = _OFF_SMALL + LANES
_TAIL = HEAD_DIM
_TAIL2 = HEAD_DIM + 36


def _params(sem):
    return pltpu.CompilerParams(dimension_semantics=sem, vmem_limit_bytes=VMEM_LIMIT)


def _nt(a, b):
    return lax.dot_general(a, b, (((1,), (1,)), ((), ())), preferred_element_type=F32)


def _mm(a, b):
    return jnp.dot(a, b, preferred_element_type=F32)


def _split3(x):
    hi = x.astype(BF16)
    r = x - hi.astype(F32)
    mid = r.astype(BF16)
    lo = (r - mid.astype(F32)).astype(BF16)
    return hi, mid, lo


def _rms_mod(x, g, shift, scale):
    ms = jnp.mean(x * x, axis=-1, keepdims=True)
    y = x * lax.rsqrt(ms + RMS_EPS) * g
    return y * (1.0 + scale) + shift


def _ada_kernel(c_ref, w_ref, b_ref, o_ref):
    c = c_ref[...]
    ca = c * jax.nn.sigmoid(c)
    h0, h1, h2 = _split3(ca)
    w0, w1, w2 = _split3(w_ref[...])
    acc = _mm(h0, w0) + _mm(h0, w1) + _mm(h1, w0)
    acc = acc + (_mm(h1, w1) + _mm(h0, w2) + _mm(h2, w0))
    o_ref[...] = acc + b_ref[...]


def _ada(c, w_ada, b_ada):
    B, D = c.shape
    n = w_ada.shape[1]
    tn = 1024
    return pl.pallas_call(
        _ada_kernel,
        grid=(n // tn,),
        in_specs=[
            pl.BlockSpec((B, D), lambda j: (0, 0)),
            pl.BlockSpec((D, tn), lambda j: (0, j)),
            pl.BlockSpec((1, tn), lambda j: (0, j)),
        ],
        out_specs=pl.BlockSpec((B, tn), lambda j: (0, j)),
        out_shape=jax.ShapeDtypeStruct((B, n), F32),
        compiler_params=_params(("arbitrary",)),
        name="ada",
    )(c, w_ada, b_ada.reshape(1, n))


def _ffn_body(x, sh, sc, ga, g, wg_ref, wu_ref, wd_ref, gf, h_ref, acc_ref, *, final, ck):
    h_ref[...] = _rms_mod(x, g, sh, sc).astype(BF16)
    acc_ref[...] = jnp.zeros_like(acc_ref)

    n_chunks = wg_ref.shape[1] // ck
    h = h_ref[...]

    def gate_up(j):
        return _mm(h, wg_ref[:, j * ck:(j + 1) * ck]), _mm(h, wu_ref[:, j * ck:(j + 1) * ck])

    gu = gate_up(0)
    for j in range(n_chunks):
        gt, up = gu
        if j + 1 < n_chunks:
            gu = gate_up(j + 1)
        a = (gt * jax.nn.sigmoid(gt) * up).astype(BF16)
        acc_ref[...] += _mm(a, wd_ref[j * ck:(j + 1) * ck, :])
    out = x + 0.5 * ga * acc_ref[...]
    if final:
        ms = jnp.mean(out * out, axis=-1, keepdims=True)
        out = out * lax.rsqrt(ms + RMS_EPS) * gf
    return out


def _ffn_kernel(x_ref, mod_ref, g_ref, wg_ref, wu_ref, wd_ref, gf_ref, o_ref, h_ref, acc_ref, *, k_mod, final, ck):
    sh = mod_ref[0, k_mod:k_mod + 1, :]
    sc = mod_ref[0, k_mod + 1:k_mod + 2, :]
    ga = mod_ref[0, k_mod + 2:k_mod + 3, :]
    o_ref[0] = _ffn_body(x_ref[0], sh, sc, ga, g_ref[...], wg_ref, wu_ref, wd_ref, gf_ref[...], h_ref, acc_ref,
                         final=final, ck=ck)


def _ffn(x, mod3, g, wg, wu, wd, g_final, *, k_mod, final, tm, ck):
    B, T, D = x.shape
    F = wg.shape[1]
    assert F % ck == 0
    const2 = lambda b, t: (0, 0)
    return pl.pallas_call(
        functools.partial(_ffn_kernel, k_mod=k_mod, final=final, ck=ck),
        grid=(B, T // tm),
        in_specs=[
            pl.BlockSpec((1, tm, D), lambda b, t: (b, t, 0)),
            pl.BlockSpec((1, N_MOD, D), lambda b, t: (b, 0, 0)),
            pl.BlockSpec((1, D), lambda b, t: (0, 0)),
            pl.BlockSpec((D, F), const2, pipeline_mode=pl.Buffered(1)),
            pl.BlockSpec((D, F), const2, pipeline_mode=pl.Buffered(1)),
            pl.BlockSpec((F, D), const2, pipeline_mode=pl.Buffered(1)),
            pl.BlockSpec((1, D), lambda b, t: (0, 0)),
        ],
        out_specs=pl.BlockSpec((1, tm, D), lambda b, t: (b, t, 0)),
        out_shape=jax.ShapeDtypeStruct((B, T, D), F32),
        scratch_shapes=[pltpu.VMEM((tm, D), BF16), pltpu.VMEM((tm, D), F32)],
        compiler_params=_params(("arbitrary", "arbitrary")),
        name="ffn_final" if final else "ffn",
    )(x, mod3, g.reshape(1, D), wg, wu, wd, g_final.reshape(1, D))


def _inproj_kernel(x_ref, mod_ref, g_ref, pos_ref, invf_ref, bf_ref, sel_ref, oneq_ref, onek_ref, tri_ref, w_ref,
                   qn_ref, kcr_ref, vcr_ref, ks_ref, vsT_ref, kw_ref, vwT_ref, qf_ref, kf_ref, vfT_ref, gT_ref,
                   h_ref, carry_ref, *, tm):
    t_idx = pl.program_id(1)
    sh = mod_ref[0, 3:4, :]
    sc = mod_ref[0, 4:5, :]
    h_ref[...] = _rms_mod(x_ref[0], g_ref[...], sh, sc).astype(BF16)
    h = h_ref[...]

    lane = lax.broadcasted_iota(jnp.int32, (tm, LANES), 1)
    low = lane < HEAD_DIM
    first8 = (lane & (HEAD_DIM - 1)) < (ROPE_DIM // 2)
    ang = invf_ref[...] * pos_ref[0]
    cos8 = jnp.cos(ang)
    sin8 = jnp.sin(ang)
    rest = HEAD_DIM - ROPE_DIM
    cos_t = jnp.concatenate([cos8, cos8, jnp.ones((rest, tm), F32)] * 2, axis=0).T
    sin_t = jnp.concatenate([-sin8, sin8, jnp.zeros((rest, tm), F32)] * 2, axis=0).T

    def rope(xs):
        partner = jnp.where(first8, pltpu.roll(xs, LANES - ROPE_DIM // 2, 1), pltpu.roll(xs, ROPE_DIM // 2, 1))
        return xs * cos_t + partner * sin_t

    def split_heads(xs):
        return jnp.where(low, xs, 0.0), jnp.where(low, pltpu.roll(xs, HEAD_DIM, 1), 0.0)

    def with_ones(vt):
        extra = (lax.broadcasted_iota(jnp.int32, (V_ROWS - HEAD_DIM, vt.shape[1]), 0) == 0).astype(BF16)
        return jnp.concatenate([vt, extra], axis=0)

    sm = _mm(h, w_ref[:, _OFF_SMALL:_OFF_SMALL + LANES])
    gT_ref[0] = jax.nn.sigmoid(sm).T[:_GATE_COLS, :]
    xl = sm + bf_ref[...]
    logf = jnp.minimum(xl, 0.0) - jnp.log1p(jnp.exp(-jnp.abs(xl)))
    tri = tri_ref[...]
    l0, l1, l2 = _split3(logf)

    @pl.when(t_idx == 0)
    def _():
        carry_ref[...] = jnp.zeros_like(carry_ref)

    cf = _mm(tri, l0) + _mm(tri, l1) + _mm(tri, l2) + carry_ref[...]
    carry_ref[...] = cf[tm - 1:tm, :]
    pieces = _split3(cf * LOG2E)
    placed = _mm(pieces[0], sel_ref[0]) + _mm(pieces[1], sel_ref[1]) + _mm(pieces[2], sel_ref[2])
    q_tail = placed[:, :LANES]
    k_tail = placed[:, LANES:]

    qn = _mm(h, w_ref[:, _OFF_QN:_OFF_QN + D_NSA])
    scale = HEAD_DIM ** -0.5 * LOG2E
    for j in range(D_NSA // LANES):
        a, b = split_heads(rope(qn[:, j * LANES:(j + 1) * LANES]) * scale)
        qn_ref[0, 2 * j] = a.astype(BF16)
        qn_ref[0, 2 * j + 1] = b.astype(BF16)

    nsa_tk = vsT_ref.shape[-1]
    kv = _mm(h, w_ref[:, _OFF_KV:_OFF_KV + N_KV_STREAMS * D_KV])
    kcr_ref[0] = rope(kv[:, 0:LANES])
    vcr_ref[0] = kv[:, LANES:2 * LANES]
    blk = lax.shift_right_logical(t_idx * tm + lax.broadcasted_iota(jnp.int32, (tm, LANES), 0), SLC_BLOCK.bit_length() - 1)
    onehot = (lane == blk + HEAD_DIM).astype(F32)
    for k_out, v_out, off, tail in ((ks_ref, vsT_ref, 2 * LANES, onehot), (kw_ref, vwT_ref, 4 * LANES, None)):
        a, b = split_heads(rope(kv[:, off:off + LANES]))
        if tail is not None:
            a, b = a + tail, b + tail
        k_out[0, 0] = a.astype(BF16)
        k_out[0, 1] = b.astype(BF16)
        vT = kv[:, off + LANES:off + 2 * LANES].T.astype(BF16)
        for gi in range(N_KV_NSA):
            for c in range(tm // nsa_tk):
                v_out[0, gi, c] = with_ones(vT[gi * HEAD_DIM:(gi + 1) * HEAD_DIM, c * nsa_tk:(c + 1) * nsa_tk])

    qf = _mm(h, w_ref[:, _OFF_QF:_OFF_QF + D_FOX])
    kf = _mm(h, w_ref[:, _OFF_KF:_OFF_KF + D_FOX])
    for j in range(D_FOX // LANES):
        qa, qb = split_heads(qf[:, j * LANES:(j + 1) * LANES] * scale)
        ka, kb = split_heads(kf[:, j * LANES:(j + 1) * LANES])
        for hd, qh, kh in ((2 * j, qa, ka), (2 * j + 1, qb, kb)):
            qf_ref[0, hd] = (qh + q_tail + oneq_ref[hd:hd + 1, :]).astype(BF16)
            kf_ref[0, hd] = (kh + k_tail + onek_ref[hd:hd + 1, :]).astype(BF16)
    vf = _mm(h, w_ref[:, _OFF_VF:_OFF_VF + D_FOX])
    fox_tk = vfT_ref.shape[-1]
    for j in range(D_FOX // LANES):
        vT = vf[:, j * LANES:(j + 1) * LANES].T.astype(BF16)
        for hh in range(2):
            for c in range(tm // fox_tk):
                vfT_ref[0, 2 * j + hh, c] = with_ones(vT[hh * HEAD_DIM:(hh + 1) * HEAD_DIM, c * fox_tk:(c + 1) * fox_tk])


def _inproj(x, mod3, g, pos_row, invf, bf_row, sel, oneq, onek, tri, w, *, tm, nsa_tk, fox_tk):
    B, T, D = x.shape
    H, G = N_HEADS_NSA, N_KV_NSA
    bt = lambda b, t: (b, t, 0)
    hb = lambda b, t: (b, 0, t, 0)
    vb = lambda b, t: (b, 0, t, 0, 0)
    out_shape = [
        jax.ShapeDtypeStruct((B, H, T, LANES), BF16),
        jax.ShapeDtypeStruct((B, T, LANES), F32),
        jax.ShapeDtypeStruct((B, T, LANES), F32),
        jax.ShapeDtypeStruct((B, G, T, LANES), BF16),
        jax.ShapeDtypeStruct((B, G, T // nsa_tk, V_ROWS, nsa_tk), BF16),
        jax.ShapeDtypeStruct((B, G, T, LANES), BF16),
        jax.ShapeDtypeStruct((B, G, T // nsa_tk, V_ROWS, nsa_tk), BF16),
        jax.ShapeDtypeStruct((B, N_HEADS_FOX, T, LANES), BF16),
        jax.ShapeDtypeStruct((B, N_HEADS_FOX, T, LANES), BF16),
        jax.ShapeDtypeStruct((B, N_HEADS_FOX, T // fox_tk, V_ROWS, fox_tk), BF16),
        jax.ShapeDtypeStruct((B, _GATE_COLS, T), F32),
    ]
    out_specs = [
        pl.BlockSpec((1, H, tm, LANES), hb),
        pl.BlockSpec((1, tm, LANES), bt),
        pl.BlockSpec((1, tm, LANES), bt),
        pl.BlockSpec((1, G, tm, LANES), hb),
        pl.BlockSpec((1, G, tm // nsa_tk, V_ROWS, nsa_tk), vb),
        pl.BlockSpec((1, G, tm, LANES), hb),
        pl.BlockSpec((1, G, tm // nsa_tk, V_ROWS, nsa_tk), vb),
        pl.BlockSpec((1, N_HEADS_FOX, tm, LANES), hb),
        pl.BlockSpec((1, N_HEADS_FOX, tm, LANES), hb),
        pl.BlockSpec((1, N_HEADS_FOX, tm // fox_tk, V_ROWS, fox_tk), vb),
        pl.BlockSpec((1, _GATE_COLS, tm), lambda b, t: (b, 0, t)),
    ]
    row = lambda b, t: (0, 0)
    return pl.pallas_call(
        functools.partial(_inproj_kernel, tm=tm),
        grid=(B, T // tm),
        in_specs=[
            pl.BlockSpec((1, tm, D), bt),
            pl.BlockSpec((1, N_MOD, D), lambda b, t: (b, 0, 0)),
            pl.BlockSpec((1, D), row),
            pl.BlockSpec((1, 1, tm), lambda b, t: (b, 0, t)),
            pl.BlockSpec((ROPE_DIM // 2, tm), row),
            pl.BlockSpec((1, LANES), row),
            pl.BlockSpec((3, LANES, 2 * LANES), lambda b, t: (0, 0, 0)),
            pl.BlockSpec((N_HEADS_FOX, LANES), row),
            pl.BlockSpec((N_HEADS_FOX, LANES), row),
            pl.BlockSpec((tm, tm), row),
            pl.BlockSpec((D, _W_IN_COLS), row, pipeline_mode=pl.Buffered(1)),
        ],
        out_specs=out_specs,
        out_shape=out_shape,
        scratch_shapes=[pltpu.VMEM((tm, D), BF16), pltpu.VMEM((1, LANES), F32)],
        compiler_params=_params(("arbitrary", "arbitrary")),
        name="inproj",
    )(x, mod3, g.reshape(1, D), pos_row, invf, bf_row, sel, oneq, onek, tri, w)


def _gelu_tanh(x):
    c = np.float32(np.sqrt(2.0 / np.pi))
    return x * (0.5 * (1.0 + jnp.tanh(c * (x + 0.044715 * (x * x * x)))))


def _compress_kernel(zk_ref, zv_ref, pe_ref, wkt_ref, wkb_ref, wvt_ref, wvb_ref, w2k_ref, w2v_ref, ovT_ref, kc_ref, lhs_ref):
    nsub = zk_ref.shape[1] // CMP_STRIDE

    def mlp(z_ref, pe_top, pe_bot, wt_ref, wb_ref):
        a = b = None
        for j in range(CMP_STRIDE):
            xj = z_ref[0, pl.ds(j, nsub, stride=CMP_STRIDE), :]
            lanes = slice(j * LANES, (j + 1) * LANES)
            aj = _mm((xj + pe_ref[pe_top:pe_top + 1, lanes]).astype(BF16), wt_ref[lanes, :])
            bj = _mm((xj + pe_ref[pe_bot:pe_bot + 1, lanes]).astype(BF16), wb_ref[lanes, :])
            a = aj if a is None else a + aj
            b = bj if b is None else b + bj
        return _gelu_tanh(a + pltpu.roll(b, nsub - 1, 0))

    hk = mlp(zk_ref, 0, 1, wkt_ref, wkb_ref)
    hv = mlp(zv_ref, 2, 3, wvt_ref, wvb_ref)
    for gi in range(N_KV_NSA):
        kc_ref[0, gi] = _mm(hk[:, gi * CMP_HIDDEN:(gi + 1) * CMP_HIDDEN].astype(BF16), w2k_ref[...]).astype(BF16)
        vc = _mm(hv[:, gi * CMP_HIDDEN:(gi + 1) * CMP_HIDDEN].astype(BF16), w2v_ref[...])
        ones = (lax.broadcasted_iota(jnp.int32, (V_ROWS - HEAD_DIM, nsub), 0) == 0).astype(BF16)
        lhs_ref[0, gi] = jnp.concatenate([vc.T[:HEAD_DIM, :].astype(BF16), ones, ovT_ref[...]], axis=0)


def _compress(zk, zv, pe4, wkt, wkb, wvt, wvb, w2k, w2v, ovT):
    B, T, _ = zk.shape
    nsub = T // CMP_STRIDE
    zc = CMP_STRIDE * LANES
    G = N_KV_NSA
    c2 = lambda b: (0, 0)
    zspec = pl.BlockSpec((1, T, LANES), lambda b: (b, 0, 0))
    wspec = pl.BlockSpec((zc, G * CMP_HIDDEN), c2)
    w2spec = pl.BlockSpec((CMP_HIDDEN, LANES), c2)
    return pl.pallas_call(
        _compress_kernel,
        grid=(B,),
        in_specs=[zspec, zspec, pl.BlockSpec((4, zc), c2), wspec, wspec, wspec, wspec, w2spec, w2spec,
                  pl.BlockSpec(ovT.shape, c2)],
        out_specs=[
            pl.BlockSpec((1, G, nsub, LANES), lambda b: (b, 0, 0, 0)),
            pl.BlockSpec((1, G, V_ROWS + ovT.shape[0], nsub), lambda b: (b, 0, 0, 0)),
        ],
        out_shape=[
            jax.ShapeDtypeStruct((B, G, nsub, LANES), BF16),
            jax.ShapeDtypeStruct((B, G, V_ROWS + ovT.shape[0], nsub), BF16),
        ],
        compiler_params=_params(("arbitrary",)),
        name="compress",
    )(zk, zv, pe4, wkt, wkb, wvt, wvb, w2k, w2v, ovT)


_SCORE_PAD = LANES


def _flash_init(m_ref, acc_ref):
    m_ref[...] = jnp.full_like(m_ref, NEG_INF)
    acc_ref[...] = jnp.zeros_like(acc_ref)


def _chain_softmax(s_list, m_ref, tile_max=None):
    m_old = m_ref[...]
    m_new = m_old
    if tile_max is not None:
        m_new = jnp.maximum(m_new, tile_max)
    else:
        for s in s_list:
            m_new = jnp.maximum(m_new, jnp.max(s, axis=0, keepdims=True))
    m_ref[...] = m_new
    return jnp.exp2(m_old - m_new), [jnp.exp2(s - m_new).astype(BF16) for s in s_list]


def _chain_values(staged, v_list, acc_ref):
    alpha, ps = staged
    pv = _mm(v_list[0], ps[0])
    for v, p in zip(v_list[1:], ps[1:]):
        pv = pv + _mm(v, p)
    acc_ref[...] = alpha * acc_ref[...] + pv


def _normalized(acc):
    return acc[:HEAD_DIM] / acc[HEAD_DIM:HEAD_DIM + 1]


def _pipelined_sweep(n, n_chains, store, softmax, values):
    chains = range(n_chains)

    def by_parity(i, fn):
        @pl.when((i & 1) == 0)
        def _():
            fn(0)

        @pl.when((i & 1) == 1)
        def _():
            fn(1)

    def step(nxt, masked, cur, par):
        staged = None
        for c in chains:
            if nxt is not None:
                store(nxt, 1 - par, masked, c)
            new = softmax(cur, par, c)
            if staged is not None:
                values(cur, staged, c - 1)
            staged = new
        values(cur, staged, n_chains - 1)

    def first_scores(masked):
        for c in chains:
            store(0, 0, masked, c)

    @pl.when(n == 0)
    def _():
        first_scores(True)
        step(None, False, 0, 0)

    @pl.when(n == 1)
    def _():
        first_scores(False)
        step(1, True, 0, 0)
        step(None, False, 1, 1)

    @pl.when(n >= 2)
    def _():
        first_scores(False)
        step(1, False, 0, 0)
        n_mid = n - 2

        def body(j, carry):
            step(2 * j + 2, False, 2 * j + 1, 1)
            step(2 * j + 3, False, 2 * j + 2, 0)
            return carry

        lax.fori_loop(0, n_mid // 2, body, 0)

        @pl.when((n_mid & 1) == 1)
        def _():
            step(n_mid + 1, False, n_mid, 1)

        def last_two(par):
            step(n, True, n - 1, par)
            step(None, False, n, 1 - par)

        by_parity(n - 1, last_two)


_FOX_HEADS_PER_STEP = 8


def _fox_kernel(q_ref, k_ref, vT_ref, o_ref, m_ref, acc_ref, s0_ref, s1_ref, x0_ref, x1_ref, *, tq, tk):
    qi = pl.program_id(2)
    nkc = tq // tk
    heads = range(_FOX_HEADS_PER_STEP)
    s_bufs = (s0_ref, s1_ref)
    x_bufs = (x0_ref, x1_ref)
    row = lax.broadcasted_iota(jnp.int32, (tk, tq), 0)
    lane = lax.broadcasted_iota(jnp.int32, (tk, tq), 1)
    for hh in heads:
        _flash_init(m_ref.at[hh], acc_ref.at[hh])

    def store(kt, buf, masked, hh):
        tile_max = None
        for c in range(nkc):
            k0 = pl.multiple_of(kt * tq + c * tk, tk)
            s = _nt(k_ref[0, hh, pl.ds(k0, tk), :], q_ref[0, hh])
            if masked:
                s = jnp.where(row + c * tk <= lane, s, NEG_INF)
            s_bufs[buf][hh, c * tk:(c + 1) * tk, :tq] = s
            cm = jnp.max(s, axis=0, keepdims=True)
            tile_max = cm if tile_max is None else jnp.maximum(tile_max, cm)
        x_bufs[buf][hh] = tile_max

    def softmax(kt, buf, hh):
        return _chain_softmax([s_bufs[buf][hh, c * tk:(c + 1) * tk, :tq] for c in range(nkc)], m_ref.at[hh], x_bufs[buf][hh])

    def values(kt, staged, hh):
        _chain_values(staged, [vT_ref[0, hh, kt * nkc + c] for c in range(nkc)], acc_ref.at[hh])

    _pipelined_sweep(qi, len(heads), store, softmax, values)
    for pr in range(_FOX_HEADS_PER_STEP // 2):
        st = jnp.concatenate([_normalized(acc_ref[2 * pr]), _normalized(acc_ref[2 * pr + 1])], axis=0)
        o_ref[0, :, pr * LANES:(pr + 1) * LANES] = st.T.astype(BF16)


def _fox(qf, kf, vfT, *, tq):
    B, H, T, _ = qf.shape
    tk = vfT.shape[-1]
    nh = _FOX_HEADS_PER_STEP
    return pl.pallas_call(
        functools.partial(_fox_kernel, tq=tq, tk=tk),
        grid=(B, H // nh, T // tq),
        in_specs=[
            pl.BlockSpec((1, nh, tq, LANES), lambda b, p, i: (b, p, i, 0)),
            pl.BlockSpec((1, nh, T, LANES), lambda b, p, i: (b, p, 0, 0)),
            pl.BlockSpec((1, nh, T // tk, V_ROWS, tk), lambda b, p, i: (b, p, 0, 0, 0)),
        ],
        out_specs=pl.BlockSpec((1, tq, nh * HEAD_DIM), lambda b, p, i: (b, i, p)),
        out_shape=jax.ShapeDtypeStruct((B, T, H * HEAD_DIM), BF16),
        scratch_shapes=[
            pltpu.VMEM((nh, 1, tq), F32),
            pltpu.VMEM((nh, V_ROWS, tq), F32),
            pltpu.VMEM((nh, tq, tq + _SCORE_PAD), F32),
            pltpu.VMEM((nh, tq, tq + _SCORE_PAD), F32),
            pltpu.VMEM((nh, 1, tq), F32),
            pltpu.VMEM((nh, 1, tq), F32),
        ],
        compiler_params=_params(("arbitrary", "arbitrary", "arbitrary")),
        name="fox",
    )(qf, kf, vfT)


def _softmax_pv(s_lists, v_lists):
    staged = []
    for s_list in s_lists:
        m = jnp.max(s_list[0], axis=0, keepdims=True)
        for s in s_list[1:]:
            m = jnp.maximum(m, jnp.max(s, axis=0, keepdims=True))
        staged.append([jnp.exp2(s - m).astype(BF16) for s in s_list])
    outs = []
    for ps, v_list in zip(staged, v_lists):
        pv = _mm(v_list[0], ps[0])
        for v, p in zip(v_list[1:], ps[1:]):
            pv = pv + _mm(v, p)
        outs.append(_normalized(pv))
    return outs


_NSA_LANE_SPLITS = GROUP


def _nsa_kernel(q_ref, kc_ref, cmp_lhs_ref, ks_ref, vsT_ref, kw_ref, vwT_ref, g_ref, o_ref,
                m_ref, acc_ref, imp_ref, cnt_ref, qa_ref, oc_ref, ow_ref, s0_ref, s1_ref, x0_ref, x1_ref, *, tq, tk):
    qt = pl.program_id(1)
    G = N_KV_NSA
    nq = GROUP * tq
    hw = nq // _NSA_LANE_SPLITS
    heads_per_split = GROUP // _NSA_LANE_SPLITS
    ncp = kc_ref.shape[2]
    ns = cmp_lhs_ref.shape[2] - V_ROWS
    blk_shift = SLC_BLOCK.bit_length() - 1
    t_row = qt * tq + (lax.broadcasted_iota(jnp.int32, (1, nq), 1) & (tq - 1))
    ql = lax.broadcasted_iota(jnp.int32, (1, hw), 1) & (tq - 1)
    row_k = lax.broadcasted_iota(jnp.int32, (tk, hw), 0)
    causal = row_k <= ql
    n_io = lax.broadcasted_iota(jnp.int32, (ns, tq), 0)
    cur = lax.shift_right_logical(qt * tq + lax.broadcasted_iota(jnp.int32, (ns, tq), 1), blk_shift)
    forced = (n_io == 0) | (n_io == cur) | (n_io == cur - 1)
    visible = n_io <= cur
    c_io = lax.broadcasted_iota(jnp.int32, (ncp, nq), 0)
    cmp_valid = (c_io * CMP_STRIDE + (CMP_BLOCK - 1)) <= t_row
    chains = [(g, h) for g in range(G) for h in range(_NSA_LANE_SPLITS)]

    def q_of(g):
        return q_ref[0, g * GROUP:(g + 1) * GROUP].reshape(nq, LANES)

    def q_split(g, h):
        h0 = g * GROUP + h * heads_per_split
        return q_ref[0, h0:h0 + heads_per_split].reshape(hw, LANES)

    def lanes_of(h):
        return pl.ds(h * hw, hw)

    cmp_ok = cmp_valid[:, :hw]
    s_cmp = [_nt(kc_ref[0, g], q_split(g, h)) for g, h in chains]
    p_cmp = []
    for sc in s_cmp:
        smk = jnp.where(cmp_ok, sc, NEG_INF)
        p_cmp.append(jnp.exp2(smk - jnp.max(smk, axis=0, keepdims=True)).astype(BF16))
    any_valid = t_row[:, :hw] >= (CMP_BLOCK - 1)
    imp_sum = [None] * G
    for (g, h), pc in zip(chains, p_cmp):
        r_all = _mm(cmp_lhs_ref[0, g], pc)
        inv = jnp.where(any_valid, 1.0 / r_all[HEAD_DIM:HEAD_DIM + 1], 0.0)
        oc_ref[g, :, lanes_of(h)] = r_all[:HEAD_DIM] * inv
        imp_h = r_all[V_ROWS:] * inv
        imp_sum[g] = imp_h if imp_sum[g] is None else imp_sum[g] + imp_h
    for g in range(G):
        imp_ref[g] = jnp.where(forced, FORCE_SCORE, jnp.where(visible, imp_sum[g], -1.0))

    n_back = WINDOW // tk
    hq = tq // 2
    row_h = lax.broadcasted_iota(jnp.int32, (hq, hq), 0)
    lane_h = lax.broadcasted_iota(jnp.int32, (hq, hq), 1)

    @pl.when(qt >= n_back)
    def _():
        base = pl.multiple_of((qt - 1) * tk, tk)
        s_lists, v_lists, where_to = [], [], []
        for g, h in chains:
            for sub in range(2):
                q = q_ref[0, g * GROUP + h, sub * hq:(sub + 1) * hq, :]
                s3 = _nt(kw_ref[0, g, pl.ds(base + sub * hq, 3 * hq), :], q)
                s_lists.append([jnp.where(row_h > lane_h, s3[:hq], NEG_INF),
                                s3[hq:2 * hq],
                                jnp.where(row_h <= lane_h, s3[2 * hq:], NEG_INF)])
                halves = [vwT_ref[0, g, qt - 1, :, :hq], vwT_ref[0, g, qt - 1, :, hq:],
                          vwT_ref[0, g, qt, :, :hq], vwT_ref[0, g, qt, :, hq:]]
                v_lists.append(halves[sub:sub + 3])
                where_to.append((g, pl.ds(h * tq + sub * hq, hq)))
        for (g, lanes), o in zip(where_to, _softmax_pv(s_lists, v_lists)):
            ow_ref[g, :, lanes] = o

    @pl.when(qt < n_back)
    def _():
        s_lists = []
        for g, h in chains:
            q = q_split(g, h)
            s_lists.append([jnp.where((row_k + j * tk) <= (qt * tq + ql), _nt(kw_ref[0, g, j * tk:(j + 1) * tk, :], q), NEG_INF)
                            for j in range(n_back)])
        v_lists = [[vwT_ref[0, g, j] for j in range(n_back)] for g, h in chains]
        for (g, h), o in zip(chains, _softmax_pv(s_lists, v_lists)):
            ow_ref[g, :, lanes_of(h)] = o

    n_vis = jnp.minimum((qt + 1) * (tq // SLC_BLOCK), ns)
    sub = 8
    sub_io = lax.broadcasted_iota(jnp.int32, (sub, tq), 0)
    cnt_ref[...] = jnp.zeros_like(cnt_ref)
    for mb in range(ns // sub):
        @pl.when(mb * sub < n_vis)
        def _():
            for g in range(G):
                groups = [imp_ref[g, j * sub:(j + 1) * sub, :] for j in range(ns // sub)]
                counts = [cnt_ref[g, j * sub:(j + 1) * sub, :] for j in range(ns // sub)]
                for mi in range(sub):
                    row = jnp.broadcast_to(groups[mb][mi:mi + 1, :], (sub, tq))
                    for j in range(ns // sub):
                        if j < mb:
                            beats = row > groups[j]
                        elif j > mb:
                            beats = row >= groups[j]
                        else:
                            beats = (row > groups[j]) | ((row >= groups[j]) & (sub_io > mi))
                        counts[j] = counts[j] + beats.astype(F32)
                for j in range(ns // sub):
                    cnt_ref[g, j * sub:(j + 1) * sub, :] = counts[j]

    for g in range(G):
        bias = jnp.where(cnt_ref[g] < float(TOP_N), 0.0, NEG_INF)
        parts = [jnp.zeros((HEAD_DIM, tq), F32), bias]
        if ns < LANES - HEAD_DIM:
            parts.append(jnp.zeros((LANES - HEAD_DIM - ns, tq), F32))
        bias_t = jnp.concatenate(parts, axis=0).T.astype(BF16)
        qa_ref[g] = q_of(g) + jnp.concatenate([bias_t] * GROUP, axis=0)
        _flash_init(m_ref.at[g], acc_ref.at[g])

    slc_refs = [(m_ref.at[g, :, lanes_of(h)], acc_ref.at[g, :, lanes_of(h)]) for g, h in chains]

    s_bufs = (s0_ref, s1_ref)
    x_bufs = (x0_ref, x1_ref)

    def slc_store(kj, buf, masked, c):
        g, h = chains[c]
        k0 = pl.multiple_of(kj * tk, tk)
        s = _nt(ks_ref[0, g, pl.ds(k0, tk), :], qa_ref[g, lanes_of(h), :])
        if masked:
            s = jnp.where(causal, s, NEG_INF)
        s_bufs[buf][c, :, :hw] = s
        x_bufs[buf][g, :, lanes_of(h)] = jnp.max(s, axis=0, keepdims=True)

    def slc_softmax(kj, buf, c):
        g, h = chains[c]
        return _chain_softmax([s_bufs[buf][c, :, :hw]], slc_refs[c][0], x_bufs[buf][g, :, lanes_of(h)])

    def slc_values(kj, staged, c):
        _chain_values(staged, [vsT_ref[0, chains[c][0], kj]], slc_refs[c][1])

    _pipelined_sweep(qt, len(chains), slc_store, slc_softmax, slc_values)

    for g in range(G):
        def gate_row(j):
            return jnp.concatenate([g_ref[0, g, 3 * r + j:3 * r + j + 1, :] for r in range(GROUP)], axis=1)

        oT = gate_row(0) * oc_ref[g] + gate_row(1) * _normalized(acc_ref[g]) + gate_row(2) * ow_ref[g]
        for pr in range(GROUP // 2):
            st = jnp.concatenate([oT[:, (2 * pr) * tq:(2 * pr + 1) * tq], oT[:, (2 * pr + 1) * tq:(2 * pr + 2) * tq]], axis=0)
            col = (g * GROUP // 2 + pr) * LANES
            o_ref[0, :, col:col + LANES] = st.T.astype(BF16)


def _nsa(qn, kc, cmp_lhs, ks, vsT, kw, vwT, gT4, *, tq):
    B, H, T, _ = qn.shape
    G = N_KV_NSA
    tk = vsT.shape[-1]
    ncp = kc.shape[2]
    ns = cmp_lhs.shape[2] - V_ROWS
    assert tq == tk and WINDOW == tk and _NSA_LANE_SPLITS == GROUP and T >= WINDOW and ns <= LANES - HEAD_DIM and ns % 8 == 0
    nq = GROUP * tq
    kvspec = pl.BlockSpec((1, G, T, LANES), lambda b, i: (b, 0, 0, 0))
    vtspec = pl.BlockSpec((1, G, T // tk, V_ROWS, tk), lambda b, i: (b, 0, 0, 0, 0))
    return pl.pallas_call(
        functools.partial(_nsa_kernel, tq=tq, tk=tk),
        grid=(B, T // tq),
        in_specs=[
            pl.BlockSpec((1, H, tq, LANES), lambda b, i: (b, 0, i, 0)),
            pl.BlockSpec((1, G, ncp, LANES), lambda b, i: (b, 0, 0, 0)),
            pl.BlockSpec((1, G, V_ROWS + ns, ncp), lambda b, i: (b, 0, 0, 0)),
            kvspec, vtspec, kvspec, vtspec,
            pl.BlockSpec((1, G, 3 * GROUP, tq), lambda b, i: (b, 0, 0, i)),
        ],
        out_specs=pl.BlockSpec((1, tq, H * HEAD_DIM), lambda b, i: (b, i, 0)),
        out_shape=jax.ShapeDtypeStruct((B, T, H * HEAD_DIM), BF16),
        scratch_shapes=[
            pltpu.VMEM((G, 1, nq), F32),
            pltpu.VMEM((G, V_ROWS, nq), F32),
            pltpu.VMEM((G, ns, tq), F32),
            pltpu.VMEM((G, ns, tq), F32),
            pltpu.VMEM((G, nq, LANES), BF16),
            pltpu.VMEM((G, HEAD_DIM, nq), F32),
            pltpu.VMEM((G, HEAD_DIM, nq), F32),
            pltpu.VMEM((G * _NSA_LANE_SPLITS, tk, tq + _SCORE_PAD), F32),
            pltpu.VMEM((G * _NSA_LANE_SPLITS, tk, tq + _SCORE_PAD), F32),
            pltpu.VMEM((G, 1, nq), F32),
            pltpu.VMEM((G, 1, nq), F32),
        ],
        compiler_params=_params(("arbitrary", "arbitrary")),
        name="nsa",
    )(qn, kc, cmp_lhs, ks, vsT, kw, vwT, gT4)


def _mixffn_kernel(x_ref, mod_ref, g_ref, on_ref, of_ref, wgm_ref, wun_ref, wuf_ref, wo_ref,
                   g2_ref, wg_ref, wu_ref, wd_ref, gf_ref, o_ref, x2_ref, h_ref, acc_ref, *, final, ck):
    x = x_ref[0]
    D = x.shape[-1]
    sh = mod_ref[0, 3:4, :]
    sc = mod_ref[0, 4:5, :]
    ga = mod_ref[0, 5:6, :]
    h = _rms_mod(x, g_ref[...], sh, sc).astype(BF16)
    gm = _mm(h, wgm_ref[...])
    un = _mm(on_ref[0], wun_ref[...])
    uf = _mm(of_ref[0], wuf_ref[...])
    merged = jax.nn.sigmoid(gm[:, :D]) * un + jax.nn.sigmoid(gm[:, D:]) * uf
    y = _mm(merged.astype(BF16), wo_ref[...])
    x2_ref[...] = x + ga * y
    o_ref[0] = _ffn_body(x2_ref[...], mod_ref[0, 6:7, :], mod_ref[0, 7:8, :], mod_ref[0, 8:9, :], g2_ref[...],
                         wg_ref, wu_ref, wd_ref, gf_ref[...], h_ref, acc_ref, final=final, ck=ck)


def _mixffn(x, mod3, g, o_nsa, o_fox, wgm, wun, wuf, wo, g2, wg, wu, wd, g_final, *, final, tm, ck):
    B, T, D = x.shape
    F = wg.shape[1]
    assert F % ck == 0
    bt = lambda b, t: (b, t, 0)
    c2 = lambda b, t: (0, 0)
    dn = o_nsa.shape[-1]
    resident = lambda shape: pl.BlockSpec(shape, c2, pipeline_mode=pl.Buffered(1))
    return pl.pallas_call(
        functools.partial(_mixffn_kernel, final=final, ck=ck),
        grid=(B, T // tm),
        in_specs=[
            pl.BlockSpec((1, tm, D), bt),
            pl.BlockSpec((1, N_MOD, D), lambda b, t: (b, 0, 0)),
            pl.BlockSpec((1, D), c2),
            pl.BlockSpec((1, tm, dn), bt),
            pl.BlockSpec((1, tm, dn), bt),
            resident((D, 2 * D)), resident((dn, D)), resident((dn, D)), resident((D, D)),
            pl.BlockSpec((1, D), c2),
            resident((D, F)), resident((D, F)), resident((F, D)),
            pl.BlockSpec((1, D), c2),
        ],
        out_specs=pl.BlockSpec((1, tm, D), bt),
        out_shape=jax.ShapeDtypeStruct((B, T, D), F32),
        scratch_shapes=[pltpu.VMEM((tm, D), F32), pltpu.VMEM((tm, D), BF16), pltpu.VMEM((tm, D), F32)],
        compiler_params=_params(("arbitrary", "arbitrary")),
        name="mixffn",
    )(x, mod3, g.reshape(1, D), o_nsa, o_fox, wgm, wun, wuf, wo, g2.reshape(1, D), wg, wu, wd, g_final.reshape(1, D))


def _compress_weights(pe, w1, w2):
    half = CMP_BLOCK // 2
    eye = jnp.eye(N_KV_NSA, dtype=F32)

    def expand(w_half):
        w3 = w_half.reshape(half, HEAD_DIM, CMP_HIDDEN)
        return jnp.einsum('jdn,gh->jgdhn', w3, eye).reshape(half * N_KV_NSA * HEAD_DIM, N_KV_NSA * CMP_HIDDEN).astype(BF16)

    def pe_row(pe_half):
        return jnp.broadcast_to(pe_half[:, None, :], (half, N_KV_NSA, HEAD_DIM)).reshape(1, -1)

    w2p = jnp.pad(w2, ((0, 0), (0, LANES - HEAD_DIM))).astype(BF16)
    return (pe_row(pe[:half]), pe_row(pe[half:]), expand(w1[:half * HEAD_DIM]), expand(w1[half * HEAD_DIM:]), w2p)


def kernel(x, c, positions, w_ada, b_ada, g_ffn1, w_gate1, w_up1, w_down1, g_mix, w_in, b_forget, pe_ck, w1_ck, w2_ck, pe_cv, w1_cv, w2_cv, w_up_nsa, w_up_fox, w_o, g_ffn2, w_gate2, w_up2, w_down2, g_final):
    B, T, D = x.shape
    depth = w_ada.shape[0]
    tm = 512
    tm_dense = 1024 if T % 1024 == 0 else tm
    ffn_ck = 256
    nsa_tq, nsa_tk = 512, 512
    fox_tq, fox_tk = 512, 512
    n_slc = T // SLC_BLOCK
    n_sub = T // CMP_STRIDE

    half = ROPE_DIM // 2
    inv_freq = ROPE_THETA ** (-jnp.arange(half, dtype=F32) / half)
    invf = jnp.broadcast_to(inv_freq[:, None], (half, tm))
    cmp_start = np.arange(n_sub) * CMP_STRIDE
    slc_start = np.arange(n_slc) * SLC_BLOCK
    ov = ((cmp_start[:, None] < slc_start[None, :] + SLC_BLOCK) & (slc_start[None, :] < cmp_start[:, None] + CMP_BLOCK))
    ov[n_sub - CMP_BLOCK // CMP_STRIDE + 1:, :] = False
    ovT = jnp.asarray(ov.T, dtype=BF16)

    pos_row = positions.astype(F32)[:, None, :]
    sel_np = np.zeros((3, LANES, 2 * LANES), np.float32)
    oneq_np = np.zeros((N_HEADS_FOX, LANES), np.float32)
    onek_np = np.zeros((N_HEADS_FOX, LANES), np.float32)
    for hd in range(N_HEADS_FOX):
        for j in range(3):
            sel_np[j, _GATE_COLS + hd, _TAIL + 3 * hd + j] = 1.0
            sel_np[j, _GATE_COLS + hd, LANES + _TAIL2 + 3 * hd + j] = -1.0
            oneq_np[hd, _TAIL2 + 3 * hd + j] = 1.0
            onek_np[hd, _TAIL + 3 * hd + j] = 1.0
    sel, oneq, onek = jnp.asarray(sel_np, BF16), jnp.asarray(oneq_np), jnp.asarray(onek_np)
    tri = jnp.asarray(np.tril(np.ones((tm, tm), np.float32)), BF16)
    c_in = c
    for l in range(depth):
        mod3 = _ada(c_in, w_ada[l], b_ada[l]).reshape(B, N_MOD, D)
        wg1, wu1, wd1 = w_gate1[l].astype(BF16), w_up1[l].astype(BF16), w_down1[l].astype(BF16)
        wg2, wu2, wd2 = w_gate2[l].astype(BF16), w_up2[l].astype(BF16), w_down2[l].astype(BF16)
        wl = w_in[l]
        c0 = D_NSA + N_KV_STREAMS * D_KV
        c1 = c0 + _GATE_COLS
        c2 = c1 + 3 * D_FOX
        c3 = c2 + N_HEADS_FOX
        small = jnp.concatenate([wl[:, c0:c1], wl[:, c2:c3], jnp.zeros((D, LANES - _GATE_COLS - N_HEADS_FOX), F32)], axis=1)
        w_proj = jnp.concatenate([wl[:, :c0], wl[:, c1:c2], small], axis=1).astype(BF16)
        w_gm = wl[:, c3:].astype(BF16)
        bf_row = jnp.zeros((1, LANES), F32).at[0, _GATE_COLS:_GATE_COLS + N_HEADS_FOX].set(b_forget[l])

        x = _ffn(x, mod3, g_ffn1[l], wg1, wu1, wd1, g_final, k_mod=0, final=False, tm=tm_dense, ck=ffn_ck)

        (qn, kcr, vcr, ks, vsT, kw, vwT, qf, kf, vfT, gT) = _inproj(
            x, mod3, g_mix[l], pos_row, invf, bf_row, sel, oneq, onek, tri, w_proj, tm=tm, nsa_tk=nsa_tk, fox_tk=fox_tk)

        pk_t, pk_b, wk_t, wk_b, w2k = _compress_weights(pe_ck[l], w1_ck[l], w2_ck[l])
        pv_t, pv_b, wv_t, wv_b, w2v = _compress_weights(pe_cv[l], w1_cv[l], w2_cv[l])
        pe4 = jnp.concatenate([pk_t, pk_b, pv_t, pv_b], axis=0)
        kc, cmp_lhs = _compress(kcr, vcr, pe4, wk_t, wk_b, wv_t, wv_b, w2k, w2v, ovT)

        o_fox = _fox(qf, kf, vfT, tq=fox_tq)
        gT4 = gT.reshape(B, N_KV_NSA, 3 * GROUP, T)
        o_nsa = _nsa(qn, kc, cmp_lhs, ks, vsT, kw, vwT, gT4, tq=nsa_tq)

        last = l == depth - 1
        x = _mixffn(x, mod3, g_mix[l], o_nsa, o_fox, w_gm, w_up_nsa[l].astype(BF16), w_up_fox[l].astype(BF16),
                    w_o[l].astype(BF16), g_ffn2[l], wg2, wu2, wd2, g_final, final=last, tm=tm_dense, ck=ffn_ck)
    return x
```
